```python
import math
import jax, jax.numpy as jnp
from jax import lax
import numpy as np

D_MODEL = 1024
BATCH = 4
SEQ = 8192
DEPTH = 2

GRID_W = 64
D_MIX = D_MODEL
GDN_WIDTH = D_MIX // 2
NA_WIDTH = D_MIX - GDN_WIDTH
GDN_HEAD_DIM = 128
GDN_HEADS = GDN_WIDTH // GDN_HEAD_DIM
NA_HEAD_DIM = 64
NA_HEADS = NA_WIDTH // NA_HEAD_DIM
CONV_WIDTH = 5
GDN_CHUNK = 64
GDN_CONV_CH = 3 * GDN_WIDTH
NA_WIN_R = 8
NA_WIN_C = 16
NA_QBLK_C = 16
NA_KSPAN_C = NA_QBLK_C + NA_WIN_C
D_FF = 4 * D_MODEL
DEEPNORM_ALPHA = (2 * DEPTH) ** 0.25
DEEPNORM_BETA = (8 * DEPTH) ** -0.25
LN_EPS = 1e-5
RMS_EPS = 1e-6
SPLIT_POINTS = (3 * GDN_WIDTH, 4 * GDN_WIDTH, 4 * GDN_WIDTH + 2 * GDN_HEADS, 4 * GDN_WIDTH + 4 * GDN_HEADS)
D_IN = 4 * GDN_WIDTH + 4 * GDN_HEADS + 3 * NA_WIDTH

kernel_name = "hybrid_gdn_natten_deepnorm_encoder"


def layer_norm(x, g, b):
    xf = x.astype(jnp.float32)
    mu = jnp.mean(xf, axis=-1, keepdims=True)
    var = jnp.mean(jnp.square(xf - mu), axis=-1, keepdims=True)
    y = (xf - mu) * lax.rsqrt(var + LN_EPS)
    return (y * g + b).astype(x.dtype)


def rms_norm(x, g):
    xf = x.astype(jnp.float32)
    y = xf * lax.rsqrt(jnp.mean(jnp.square(xf), axis=-1, keepdims=True) + RMS_EPS)
    return y * g


def l2_normalize(x):
    xf = x.astype(jnp.float32)
    return xf * lax.rsqrt(jnp.sum(jnp.square(xf), axis=-1, keepdims=True) + RMS_EPS)


def short_conv(x, w):
    pad = CONV_WIDTH // 2
    T = x.shape[1]
    xp = jnp.pad(x, ((0, 0), (pad, pad), (0, 0)))
    return sum(xp[:, i:i + T] * w[i] for i in range(CONV_WIDTH))


def gated_delta_chunked(q, k, v, g, beta):
    out_dtype = v.dtype
    B, T, H, dk = q.shape
    dv = v.shape[-1]
    C = GDN_CHUNK
    N = T // C
    f32 = jnp.float32

    def chunk(t):
        t = t.astype(f32).reshape((B, N, C, H) + t.shape[3:])
        return jnp.moveaxis(t, 3, 1)

    q = chunk(q) * (dk ** -0.5)
    k = chunk(k)
    v = chunk(v)
    g = chunk(g)
    beta = chunk(beta)
    gc = jnp.cumsum(g, axis=-1)
    lower = jnp.tril(jnp.ones((C, C), dtype=bool))
    strict = jnp.tril(jnp.ones((C, C), dtype=bool), -1)
    gamma = jnp.exp(jnp.where(lower, gc[..., :, None] - gc[..., None, :], -jnp.inf))
    kb = k * beta[..., None]
    a_mat = jnp.where(strict, jnp.einsum('bhncd,bhnsd->bhncs', kb, k) * gamma, 0.0)
    t_mat = a_mat + jnp.eye(C, dtype=f32)
    u = lax.linalg.triangular_solve(t_mat, v * beta[..., None], left_side=True, lower=True, unit_diagonal=True)
    w = lax.linalg.triangular_solve(t_mat, kb * jnp.exp(gc)[..., None], left_side=True, lower=True, unit_diagonal=True)
    qk = jnp.einsum('bhncd,bhnsd->bhncs', q, k) * gamma
    q_dec = q * jnp.exp(gc)[..., None]
    k_dec = k * jnp.exp(gc[..., -1:] - gc)[..., None]
    g_last = jnp.exp(gc[..., -1])
    xs = tuple(jnp.moveaxis(t, 2, 0) for t in (q_dec, k_dec, u, w, qk, g_last))

    def step(S, inp):
        qd, kd, u_n, w_n, qk_n, gl = inp
        v_new = u_n - jnp.einsum('bhck,bhkv->bhcv', w_n, S)
        o = jnp.einsum('bhck,bhkv->bhcv', qd, S) + jnp.einsum('bhcs,bhsv->bhcv', qk_n, v_new)
        S = S * gl[..., None, None] + jnp.einsum('bhck,bhcv->bhkv', kd, v_new)
        return S, o

    S0 = jnp.zeros((B, H, dk, dv), f32)
    _, o = lax.scan(step, S0, xs)
    o = jnp.transpose(o, (1, 0, 3, 2, 4)).reshape(B, T, H, dv)
    return o.astype(out_dtype)


def gdn_mixer(h_qkv, h_z, h_a, h_b, conv_w, a_log, dt_bias, norm_g):
    B, T, _ = h_qkv.shape
    qkv = jax.nn.silu(short_conv(h_qkv, conv_w))
    q, k, v = jnp.split(qkv, 3, axis=-1)
    q = l2_normalize(q.reshape(B, T, GDN_HEADS, GDN_HEAD_DIM))
    k = l2_normalize(k.reshape(B, T, GDN_HEADS, GDN_HEAD_DIM))
    v = v.reshape(B, T, GDN_HEADS, GDN_HEAD_DIM)
    a = h_a.astype(jnp.float32).reshape(B, T, 2, GDN_HEADS)
    b = h_b.astype(jnp.float32).reshape(B, T, 2, GDN_HEADS)
    g = -jnp.exp(a_log.astype(jnp.float32)) * jax.nn.softplus(a + dt_bias.astype(jnp.float32))
    beta = jax.nn.sigmoid(b)
    q2 = jnp.concatenate([q, q[:, ::-1]], axis=0)
    k2 = jnp.concatenate([k, k[:, ::-1]], axis=0)
    v2 = jnp.concatenate([v, v[:, ::-1]], axis=0)
    g2 = jnp.concatenate([g[:, :, 0], g[:, ::-1, 1]], axis=0)
    beta2 = jnp.concatenate([beta[:, :, 0], beta[:, ::-1, 1]], axis=0)
    o2 = gated_delta_chunked(q2, k2, v2, g2, beta2)
    o = o2[:B].astype(jnp.float32) + o2[B:, ::-1].astype(jnp.float32)
    z = h_z.reshape(B, T, GDN_HEADS, GDN_HEAD_DIM).astype(jnp.float32)
    o = rms_norm(o, norm_g) * jax.nn.silu(z)
    return o.reshape(B, T, GDN_WIDTH).astype(h_qkv.dtype)


def _na_column_tables():
    n_blk = GRID_W // NA_QBLK_C
    j = np.arange(n_blk)
    span_start = np.clip(j * NA_QBLK_C - NA_WIN_C // 2, 0, GRID_W - NA_KSPAN_C)
    key_cols = span_start[:, None] + np.arange(NA_KSPAN_C)[None, :]
    q_cols = j[:, None] * NA_QBLK_C + np.arange(NA_QBLK_C)[None, :]
    win_start = np.clip(q_cols - NA_WIN_C // 2, 0, GRID_W - NA_WIN_C)
    kc = key_cols[:, None, :]
    in_win = (kc >= win_start[:, :, None]) & (kc < win_start[:, :, None] + NA_WIN_C)
    rel_idx = np.clip(kc - q_cols[:, :, None] + NA_WIN_C - 1, 0, 2 * NA_WIN_C - 2)
    return key_cols, in_win, rel_idx


def neighborhood_attention(q, k, v, rpb):
    B, T, H, d = q.shape
    rows = T // GRID_W
    kr = min(NA_WIN_R, rows)
    n_blk = GRID_W // NA_QBLK_C
    key_cols, in_win, rel_idx = _na_column_tables()
    qg = q.reshape(B, rows, n_blk, NA_QBLK_C, H, d)
    kg = k.reshape(B, rows, GRID_W, H, d)
    vg = v.reshape(B, rows, GRID_W, H, d)
    scale = d ** -0.5
    mask = jnp.asarray(in_win)[None, None, :, :, None, :]
    rpb_cols = rpb.astype(jnp.float32)[:, :, rel_idx]

    def one_row(r):
        r0 = jnp.clip(r - kr // 2, 0, rows - kr)
        k_band = lax.dynamic_slice_in_dim(kg, r0, kr, axis=1)[:, :, key_cols]
        v_band = lax.dynamic_slice_in_dim(vg, r0, kr, axis=1)[:, :, key_cols]
        q_row = lax.dynamic_index_in_dim(qg, r, axis=1, keepdims=False)
        s = jnp.einsum('bjqhd,brjkhd->bhjqrk', q_row, k_band).astype(jnp.float32) * scale
        dr = r0 + jnp.arange(kr) - r + NA_WIN_R - 1
        bias = jnp.take(rpb_cols, dr, axis=1)
        s = s + jnp.transpose(bias, (0, 2, 3, 1, 4))[None]
        s = jnp.where(mask, s, -jnp.inf)
        p = jax.nn.softmax(s.reshape(s.shape[:4] + (kr * NA_KSPAN_C,)), axis=-1).reshape(s.shape)
        o = jnp.einsum('bhjqrk,brjkhd->bjqhd', p.astype(v.dtype), v_band)
        return o.reshape(B, GRID_W, H, d)

    out = lax.map(one_row, jnp.arange(rows))
    return jnp.transpose(out, (1, 0, 2, 3, 4)).reshape(B, T, H, d)


def setup_inputs(seed: int = 0) -> dict:
    key = jax.random.key(seed)
    ks = jax.random.split(key, 20)
    f32 = jnp.float32

    def nrm(k, shape, scale):
        return jax.random.normal(k, shape, f32) * scale

    x = nrm(ks[0], (BATCH, SEQ, D_MODEL), 1.0)
    ln_in_g = 1.0 + nrm(ks[1], (D_MODEL,), 0.02)
    ln_in_b = nrm(ks[2], (D_MODEL,), 0.02)
    w_in = nrm(ks[3], (DEPTH, D_MODEL, D_IN), D_MODEL ** -0.5)
    conv_w = nrm(ks[4], (DEPTH, CONV_WIDTH, GDN_CONV_CH), CONV_WIDTH ** -0.5)
    a_log = jnp.log(jax.random.uniform(ks[5], (DEPTH, 2, GDN_HEADS), f32, 1.0, 16.0))
    dt = jnp.exp(jax.random.uniform(ks[6], (DEPTH, 2, GDN_HEADS), f32, math.log(1e-3), math.log(1e-1)))
    dt_bias = dt + jnp.log(-jnp.expm1(-dt))
    gdn_norm_g = 1.0 + nrm(ks[7], (DEPTH, GDN_HEAD_DIM), 0.02)
    rpb = nrm(ks[8], (DEPTH, NA_HEADS, 2 * NA_WIN_R - 1, 2 * NA_WIN_C - 1), 0.1)
    na_norm_g = 1.0 + nrm(ks[9], (DEPTH, NA_HEAD_DIM), 0.02)
    w_out = nrm(ks[10], (DEPTH, D_MIX, D_MODEL), D_MIX ** -0.5 * DEEPNORM_BETA)
    ln1_g = 1.0 + nrm(ks[11], (DEPTH, D_MODEL), 0.02)
    ln1_b = nrm(ks[12], (DEPTH, D_MODEL), 0.02)
    w1 = nrm(ks[13], (DEPTH, D_MODEL, D_FF), D_MODEL ** -0.5)
    b1 = nrm(ks[14], (DEPTH, D_FF), 0.01)
    w2 = nrm(ks[15], (DEPTH, D_FF, D_MODEL), D_FF ** -0.5 * DEEPNORM_BETA)
    b2 = nrm(ks[16], (DEPTH, D_MODEL), 0.01)
    ln2_g = 1.0 + nrm(ks[17], (DEPTH, D_MODEL), 0.02)
    ln2_b = nrm(ks[18], (DEPTH, D_MODEL), 0.02)
    return {"x": x, "ln_in_g": ln_in_g, "ln_in_b": ln_in_b, "w_in": w_in, "conv_w": conv_w,
            "a_log": a_log, "dt_bias": dt_bias, "gdn_norm_g": gdn_norm_g, "rpb": rpb,
            "na_norm_g": na_norm_g, "w_out": w_out, "ln1_g": ln1_g, "ln1_b": ln1_b,
            "w1": w1, "b1": b1, "w2": w2, "b2": b2, "ln2_g": ln2_g, "ln2_b": ln2_b}


def reference(x, ln_in_g, ln_in_b, w_in, conv_w, a_log, dt_bias, gdn_norm_g, rpb,
              na_norm_g, w_out, ln1_g, ln1_b, w1, b1, w2, b2, ln2_g, ln2_b):
    B, T, _ = x.shape
    x = layer_norm(x, ln_in_g, ln_in_b)
    for l in range(DEPTH):
        h = jnp.einsum('btd,de->bte', x, w_in[l])
        h_qkv, h_z, h_a, h_b, h_na = jnp.split(h, SPLIT_POINTS, axis=-1)
        o_gdn = gdn_mixer(h_qkv, h_z, h_a, h_b, conv_w[l], a_log[l], dt_bias[l], gdn_norm_g[l])
        q_na, k_na, v_na = (t.reshape(B, T, NA_HEADS, NA_HEAD_DIM) for t in jnp.split(h_na, 3, axis=-1))
        o_na = neighborhood_attention(q_na, k_na, v_na, rpb[l])
        o_na = rms_norm(o_na, na_norm_g[l]).reshape(B, T, NA_WIDTH).astype(x.dtype)
        mix = jnp.einsum('bte,ed->btd', jnp.concatenate([o_gdn, o_na], axis=-1), w_out[l])
        x = layer_norm(DEEPNORM_ALPHA * x + mix, ln1_g[l], ln1_b[l])
        ff = jnp.square(jax.nn.relu(jnp.einsum('btd,df->btf', x, w1[l]) + b1[l]))
        ff = jnp.einsum('btf,fd->btd', ff, w2[l]) + b2[l]
        x = layer_norm(DEEPNORM_ALPHA * x + ff, ln2_g[l], ln2_b[l])
    return x
```

```python
import functools
import math

import jax
import jax.numpy as jnp
import numpy as np
from jax import lax
from jax.experimental import pallas as pl
from jax.experimental.pallas import tpu as pltpu

F32 = jnp.float32
BF16 = jnp.bfloat16

GRID_W = 64
GDN_HEAD_DIM = 128
GDN_HEADS = 4
GDN_WIDTH = GDN_HEADS * GDN_HEAD_DIM
NA_HEAD_DIM = 64
NA_HEADS = 8
NA_WIDTH = NA_HEADS * NA_HEAD_DIM
CONV_WIDTH = 5
GDN_CHUNK = 64
NA_WIN_R = 8
NA_WIN_C = 16
DEPTH = 2
DEEPNORM_ALPHA = (2 * DEPTH) ** 0.25
LN_EPS = 1e-5
RMS_EPS = 1e-6

LANES = 128
VMEM_LIMIT = 56 * 1024 * 1024

TM_PROJ = 512
TM_FFN = 512
TT_SCAN = 512
CONV_RB = 512
FF_CHUNK = 1024

C_QKV = 3 * GDN_WIDTH
C_Z = GDN_WIDTH
C_NA = 3 * NA_WIDTH
OFF_Z = C_QKV
OFF_GF = OFF_Z + C_Z
OFF_GB = OFF_GF + LANES
OFF_NA = OFF_GB + LANES
C_TOTAL = OFF_NA + C_NA


def _cparams(sem):
    return pltpu.CompilerParams(dimension_semantics=sem, vmem_limit_bytes=VMEM_LIMIT)


def _const_spec(shape):
    nd = len(shape)
    return pl.BlockSpec(shape, lambda *_: (0,) * nd, pipeline_mode=pl.Buffered(1))


def _layer_norm(y, g, b):
    mu = jnp.mean(y, axis=-1, keepdims=True)
    yc = y - mu
    var = jnp.mean(yc * yc, axis=-1, keepdims=True)
    return yc * lax.rsqrt(var + LN_EPS) * g + b


def _sigmoid(x):
    return 1.0 / (1.0 + jnp.exp(-x))


def _chunk_cumsum(g, tpos, reverse):
    n = g.shape[0]
    s = 1
    while s < GDN_CHUNK:
        if reverse:
            g = g + jnp.where(tpos < GDN_CHUNK - s, pltpu.roll(g, n - s, 0), 0.0)
        else:
            g = g + jnp.where(tpos >= s, pltpu.roll(g, s, 0), 0.0)
        s *= 2
    return g


def _inproj_kernel(x_ref, lng_ref, lnb_ref, w_ref, alog_ref, dtb_ref, *out_refs, apply_ln):
    if apply_ln:
        xn_ref, qkv_ref, z_ref, gates_ref, na_ref = out_refs
    else:
        qkv_ref, z_ref, gates_ref, na_ref = out_refs
    x = x_ref[...]
    if apply_ln:
        x = _layer_norm(x, lng_ref[...], lnb_ref[...])
        xn_ref[...] = x
    xb = x.astype(BF16)
    qkv_ref[...] = jnp.dot(xb, w_ref[:, 0:C_QKV], preferred_element_type=F32).astype(BF16)
    z_ref[...] = jnp.dot(xb, w_ref[:, OFF_Z:OFF_Z + C_Z], preferred_element_type=F32).astype(BF16)
    na_ref[...] = jnp.dot(xb, w_ref[:, OFF_NA:OFF_NA + C_NA], preferred_element_type=F32).astype(BF16)

    tm = x.shape[0]
    tpos = lax.broadcasted_iota(jnp.int32, (tm, LANES), 0) & (GDN_CHUNK - 1)
    lane = lax.broadcasted_iota(jnp.int32, (tm, LANES), 1)
    for d in range(2):
        off = OFF_GF if d == 0 else OFF_GB
        hab = jnp.dot(xb, w_ref[:, off:off + LANES], preferred_element_type=F32)
        sp_in = hab + dtb_ref[d:d + 1, :]
        softplus = jnp.maximum(sp_in, 0.0) + jnp.log1p(jnp.exp(-jnp.abs(sp_in)))
        g = -jnp.exp(alog_ref[d:d + 1, :]) * softplus
        g = jnp.where(lane < GDN_HEADS, g, 0.0)
        gc = _chunk_cumsum(g, tpos, reverse=(d == 1))
        gates_ref[d] = jnp.where(lane < GDN_HEADS, gc, _sigmoid(hab))


def _inproj(x2d, lng, lnb, w, alog_rows, dtb_rows, apply_ln):
    bt, dm = x2d.shape
    tm = TM_PROJ
    grid = (bt // tm,)
    row = lambda i: (i, 0)
    in_specs = [
        pl.BlockSpec((tm, dm), row),
        _const_spec((1, dm)), _const_spec((1, dm)),
        _const_spec((dm, C_TOTAL)),
        _const_spec((2, LANES)), _const_spec((2, LANES)),
    ]
    out_shape = [
        jax.ShapeDtypeStruct((bt, C_QKV), BF16),
        jax.ShapeDtypeStruct((bt, C_Z), BF16),
        jax.ShapeDtypeStruct((2, bt, LANES), F32),
        jax.ShapeDtypeStruct((bt, C_NA), BF16),
    ]
    out_specs = [
        pl.BlockSpec((tm, C_QKV), row),
        pl.BlockSpec((tm, C_Z), row),
        pl.BlockSpec((2, tm, LANES), lambda i: (0, i, 0)),
        pl.BlockSpec((tm, C_NA), row),
    ]
    if apply_ln:
        out_shape = [jax.ShapeDtypeStruct((bt, dm), F32)] + out_shape
        out_specs = [pl.BlockSpec((tm, dm), row)] + out_specs
    return pl.pallas_call(
        functools.partial(_inproj_kernel, apply_ln=apply_ln),
        grid=grid, in_specs=in_specs, out_specs=out_specs, out_shape=out_shape,
        compiler_params=_cparams(("parallel",)),
        name="inproj_ln" if apply_ln else "inproj",
    )(x2d, lng, lnb, w, alog_rows, dtb_rows)


CONV_HALO = 8


def _gdn_prep_kernel(x_ref, w_ref, o_ref, pad_ref, *, seq):
    j = pl.program_id(1)
    heads = GDN_HEADS
    zeros = jnp.zeros((CONV_HALO, LANES), F32)
    pad_ref[0:CONV_HALO, :] = zeros
    pad_ref[seq + CONV_HALO:seq + 2 * CONV_HALO, :] = zeros
    nblk = seq // CONV_RB

    def fill(i, c):
        r0 = pl.multiple_of(i * CONV_RB, CONV_RB)
        pad_ref[pl.ds(r0 + CONV_HALO, CONV_RB), :] = x_ref[0, pl.ds(r0, CONV_RB), :].astype(F32)
        return c

    lax.fori_loop(0, nblk, fill, 0)

    do_norm = j < 2 * heads
    post = jnp.where(j < heads, GDN_HEAD_DIM ** -0.5, 1.0).astype(F32)
    win_rows = CONV_RB + 2 * CONV_HALO

    def body(i, c):
        r0 = pl.multiple_of(i * CONV_RB, CONV_RB)
        win = pad_ref[pl.ds(r0, win_rows), :]
        y = jnp.zeros((CONV_RB, LANES), F32)
        for tap in range(CONV_WIDTH):
            shift = (CONV_WIDTH // 2 - tap) % win_rows
            src = win if shift == 0 else pltpu.roll(win, shift, 0)
            y = y + src[CONV_HALO:CONV_HALO + CONV_RB, :] * w_ref[tap:tap + 1, :]
        y = y * _sigmoid(y)
        ss = jnp.sum(y * y, axis=-1, keepdims=True)
        fac = jnp.where(do_norm, lax.rsqrt(ss + RMS_EPS), 1.0) * post
        o_ref[0, pl.ds(r0, CONV_RB), :] = (y * fac).astype(BF16)
        return c

    lax.fori_loop(0, nblk, body, 0)


def _gdn_prep(qkv, conv_w8):
    b, seq, c = qkv.shape
    nblk = c // LANES
    return pl.pallas_call(
        functools.partial(_gdn_prep_kernel, seq=seq),
        grid=(b, nblk),
        in_specs=[pl.BlockSpec((1, seq, LANES), lambda i, j: (i, 0, j)),
                  pl.BlockSpec((8, LANES), lambda i, j: (0, j))],
        out_specs=pl.BlockSpec((1, seq, LANES), lambda i, j: (i, 0, j)),
        out_shape=jax.ShapeDtypeStruct((b, seq, c), BF16),
        scratch_shapes=[pltpu.VMEM((seq + 2 * CONV_HALO, LANES), F32)],
        compiler_params=_cparams(("parallel", "parallel")),
        name="gdn_prep",
    )(qkv, conv_w8)


_HI = lax.Precision.HIGHEST


def _unit_tri_inverse(a):
    c = a.shape[0]
    eye = (lax.broadcasted_iota(jnp.int32, (c, c), 0) == lax.broadcasted_iota(jnp.int32, (c, c), 1)).astype(F32)
    npow = -a
    inv = eye + npow
    m = 2
    while m < c:
        npow = jnp.dot(npow, npow, preferred_element_type=F32, precision=_HI)
        inv = inv + jnp.dot(inv, npow, preferred_element_type=F32, precision=_HI)
        m *= 2
    return inv


def _gdn_scan_kernel(q_ref, k_ref, v_ref, g_ref, gr_ref, o_ref, s_ref, *, tt):
    d = pl.program_id(1)
    i = pl.program_id(2)
    nc = tt // GDN_CHUNK
    C = GDN_CHUNK
    dh = GDN_HEAD_DIM

    @pl.when(i == 0)
    def _():
        s_ref[...] = jnp.zeros_like(s_ref)

    sgn = 1 - 2 * d
    ri = lax.broadcasted_iota(jnp.int32, (C, C), 0)
    ci = lax.broadcasted_iota(jnp.int32, (C, C), 1)
    rel = (ri - ci) * sgn
    mask_incl = rel >= 0
    mask_strict = rel > 0
    fwd = d == 0

    def chunk(step, carry):
        c = step + d * (nc - 1 - 2 * step)
        r0 = pl.multiple_of(c * C, C)
        gt = g_ref[0, 0, pl.ds(r0, C), :]
        gr = gr_ref[0, 0, c]
        g_end = jnp.where(fwd, gt[C - 1:C, :], gt[0:1, :])
        for h in range(GDN_HEADS):
            cols = slice(h * dh, (h + 1) * dh)
            gc_col = gt[:, h:h + 1]
            beta = gt[:, GDN_HEADS + h:GDN_HEADS + h + 1]
            gc_row = gr[h:h + 1, :]
            glast = g_end[:, h:h + 1]
            q = q_ref[0, pl.ds(r0, C), cols]
            k = k_ref[0, pl.ds(r0, C), cols]
            v = v_ref[0, pl.ds(r0, C), cols].astype(F32)
            qf = q.astype(F32)
            kf = k.astype(F32)
            e_col = jnp.exp(gc_col)
            kb = kf * beta
            kbe = (kb * e_col).astype(BF16)
            qd = (qf * e_col).astype(BF16)
            kdec = (kf * jnp.exp(glast - gc_col)).astype(BF16)
            gam = jnp.where(mask_incl, jnp.exp(gc_col - gc_row), 0.0)
            lhs = jnp.concatenate([kb.astype(BF16), q], axis=0)
            kq = lax.dot_general(lhs, k, (((1,), (1,)), ((), ())), preferred_element_type=F32)
            a = jnp.where(mask_strict, kq[:C] * gam, 0.0)
            qk = kq[C:] * gam
            tinv = _unit_tri_inverse(a)
            s_old = s_ref[h]
            x1 = jnp.dot(jnp.concatenate([kbe, qd], axis=0), s_old.astype(BF16),
                         preferred_element_type=F32)
            rhs = (v * beta - x1[:C]).astype(BF16)
            vnew = jnp.dot(tinv.astype(BF16), rhs, preferred_element_type=F32)
            vnb = vnew.astype(BF16)
            o = x1[C:] + jnp.dot(qk.astype(BF16), vnb, preferred_element_type=F32)
            s_ref[h] = s_old * jnp.exp(glast) + lax.dot_general(
                kdec, vnb, (((0,), (0,)), ((), ())), preferred_element_type=F32)
            o_ref[0, 0, pl.ds(r0, C), cols] = o
        return carry

    lax.fori_loop(0, nc, chunk, 0)


def _gdn_scan(qkvp, gates, gcr):
    b, seq, _ = qkvp.shape
    tt = TT_SCAN
    nt = seq // tt
    nc = tt // GDN_CHUNK
    tmap = lambda d, i: i + d * (nt - 1 - 2 * i)
    return pl.pallas_call(
        functools.partial(_gdn_scan_kernel, tt=tt),
        grid=(b, 2, nt),
        in_specs=[
            pl.BlockSpec((1, tt, GDN_WIDTH), lambda bi, d, i: (bi, tmap(d, i), 0)),
            pl.BlockSpec((1, tt, GDN_WIDTH), lambda bi, d, i: (bi, tmap(d, i), 1)),
            pl.BlockSpec((1, tt, GDN_WIDTH), lambda bi, d, i: (bi, tmap(d, i), 2)),
            pl.BlockSpec((1, 1, tt, LANES), lambda bi, d, i: (d, bi, tmap(d, i), 0)),
            pl.BlockSpec((1, 1, nc, 8, GDN_CHUNK), lambda bi, d, i: (d, bi, tmap(d, i), 0, 0)),
        ],
        out_specs=pl.BlockSpec((1, 1, tt, GDN_WIDTH), lambda bi, d, i: (d, bi, tmap(d, i), 0)),
        out_shape=jax.ShapeDtypeStruct((2, b, seq, GDN_WIDTH), F32),
        scratch_shapes=[pltpu.VMEM((GDN_HEADS, GDN_HEAD_DIM, GDN_HEAD_DIM), F32)],
        compiler_params=_cparams(("parallel", "parallel", "arbitrary")),
        name="gdn_scan",
    )(qkvp, qkvp, qkvp, gates, gcr)


def _na_kernel(q_ref, k_ref, v_ref, bm_ref, g_ref, o_ref, *, rows):
    W = GRID_W
    band = NA_WIN_R * W
    hd = NA_HEAD_DIM
    lane_q = lax.broadcasted_iota(jnp.int32, (W, LANES), 1)
    first = lane_q < hd
    scale = hd ** -0.5

    def row(r, carry):
        r0 = jnp.clip(r - NA_WIN_R // 2, 0, rows - NA_WIN_R)
        variant = r - r0
        q2 = q_ref[0, pl.ds(pl.multiple_of(r * W, W), W), :]
        zero = jnp.zeros_like(q2)
        qs = jnp.concatenate([jnp.where(first, q2, zero), jnp.where(first, zero, q2)], axis=0)
        kb = k_ref[0, pl.ds(pl.multiple_of(r0 * W, W), band), :]
        vb = v_ref[0, pl.ds(pl.multiple_of(r0 * W, W), band), :]
        s = lax.dot_general(qs, kb, (((1,), (1,)), ((), ())), preferred_element_type=F32)
        s = s * scale + bm_ref[0, variant]
        m = jnp.max(s, axis=-1, keepdims=True)
        p = jnp.exp(s - m)
        l = jnp.sum(p, axis=-1, keepdims=True)
        pb = p.astype(BF16)
        o0 = jnp.dot(pb[:W], vb, preferred_element_type=F32) / l[:W]
        o1 = jnp.dot(pb[W:], vb, preferred_element_type=F32) / l[W:]
        o = jnp.where(first, o0, o1)
        sq = o * o
        ms0 = jnp.sum(jnp.where(first, sq, 0.0), axis=-1, keepdims=True)
        ms1 = jnp.sum(jnp.where(first, 0.0, sq), axis=-1, keepdims=True)
        ms = jnp.where(first, ms0, ms1) * (1.0 / hd)
        o_ref[0, pl.ds(pl.multiple_of(r * W, W), W), :] = (o * lax.rsqrt(ms + RMS_EPS) * g_ref[...]).astype(BF16)
        return carry

    lax.fori_loop(0, rows, row, 0)


def _na(h_na, bias_tab, g_row):
    b, seq, _ = h_na.shape
    rows = seq // GRID_W
    assert rows >= NA_WIN_R
    npair = NA_WIDTH // LANES
    band = NA_WIN_R * GRID_W
    return pl.pallas_call(
        functools.partial(_na_kernel, rows=rows),
        grid=(b, npair),
        in_specs=[
            pl.BlockSpec((1, seq, LANES), lambda bi, p: (bi, 0, p)),
            pl.BlockSpec((1, seq, LANES), lambda bi, p: (bi, 0, npair + p)),
            pl.BlockSpec((1, seq, LANES), lambda bi, p: (bi, 0, 2 * npair + p)),
            pl.BlockSpec((1, NA_WIN_R, 2 * GRID_W, band), lambda bi, p: (p, 0, 0, 0)),
            pl.BlockSpec((1, LANES), lambda bi, p: (0, 0)),
        ],
        out_specs=pl.BlockSpec((1, seq, LANES), lambda bi, p: (bi, 0, p)),
        out_shape=jax.ShapeDtypeStruct((b, seq, NA_WIDTH), BF16),
        compiler_params=_cparams(("parallel", "parallel")),
        name="natten",
    )(h_na, h_na, h_na, bias_tab, g_row)


def _na_bias_table(rpb_l):
    W = GRID_W
    vi = np.arange(NA_WIN_R)
    kr = np.arange(NA_WIN_R)
    dr = kr[None, :] - vi[:, None] + NA_WIN_R - 1
    qc = np.arange(W)
    kc = np.arange(W)
    win_start = np.clip(qc - NA_WIN_C // 2, 0, W - NA_WIN_C)
    in_win = (kc[None, :] >= win_start[:, None]) & (kc[None, :] < win_start[:, None] + NA_WIN_C)
    dc = np.clip(kc[None, :] - qc[:, None] + NA_WIN_C - 1, 0, 2 * NA_WIN_C - 2)
    tab = rpb_l.astype(F32)[:, dr[:, None, :, None], dc[None, :, None, :]]
    tab = jnp.where(jnp.asarray(in_win)[None, None, :, None, :], tab, -jnp.inf)
    tab = tab.reshape(NA_HEADS // 2, 2, NA_WIN_R, W, NA_WIN_R * W)
    tab = jnp.transpose(tab, (0, 2, 1, 3, 4))
    return tab.reshape(NA_HEADS // 2, NA_WIN_R, 2 * W, NA_WIN_R * W)


def _mix_ffn_kernel(x_ref, o2_ref, z_ref, ona_ref, gg_ref, wo_ref, l1g_ref, l1b_ref,
                    w1_ref, b1_ref, w2_ref, b2_ref, l2g_ref, l2b_ref, out_ref):
    dh = GDN_HEAD_DIM
    o = o2_ref[0] + o2_ref[1]
    z = z_ref[...].astype(F32)
    gate = z * _sigmoid(z)
    parts = []
    for h in range(GDN_HEADS):
        cols = slice(h * dh, (h + 1) * dh)
        oh = o[:, cols]
        ms = jnp.mean(oh * oh, axis=-1, keepdims=True)
        parts.append((oh * lax.rsqrt(ms + RMS_EPS) * gg_ref[:, cols] * gate[:, cols]).astype(BF16))
    og = jnp.concatenate(parts, axis=-1)
    mix = jnp.dot(og, wo_ref[0:GDN_WIDTH, :], preferred_element_type=F32)
    mix = mix + jnp.dot(ona_ref[...], wo_ref[GDN_WIDTH:, :], preferred_element_type=F32)
    x1 = _layer_norm(DEEPNORM_ALPHA * x_ref[...] + mix, l1g_ref[...], l1b_ref[...])
    x1b = x1.astype(BF16)
    d_ff = w1_ref.shape[1]
    acc = jnp.zeros(x1.shape, F32)
    for f in range(d_ff // FF_CHUNK):
        fs = slice(f * FF_CHUNK, (f + 1) * FF_CHUNK)
        hf = jnp.dot(x1b, w1_ref[:, fs], preferred_element_type=F32) + b1_ref[:, fs]
        hf = jnp.square(jnp.maximum(hf, 0.0)).astype(BF16)
        acc = acc + jnp.dot(hf, w2_ref[fs, :], preferred_element_type=F32)
    y = DEEPNORM_ALPHA * x1 + (acc + b2_ref[...])
    out_ref[...] = _layer_norm(y, l2g_ref[...], l2b_ref[...])


def _mix_ffn(x2d, o2, z, ona, gg, wo, l1g, l1b, w1, b1, w2, b2, l2g, l2b):
    bt, dm = x2d.shape
    d_ff = w1.shape[1]
    tm = TM_FFN
    row = lambda i: (i, 0)
    return pl.pallas_call(
        _mix_ffn_kernel,
        grid=(bt // tm,),
        in_specs=[
            pl.BlockSpec((tm, dm), row),
            pl.BlockSpec((2, tm, GDN_WIDTH), lambda i: (0, i, 0)),
            pl.BlockSpec((tm, GDN_WIDTH), row),
            pl.BlockSpec((tm, NA_WIDTH), row),
            _const_spec((1, GDN_WIDTH)),
            _const_spec((dm, dm)),
            _const_spec((1, dm)), _const_spec((1, dm)),
            _const_spec((dm, d_ff)), _const_spec((1, d_ff)),
            _const_spec((d_ff, dm)), _const_spec((1, dm)),
            _const_spec((1, dm)), _const_spec((1, dm)),
        ],
        out_specs=pl.BlockSpec((tm, dm), row),
        out_shape=jax.ShapeDtypeStruct((bt, dm), F32),
        compiler_params=_cparams(("parallel",)),
        name="mix_ffn",
    )(x2d, o2, z, ona, gg, wo, l1g, l1b, w1, b1, w2, b2, l2g, l2b)


def _pack_w_in(w_l):
    dm = w_l.shape[0]
    h = GDN_HEADS
    a0 = 4 * GDN_WIDTH
    b0 = a0 + 2 * h
    na0 = b0 + 2 * h
    pad = jnp.zeros((dm, LANES - 2 * h), w_l.dtype)
    gate_f = jnp.concatenate([w_l[:, a0:a0 + h], w_l[:, b0:b0 + h], pad], axis=1)
    gate_b = jnp.concatenate([w_l[:, a0 + h:a0 + 2 * h], w_l[:, b0 + h:b0 + 2 * h], pad], axis=1)
    w = jnp.concatenate([w_l[:, :a0], gate_f, gate_b, w_l[:, na0:]], axis=1)
    return w.astype(BF16)


def _gate_rows(p):
    return jnp.pad(p.astype(F32), ((0, 0), (0, LANES - p.shape[1])))


def kernel(x, ln_in_g, ln_in_b, w_in, conv_w, a_log, dt_bias, gdn_norm_g, rpb, na_norm_g, w_out,
           ln1_g, ln1_b, w1, b1, w2, b2, ln2_g, ln2_b):
    B, T, dm = x.shape
    bt = B * T
    nchunks = T // GDN_CHUNK
    row = lambda v: v.reshape(1, -1).astype(F32)
    xs = x.reshape(bt, dm)
    for l in range(DEPTH):
        w_l = _pack_w_in(w_in[l])
        outs = _inproj(xs, row(ln_in_g), row(ln_in_b), w_l, _gate_rows(a_log[l]), _gate_rows(dt_bias[l]),
                       apply_ln=(l == 0))
        if l == 0:
            xs, qkv, z, gates, h_na = outs
        else:
            qkv, z, gates, h_na = outs
        conv8 = jnp.pad(conv_w[l].astype(F32), ((0, 8 - CONV_WIDTH), (0, 0)))
        qkvp = _gdn_prep(qkv.reshape(B, T, C_QKV), conv8)
        gates4 = gates.reshape(2, B, T, LANES)
        gcr = gates4[..., :GDN_HEADS].reshape(2, B, nchunks, GDN_CHUNK, GDN_HEADS)
        gcr = jnp.pad(jnp.swapaxes(gcr, -1, -2), ((0, 0),) * 3 + ((0, 8 - GDN_HEADS), (0, 0)))
        o2 = _gdn_scan(qkvp, gates4, gcr)
        ona = _na(h_na.reshape(B, T, C_NA), _na_bias_table(rpb[l]),
                  jnp.tile(na_norm_g[l].astype(F32), LANES // NA_HEAD_DIM).reshape(1, LANES))
        xs = _mix_ffn(xs, o2.reshape(2, bt, GDN_WIDTH), z, ona.reshape(bt, NA_WIDTH),
                      jnp.tile(gdn_norm_g[l].astype(F32), GDN_HEADS).reshape(1, GDN_WIDTH),
                      w_out[l].astype(BF16), row(ln1_g[l]), row(ln1_b[l]),
                      w1[l].astype(BF16), row(b1[l]), w2[l].astype(BF16), row(b2[l]),
                      row(ln2_g[l]), row(ln2_b[l]))
    return xs.reshape(B, T, dm)
```

```python
import functools
import math

import jax
import jax.numpy as jnp
import numpy as np
from jax import lax
from jax.experimental import pallas as pl
from jax.experimental.pallas import tpu as pltpu

F32 = jnp.float32
BF16 = jnp.bfloat16

GRID_W = 64
GDN_HEAD_DIM = 128
GDN_HEADS = 4
GDN_WIDTH = GDN_HEADS * GDN_HEAD_DIM
NA_HEAD_DIM = 64
NA_HEADS = 8
NA_WIDTH = NA_HEADS * NA_HEAD_DIM
CONV_WIDTH = 5
GDN_CHUNK = 64
NA_WIN_R = 8
NA_WIN_C = 16
DEPTH = 2
DEEPNORM_ALPHA = (2 * DEPTH) ** 0.25
LN_EPS = 1e-5
RMS_EPS = 1e-6

LANES = 128
VMEM_LIMIT = 56 * 1024 * 1024

TM_PROJ = 512
TM_FFN = 512
TT_SCAN = 512
CONV_RB = 512
FF_CHUNK = 1024
NA_ROWS_PER_STEP = 4

C_QKV = 3 * GDN_WIDTH
C_Z = GDN_WIDTH
C_NA = 3 * NA_WIDTH
OFF_Z = C_QKV
OFF_GF = OFF_Z + C_Z
OFF_GB = OFF_GF + LANES
OFF_NA = OFF_GB + LANES
C_TOTAL = OFF_NA + C_NA


def _cparams(sem):
    return pltpu.CompilerParams(dimension_semantics=sem, vmem_limit_bytes=VMEM_LIMIT)


def _const_spec(shape):
    nd = len(shape)
    return pl.BlockSpec(shape, lambda *_: (0,) * nd, pipeline_mode=pl.Buffered(1))


def _layer_norm(y, g, b):
    mu = jnp.mean(y, axis=-1, keepdims=True)
    yc = y - mu
    var = jnp.mean(yc * yc, axis=-1, keepdims=True)
    return yc * lax.rsqrt(var + LN_EPS) * g + b


def _sigmoid(x):
    return 1.0 / (1.0 + jnp.exp(-x))


def _chunk_cumsum(g, tpos, reverse):
    n = g.shape[0]
    s = 1
    while s < GDN_CHUNK:
        if reverse:
            g = g + jnp.where(tpos < GDN_CHUNK - s, pltpu.roll(g, n - s, 0), 0.0)
        else:
            g = g + jnp.where(tpos >= s, pltpu.roll(g, s, 0), 0.0)
        s *= 2
    return g


def _inproj_kernel(x_ref, lng_ref, lnb_ref, w_ref, alog_ref, dtb_ref, *out_refs, apply_ln):
    if apply_ln:
        xn_ref, qkv_ref, z_ref, gates_ref, na_ref = out_refs
    else:
        qkv_ref, z_ref, gates_ref, na_ref = out_refs
    x = x_ref[...]
    if apply_ln:
        x = _layer_norm(x, lng_ref[...], lnb_ref[...])
        xn_ref[...] = x
    xb = x.astype(BF16)
    qkv_ref[...] = jnp.dot(xb, w_ref[:, 0:C_QKV], preferred_element_type=F32).astype(BF16)
    z_ref[...] = jnp.dot(xb, w_ref[:, OFF_Z:OFF_Z + C_Z], preferred_element_type=F32).astype(BF16)
    na_ref[...] = jnp.dot(xb, w_ref[:, OFF_NA:OFF_NA + C_NA], preferred_element_type=F32).astype(BF16)

    tm = x.shape[0]
    tpos = lax.broadcasted_iota(jnp.int32, (tm, LANES), 0) & (GDN_CHUNK - 1)
    lane = lax.broadcasted_iota(jnp.int32, (tm, LANES), 1)
    for d in range(2):
        off = OFF_GF if d == 0 else OFF_GB
        hab = jnp.dot(xb, w_ref[:, off:off + LANES], preferred_element_type=F32)
        sp_in = hab + dtb_ref[d:d + 1, :]
        softplus = jnp.maximum(sp_in, 0.0) + jnp.log1p(jnp.exp(-jnp.abs(sp_in)))
        g = -jnp.exp(alog_ref[d:d + 1, :]) * softplus
        g = jnp.where(lane < GDN_HEADS, g, 0.0)
        gc = _chunk_cumsum(g, tpos, reverse=(d == 1))
        gates_ref[d] = jnp.where(lane < GDN_HEADS, gc, _sigmoid(hab))


def _inproj(x2d, lng, lnb, w, alog_rows, dtb_rows, apply_ln):
    bt, dm = x2d.shape
    tm = TM_PROJ
    grid = (bt // tm,)
    row = lambda i: (i, 0)
    in_specs = [
        pl.BlockSpec((tm, dm), row),
        _const_spec((1, dm)), _const_spec((1, dm)),
        _const_spec((dm, C_TOTAL)),
        _const_spec((2, LANES)), _const_spec((2, LANES)),
    ]
    out_shape = [
        jax.ShapeDtypeStruct((bt, C_QKV), BF16),
        jax.ShapeDtypeStruct((bt, C_Z), BF16),
        jax.ShapeDtypeStruct((2, bt, LANES), F32),
        jax.ShapeDtypeStruct((bt, C_NA), BF16),
    ]
    out_specs = [
        pl.BlockSpec((tm, C_QKV), row),
        pl.BlockSpec((tm, C_Z), row),
        pl.BlockSpec((2, tm, LANES), lambda i: (0, i, 0)),
        pl.BlockSpec((tm, C_NA), row),
    ]
    if apply_ln:
        out_shape = [jax.ShapeDtypeStruct((bt, dm), F32)] + out_shape
        out_specs = [pl.BlockSpec((tm, dm), row)] + out_specs
    return pl.pallas_call(
        functools.partial(_inproj_kernel, apply_ln=apply_ln),
        grid=grid, in_specs=in_specs, out_specs=out_specs, out_shape=out_shape,
        compiler_params=_cparams(("parallel",)),
        name="inproj_ln" if apply_ln else "inproj",
    )(x2d, lng, lnb, w, alog_rows, dtb_rows)


CONV_HALO = 8


def _gdn_prep_kernel(x_ref, w_ref, o_ref, pad_ref, *, seq):
    j = pl.program_id(1)
    heads = GDN_HEADS
    zeros = jnp.zeros((CONV_HALO, LANES), F32)
    pad_ref[0:CONV_HALO, :] = zeros
    pad_ref[seq + CONV_HALO:seq + 2 * CONV_HALO, :] = zeros
    nblk = seq // CONV_RB

    def fill(i, c):
        r0 = pl.multiple_of(i * CONV_RB, CONV_RB)
        pad_ref[pl.ds(r0 + CONV_HALO, CONV_RB), :] = x_ref[0, pl.ds(r0, CONV_RB), :].astype(F32)
        return c

    lax.fori_loop(0, nblk, fill, 0)

    do_norm = j < 2 * heads
    post = jnp.where(j < heads, GDN_HEAD_DIM ** -0.5, 1.0).astype(F32)
    win_rows = CONV_RB + 2 * CONV_HALO

    def body(i, c):
        r0 = pl.multiple_of(i * CONV_RB, CONV_RB)
        win = pad_ref[pl.ds(r0, win_rows), :]
        y = jnp.zeros((CONV_RB, LANES), F32)
        for tap in range(CONV_WIDTH):
            shift = (CONV_WIDTH // 2 - tap) % win_rows
            src = win if shift == 0 else pltpu.roll(win, shift, 0)
            y = y + src[CONV_HALO:CONV_HALO + CONV_RB, :] * w_ref[tap:tap + 1, :]
        y = y * _sigmoid(y)
        ss = jnp.sum(y * y, axis=-1, keepdims=True)
        fac = jnp.where(do_norm, lax.rsqrt(ss + RMS_EPS), 1.0) * post
        o_ref[0, pl.ds(r0, CONV_RB), :] = (y * fac).astype(BF16)
        return c

    lax.fori_loop(0, nblk, body, 0)


def _gdn_prep(qkv, conv_w8):
    b, seq, c = qkv.shape
    nblk = c // LANES
    return pl.pallas_call(
        functools.partial(_gdn_prep_kernel, seq=seq),
        grid=(b, nblk),
        in_specs=[pl.BlockSpec((1, seq, LANES), lambda i, j: (i, 0, j)),
                  pl.BlockSpec((8, LANES), lambda i, j: (0, j))],
        out_specs=pl.BlockSpec((1, seq, LANES), lambda i, j: (i, 0, j)),
        out_shape=jax.ShapeDtypeStruct((b, seq, c), BF16),
        scratch_shapes=[pltpu.VMEM((seq + 2 * CONV_HALO, LANES), F32)],
        compiler_params=_cparams(("parallel", "parallel")),
        name="gdn_prep",
    )(qkv, conv_w8)


def _unit_tri_inverses(mats):
    c = mats[0].shape[0]
    eye = (lax.broadcasted_iota(jnp.int32, (c, c), 0) == lax.broadcasted_iota(jnp.int32, (c, c), 1)).astype(F32)
    ps = [-a for a in mats]
    xs = [eye + p for p in ps]
    pbs = [p.astype(BF16) for p in ps]
    ps = [jnp.dot(pb, pb, preferred_element_type=F32) for pb in pbs]
    m = 2
    while 2 * m < c:
        pbs = [p.astype(BF16) for p in ps]
        prods = [jnp.dot(jnp.concatenate([x.astype(BF16), pb], axis=0), pb, preferred_element_type=F32)
                 for x, pb in zip(xs, pbs)]
        xs = [x + prod[:c] for x, prod in zip(xs, prods)]
        ps = [prod[c:] for prod in prods]
        m *= 2
    return [x + jnp.dot(x.astype(BF16), p.astype(BF16), preferred_element_type=F32) for x, p in zip(xs, ps)]


def _gdn_scan_kernel(qf_ref, kf_ref, vf_ref, gf_ref, grf_ref, qb_ref, kb_ref, vb_ref, gb_ref, grb_ref,
                     of_ref, ob_ref, s_ref, stk_ref, wq_ref, vbeta_ref, *, tt):
    i = pl.program_id(1)
    nc = tt // GDN_CHUNK
    C = GDN_CHUNK
    dh = GDN_HEAD_DIM
    H = GDN_HEADS

    @pl.when(i == 0)
    def _():
        s_ref[...] = jnp.zeros_like(s_ref)

    ri = lax.broadcasted_iota(jnp.int32, (C, C), 0)
    ci = lax.broadcasted_iota(jnp.int32, (C, C), 1)
    mask_incl = (ri >= ci, ri <= ci)
    mask_strict = (ri > ci, ri < ci)
    dirs = ((qf_ref, kf_ref, vf_ref, gf_ref, grf_ref, of_ref), (qb_ref, kb_ref, vb_ref, gb_ref, grb_ref, ob_ref))

    def chunk_end_decay(gt, d):
        return gt[C - 1:C, :] if d == 0 else gt[0:1, :]

    def prep_chunk(c, carry):
        r0 = pl.multiple_of(c * C, C)
        lhs_l, k_l, gam_l, kdec_l, dir_l = [], [], [], [], []
        for d in range(2):
            q_ref, k_ref, v_ref, g_ref, gr_ref, _ = dirs[d]
            gt = g_ref[0, 0, pl.ds(r0, C), :]
            gr = gr_ref[0, 0, c]
            g_end = chunk_end_decay(gt, d)
            for h in range(H):
                combo = d * H + h
                cols = slice(h * dh, (h + 1) * dh)
                gc_col = gt[:, h:h + 1]
                beta = gt[:, H + h:H + h + 1]
                gc_row = gr[h:h + 1, :]
                glast = g_end[:, h:h + 1]
                q = q_ref[0, pl.ds(r0, C), cols]
                k = k_ref[0, pl.ds(r0, C), cols]
                v = v_ref[0, pl.ds(r0, C), cols].astype(F32)
                kf = k.astype(F32)
                e_col = jnp.exp(gc_col)
                kb = kf * beta
                kbe = (kb * e_col).astype(BF16)
                qd = (q.astype(F32) * e_col).astype(BF16)
                stk_ref[c, combo] = jnp.concatenate([kbe, qd], axis=0)
                vbeta_ref[c, combo] = v * beta
                lhs_l.append(jnp.concatenate([kb.astype(BF16), q], axis=0))
                k_l.append(k)
                kdec_l.append((kf * jnp.exp(glast - gc_col)).astype(BF16))
                gam_l.append(jnp.where(mask_incl[d], jnp.exp(gc_col - gc_row), 0.0))
                dir_l.append(d)
        nt_dims = (((1,), (1,)), ((), ()))
        tn_dims = (((0,), (0,)), ((), ()))
        kq_l = [lax.dot_general(lhs, k, nt_dims, preferred_element_type=F32) for lhs, k in zip(lhs_l, k_l)]
        a_l = [jnp.where(mask_strict[d], kq[:C] * gam, 0.0) for kq, gam, d in zip(kq_l, gam_l, dir_l)]
        qk_l = [(kq[C:] * gam).astype(BF16) for kq, gam in zip(kq_l, gam_l)]
        tinv_l = [t.astype(BF16) for t in _unit_tri_inverses(a_l)]
        wc_l = [lax.dot_general(kdec, t, tn_dims, preferred_element_type=F32) for kdec, t in zip(kdec_l, tinv_l)]
        qt_l = [jnp.dot(qk, t, preferred_element_type=F32) for qk, t in zip(qk_l, tinv_l)]
        for combo, (wc, qt) in enumerate(zip(wc_l, qt_l)):
            wq_ref[c, combo] = jnp.concatenate([wc, qt], axis=0).astype(BF16)
        return carry

    lax.fori_loop(0, nc, prep_chunk, 0)

    def scan_step(step, carry):
        cs = (step, nc - 1 - step)
        egs = []
        for d in range(2):
            r0 = pl.multiple_of(cs[d] * C, C)
            egs.append(jnp.exp(chunk_end_decay(dirs[d][3][0, 0, pl.ds(r0, C), :], d)))
        idx = [(d, h, d * H + h) for d in range(2) for h in range(H)]
        s_l = [s_ref[combo] for _, _, combo in idx]
        x1_l = [jnp.dot(stk_ref[cs[d], combo], s.astype(BF16), preferred_element_type=F32)
                for (d, _, combo), s in zip(idx, s_l)]
        res_l = [(vbeta_ref[cs[d], combo] - x1[:C]).astype(BF16) for (d, _, combo), x1 in zip(idx, x1_l)]
        z_l = [jnp.dot(wq_ref[cs[d], combo], res, preferred_element_type=F32)
               for (d, _, combo), res in zip(idx, res_l)]
        for (d, h, combo), s, x1, z in zip(idx, s_l, x1_l, z_l):
            s_ref[combo] = s * egs[d][:, h:h + 1] + z[:dh]
            r0 = pl.multiple_of(cs[d] * C, C)
            dirs[d][5][0, pl.ds(r0, C), h * dh:(h + 1) * dh] = x1[C:] + z[dh:]
        return carry

    lax.fori_loop(0, nc, scan_step, 0)


def _gdn_scan(qkvp, gates, gcr):
    b, seq, _ = qkvp.shape
    tt = TT_SCAN
    nt = seq // tt
    nc = tt // GDN_CHUNK
    C = GDN_CHUNK
    dh = GDN_HEAD_DIM
    ncombo = 2 * GDN_HEADS

    def dir_specs(d):
        tmap = (lambda i: i) if d == 0 else (lambda i: nt - 1 - i)
        return [
            pl.BlockSpec((1, tt, GDN_WIDTH), lambda bi, i: (bi, tmap(i), 0)),
            pl.BlockSpec((1, tt, GDN_WIDTH), lambda bi, i: (bi, tmap(i), 1)),
            pl.BlockSpec((1, tt, GDN_WIDTH), lambda bi, i: (bi, tmap(i), 2)),
            pl.BlockSpec((1, 1, tt, LANES), lambda bi, i: (d, bi, tmap(i), 0)),
            pl.BlockSpec((1, 1, nc, 8, C), lambda bi, i: (d, bi, tmap(i), 0, 0)),
        ]

    out_sds = jax.ShapeDtypeStruct((b, seq, GDN_WIDTH), F32)
    return pl.pallas_call(
        functools.partial(_gdn_scan_kernel, tt=tt),
        grid=(b, nt),
        in_specs=dir_specs(0) + dir_specs(1),
        out_specs=[pl.BlockSpec((1, tt, GDN_WIDTH), lambda bi, i: (bi, i, 0)),
                   pl.BlockSpec((1, tt, GDN_WIDTH), lambda bi, i: (bi, nt - 1 - i, 0))],
        out_shape=[out_sds, out_sds],
        scratch_shapes=[
            pltpu.VMEM((ncombo, dh, dh), F32),
            pltpu.VMEM((nc, ncombo, 2 * C, dh), BF16),
            pltpu.VMEM((nc, ncombo, dh + C, C), BF16),
            pltpu.VMEM((nc, ncombo, C, dh), F32),
        ],
        compiler_params=_cparams(("parallel", "arbitrary")),
        name="gdn_scan",
    )(qkvp, qkvp, qkvp, gates, gcr, qkvp, qkvp, qkvp, gates, gcr)


def _na_kernel(q_ref, k_ref, v_ref, bm_ref, g_ref, o_ref, *, rows):
    W = GRID_W
    band = NA_WIN_R * W
    hd = NA_HEAD_DIM
    lane_q = lax.broadcasted_iota(jnp.int32, (W, LANES), 1)
    first = lane_q < hd
    scale = hd ** -0.5

    nt_dims = (((1,), (1,)), ((), ()))

    def row_group(gi, carry):
        rs = [gi * NA_ROWS_PER_STEP + j for j in range(NA_ROWS_PER_STEP)]
        r0s = [jnp.clip(r - NA_WIN_R // 2, 0, rows - NA_WIN_R) for r in rs]
        qs_l, kb_l, vb_l = [], [], []
        for r, r0 in zip(rs, r0s):
            q2 = q_ref[0, pl.ds(pl.multiple_of(r * W, W), W), :]
            zero = jnp.zeros_like(q2)
            qs_l.append(jnp.concatenate([jnp.where(first, q2, zero), jnp.where(first, zero, q2)], axis=0))
            kb_l.append(k_ref[0, pl.ds(pl.multiple_of(r0 * W, W), band), :])
            vb_l.append(v_ref[0, pl.ds(pl.multiple_of(r0 * W, W), band), :])
        s_l = [lax.dot_general(qs, kb, nt_dims, preferred_element_type=F32) for qs, kb in zip(qs_l, kb_l)]
        s_l = [s * scale + bm_ref[0, r - r0] for s, r, r0 in zip(s_l, rs, r0s)]
        m_l = [jnp.max(s, axis=-1, keepdims=True) for s in s_l]
        p_l = [jnp.exp(s - m) for s, m in zip(s_l, m_l)]
        l_l = [jnp.sum(p, axis=-1, keepdims=True) for p in p_l]
        pb_l = [p.astype(BF16) for p in p_l]
        o0_l = [jnp.dot(pb[:W], vb, preferred_element_type=F32) for pb, vb in zip(pb_l, vb_l)]
        o1_l = [jnp.dot(pb[W:], vb, preferred_element_type=F32) for pb, vb in zip(pb_l, vb_l)]
        for r, o0, o1, l in zip(rs, o0_l, o1_l, l_l):
            o = jnp.where(first, o0 / l[:W], o1 / l[W:])
            sq = o * o
            ms0 = jnp.sum(jnp.where(first, sq, 0.0), axis=-1, keepdims=True)
            ms1 = jnp.sum(jnp.where(first, 0.0, sq), axis=-1, keepdims=True)
            ms = jnp.where(first, ms0, ms1) * (1.0 / hd)
            o_ref[0, pl.ds(pl.multiple_of(r * W, W), W), :] = (o * lax.rsqrt(ms + RMS_EPS) * g_ref[...]).astype(BF16)
        return carry

    lax.fori_loop(0, rows // NA_ROWS_PER_STEP, row_group, 0)


def _na(h_na, bias_tab, g_row):
    b, seq, _ = h_na.shape
    rows = seq // GRID_W
    assert rows >= NA_WIN_R
    npair = NA_WIDTH // LANES
    band = NA_WIN_R * GRID_W
    return pl.pallas_call(
        functools.partial(_na_kernel, rows=rows),
        grid=(b, npair),
        in_specs=[
            pl.BlockSpec((1, seq, LANES), lambda bi, p: (bi, 0, p)),
            pl.BlockSpec((1, seq, LANES), lambda bi, p: (bi, 0, npair + p)),
            pl.BlockSpec((1, seq, LANES), lambda bi, p: (bi, 0, 2 * npair + p)),
            pl.BlockSpec((1, NA_WIN_R, 2 * GRID_W, band), lambda bi, p: (p, 0, 0, 0)),
            pl.BlockSpec((1, LANES), lambda bi, p: (0, 0)),
        ],
        out_specs=pl.BlockSpec((1, seq, LANES), lambda bi, p: (bi, 0, p)),
        out_shape=jax.ShapeDtypeStruct((b, seq, NA_WIDTH), BF16),
        compiler_params=_cparams(("parallel", "parallel")),
        name="natten",
    )(h_na, h_na, h_na, bias_tab, g_row)


def _na_bias_table(rpb_l):
    W = GRID_W
    vi = np.arange(NA_WIN_R)
    kr = np.arange(NA_WIN_R)
    dr = kr[None, :] - vi[:, None] + NA_WIN_R - 1
    qc = np.arange(W)
    kc = np.arange(W)
    win_start = np.clip(qc - NA_WIN_C // 2, 0, W - NA_WIN_C)
    in_win = (kc[None, :] >= win_start[:, None]) & (kc[None, :] < win_start[:, None] + NA_WIN_C)
    dc = kc[None, :] - qc[:, None] + NA_WIN_C - 1
    rsel = (dr[:, :, None] == np.arange(2 * NA_WIN_R - 1)).astype(np.float32)
    csel = ((dc[:, :, None] == np.arange(2 * NA_WIN_C - 1)) & in_win[:, :, None]).astype(np.float32)
    tab = jnp.einsum("hab,vka,qcb->hvqkc", rpb_l.astype(F32), rsel, csel,
                     precision=lax.Precision.HIGHEST)
    tab = jnp.where(jnp.asarray(in_win)[None, None, :, None, :], tab, -jnp.inf)
    tab = tab.reshape(NA_HEADS // 2, 2, NA_WIN_R, W, NA_WIN_R * W)
    tab = jnp.transpose(tab, (0, 2, 1, 3, 4))
    return tab.reshape(NA_HEADS // 2, NA_WIN_R, 2 * W, NA_WIN_R * W)


def _mix_ffn_kernel(x_ref, of_ref, ob_ref, z_ref, ona_ref, gg_ref, wo_ref, l1g_ref, l1b_ref,
                    w1_ref, b1_ref, w2_ref, b2_ref, l2g_ref, l2b_ref, out_ref):
    dh = GDN_HEAD_DIM
    o = of_ref[...] + ob_ref[...]
    z = z_ref[...].astype(F32)
    gate = z * _sigmoid(z)
    parts = []
    for h in range(GDN_HEADS):
        cols = slice(h * dh, (h + 1) * dh)
        oh = o[:, cols]
        ms = jnp.mean(oh * oh, axis=-1, keepdims=True)
        parts.append((oh * lax.rsqrt(ms + RMS_EPS) * gg_ref[:, cols] * gate[:, cols]).astype(BF16))
    og = jnp.concatenate(parts, axis=-1)
    mix = jnp.dot(og, wo_ref[0:GDN_WIDTH, :], preferred_element_type=F32)
    mix = mix + jnp.dot(ona_ref[...], wo_ref[GDN_WIDTH:, :], preferred_element_type=F32)
    x1 = _layer_norm(DEEPNORM_ALPHA * x_ref[...] + mix, l1g_ref[...], l1b_ref[...])
    x1b = x1.astype(BF16)
    d_ff = w1_ref.shape[1]
    acc = jnp.zeros(x1.shape, F32)
    for f in range(d_ff // FF_CHUNK):
        fs = slice(f * FF_CHUNK, (f + 1) * FF_CHUNK)
        hf = jnp.dot(x1b, w1_ref[:, fs], preferred_element_type=F32) + b1_ref[:, fs]
        hf = jnp.square(jnp.maximum(hf, 0.0)).astype(BF16)
        acc = acc + jnp.dot(hf, w2_ref[fs, :], preferred_element_type=F32)
    y = DEEPNORM_ALPHA * x1 + (acc + b2_ref[...])
    out_ref[...] = _layer_norm(y, l2g_ref[...], l2b_ref[...])


def _mix_ffn(x2d, o_f, o_b, z, ona, gg, wo, l1g, l1b, w1, b1, w2, b2, l2g, l2b):
    bt, dm = x2d.shape
    d_ff = w1.shape[1]
    tm = TM_FFN
    row = lambda i: (i, 0)
    return pl.pallas_call(
        _mix_ffn_kernel,
        grid=(bt // tm,),
        in_specs=[
            pl.BlockSpec((tm, dm), row),
            pl.BlockSpec((tm, GDN_WIDTH), row),
            pl.BlockSpec((tm, GDN_WIDTH), row),
            pl.BlockSpec((tm, GDN_WIDTH), row),
            pl.BlockSpec((tm, NA_WIDTH), row),
            _const_spec((1, GDN_WIDTH)),
            _const_spec((dm, dm)),
            _const_spec((1, dm)), _const_spec((1, dm)),
            _const_spec((dm, d_ff)), _const_spec((1, d_ff)),
            _const_spec((d_ff, dm)), _const_spec((1, dm)),
            _const_spec((1, dm)), _const_spec((1, dm)),
        ],
        out_specs=pl.BlockSpec((tm, dm), row),
        out_shape=jax.ShapeDtypeStruct((bt, dm), F32),
        compiler_params=_cparams(("parallel",)),
        name="mix_ffn",
    )(x2d, o_f, o_b, z, ona, gg, wo, l1g, l1b, w1, b1, w2, b2, l2g, l2b)


def _pack_w_in(w_l):
    dm = w_l.shape[0]
    h = GDN_HEADS
    a0 = 4 * GDN_WIDTH
    b0 = a0 + 2 * h
    na0 = b0 + 2 * h
    pad = jnp.zeros((dm, LANES - 2 * h), w_l.dtype)
    gate_f = jnp.concatenate([w_l[:, a0:a0 + h], w_l[:, b0:b0 + h], pad], axis=1)
    gate_b = jnp.concatenate([w_l[:, a0 + h:a0 + 2 * h], w_l[:, b0 + h:b0 + 2 * h], pad], axis=1)
    w = jnp.concatenate([w_l[:, :a0], gate_f, gate_b, w_l[:, na0:]], axis=1)
    return w.astype(BF16)


def _gate_rows(p):
    return jnp.pad(p.astype(F32), ((0, 0), (0, LANES - p.shape[1])))


def kernel(x, ln_in_g, ln_in_b, w_in, conv_w, a_log, dt_bias, gdn_norm_g, rpb, na_norm_g, w_out,
           ln1_g, ln1_b, w1, b1, w2, b2, ln2_g, ln2_b):
    B, T, dm = x.shape
    bt = B * T
    nchunks = T // GDN_CHUNK
    row = lambda v: v.reshape(1, -1).astype(F32)
    xs = x.reshape(bt, dm)
    for l in range(DEPTH):
        w_l = _pack_w_in(w_in[l])
        outs = _inproj(xs, row(ln_in_g), row(ln_in_b), w_l, _gate_rows(a_log[l]), _gate_rows(dt_bias[l]),
                       apply_ln=(l == 0))
        if l == 0:
            xs, qkv, z, gates, h_na = outs
        else:
            qkv, z, gates, h_na = outs
        conv8 = jnp.pad(conv_w[l].astype(F32), ((0, 8 - CONV_WIDTH), (0, 0)))
        qkvp = _gdn_prep(qkv.reshape(B, T, C_QKV), conv8)
        gates4 = gates.reshape(2, B, T, LANES)
        gcr = gates4[..., :GDN_HEADS].reshape(2, B, nchunks, GDN_CHUNK, GDN_HEADS)
        gcr = jnp.pad(jnp.swapaxes(gcr, -1, -2), ((0, 0),) * 3 + ((0, 8 - GDN_HEADS), (0, 0)))
        o_f, o_b = _gdn_scan(qkvp, gates4, gcr)
        ona = _na(h_na.reshape(B, T, C_NA), _na_bias_table(rpb[l]),
                  jnp.tile(na_norm_g[l].astype(F32), LANES // NA_HEAD_DIM).reshape(1, LANES))
        xs = _mix_ffn(xs, o_f.reshape(bt, GDN_WIDTH), o_b.reshape(bt, GDN_WIDTH), z, ona.reshape(bt, NA_WIDTH),
                      jnp.tile(gdn_norm_g[l].astype(F32), GDN_HEADS).reshape(1, GDN_WIDTH),
                      w_out[l].astype(BF16), row(ln1_g[l]), row(ln1_b[l]),
                      w1[l].astype(BF16), row(b1[l]), w2[l].astype(BF16), row(b2[l]),
                      row(ln2_g[l]), row(ln2_b[l]))
    return xs.reshape(B, T, dm)
```

```python
import functools
import math

import jax
import jax.numpy as jnp
import numpy as np
from jax import lax
from jax.experimental import pallas as pl
from jax.experimental.pallas import tpu as pltpu

F32 = jnp.float32
BF16 = jnp.bfloat16

GRID_W = 64
GDN_HEAD_DIM = 128
GDN_HEADS = 4
GDN_WIDTH = GDN_HEADS * GDN_HEAD_DIM
NA_HEAD_DIM = 64
NA_HEADS = 8
NA_WIDTH = NA_HEADS * NA_HEAD_DIM
CONV_WIDTH = 5
GDN_CHUNK = 64
NA_WIN_R = 8
NA_WIN_C = 16
DEPTH = 2
DEEPNORM_ALPHA = (2 * DEPTH) ** 0.25
LN_EPS = 1e-5
RMS_EPS = 1e-6

LANES = 128
VMEM_LIMIT = 56 * 1024 * 1024

TM_PROJ = 512
TM_FFN = 512
TT_SCAN = 512
CONV_RB = 512
FF_CHUNK = 1024
NA_ROWS_PER_STEP = 4
GDN_BATCH_PER_STEP = 2
GDN_PREP_CHUNKS = 4

C_QKV = 3 * GDN_WIDTH
C_Z = GDN_WIDTH
C_NA = 3 * NA_WIDTH
OFF_Z = C_QKV
OFF_GF = OFF_Z + C_Z
OFF_GB = OFF_GF + LANES
OFF_NA = OFF_GB + LANES
C_TOTAL = OFF_NA + C_NA


def _cparams(sem):
    return pltpu.CompilerParams(dimension_semantics=sem, vmem_limit_bytes=VMEM_LIMIT)


def _const_spec(shape):
    nd = len(shape)
    return pl.BlockSpec(shape, lambda *_: (0,) * nd, pipeline_mode=pl.Buffered(1))


def _layer_norm(y, g, b):
    mu = jnp.mean(y, axis=-1, keepdims=True)
    yc = y - mu
    var = jnp.mean(yc * yc, axis=-1, keepdims=True)
    return yc * lax.rsqrt(var + LN_EPS) * g + b


def _sigmoid(x):
    return 1.0 / (1.0 + jnp.exp(-x))


def _chunk_cumsum(g, tpos, reverse):
    n = g.shape[0]
    s = 1
    while s < GDN_CHUNK:
        if reverse:
            g = g + jnp.where(tpos < GDN_CHUNK - s, pltpu.roll(g, n - s, 0), 0.0)
        else:
            g = g + jnp.where(tpos >= s, pltpu.roll(g, s, 0), 0.0)
        s *= 2
    return g


def _inproj_kernel(x_ref, lng_ref, lnb_ref, w_ref, alog_ref, dtb_ref, *out_refs, apply_ln):
    if apply_ln:
        xn_ref, qkv_ref, z_ref, gates_ref, na_ref = out_refs
    else:
        qkv_ref, z_ref, gates_ref, na_ref = out_refs
    x = x_ref[...]
    if apply_ln:
        x = _layer_norm(x, lng_ref[...], lnb_ref[...])
        xn_ref[...] = x
    xb = x.astype(BF16)
    qkv_ref[...] = jnp.dot(xb, w_ref[:, 0:C_QKV], preferred_element_type=F32).astype(BF16)
    z_ref[...] = jnp.dot(xb, w_ref[:, OFF_Z:OFF_Z + C_Z], preferred_element_type=F32).astype(BF16)
    na_ref[...] = jnp.dot(xb, w_ref[:, OFF_NA:OFF_NA + C_NA], preferred_element_type=F32).astype(BF16)

    tm = x.shape[0]
    tpos = lax.broadcasted_iota(jnp.int32, (tm, LANES), 0) & (GDN_CHUNK - 1)
    lane = lax.broadcasted_iota(jnp.int32, (tm, LANES), 1)
    for d in range(2):
        off = OFF_GF if d == 0 else OFF_GB
        hab = jnp.dot(xb, w_ref[:, off:off + LANES], preferred_element_type=F32)
        sp_in = hab + dtb_ref[d:d + 1, :]
        softplus = jnp.maximum(sp_in, 0.0) + jnp.log1p(jnp.exp(-jnp.abs(sp_in)))
        g = -jnp.exp(alog_ref[d:d + 1, :]) * softplus
        g = jnp.where(lane < GDN_HEADS, g, 0.0)
        gc = _chunk_cumsum(g, tpos, reverse=(d == 1))
        gates_ref[d] = jnp.where(lane < GDN_HEADS, gc, _sigmoid(hab))


def _inproj(x2d, lng, lnb, w, alog_rows, dtb_rows, apply_ln):
    bt, dm = x2d.shape
    tm = TM_PROJ
    grid = (bt // tm,)
    row = lambda i: (i, 0)
    in_specs = [
        pl.BlockSpec((tm, dm), row),
        _const_spec((1, dm)), _const_spec((1, dm)),
        _const_spec((dm, C_TOTAL)),
        _const_spec((2, LANES)), _const_spec((2, LANES)),
    ]
    out_shape = [
        jax.ShapeDtypeStruct((bt, C_QKV), BF16),
        jax.ShapeDtypeStruct((bt, C_Z), BF16),
        jax.ShapeDtypeStruct((2, bt, LANES), F32),
        jax.ShapeDtypeStruct((bt, C_NA), BF16),
    ]
    out_specs = [
        pl.BlockSpec((tm, C_QKV), row),
        pl.BlockSpec((tm, C_Z), row),
        pl.BlockSpec((2, tm, LANES), lambda i: (0, i, 0)),
        pl.BlockSpec((tm, C_NA), row),
    ]
    if apply_ln:
        out_shape = [jax.ShapeDtypeStruct((bt, dm), F32)] + out_shape
        out_specs = [pl.BlockSpec((tm, dm), row)] + out_specs
    return pl.pallas_call(
        functools.partial(_inproj_kernel, apply_ln=apply_ln),
        grid=grid, in_specs=in_specs, out_specs=out_specs, out_shape=out_shape,
        compiler_params=_cparams(("parallel",)),
        name="inproj_ln" if apply_ln else "inproj",
    )(x2d, lng, lnb, w, alog_rows, dtb_rows)


CONV_HALO = 8


def _gdn_prep_kernel(x_ref, w_ref, o_ref, pad_ref, *, seq):
    j = pl.program_id(1)
    heads = GDN_HEADS
    zeros = jnp.zeros((CONV_HALO, LANES), F32)
    pad_ref[0:CONV_HALO, :] = zeros
    pad_ref[seq + CONV_HALO:seq + 2 * CONV_HALO, :] = zeros
    nblk = seq // CONV_RB

    def fill(i, c):
        r0 = pl.multiple_of(i * CONV_RB, CONV_RB)
        pad_ref[pl.ds(r0 + CONV_HALO, CONV_RB), :] = x_ref[0, pl.ds(r0, CONV_RB), :].astype(F32)
        return c

    lax.fori_loop(0, nblk, fill, 0)

    do_norm = j < 2 * heads
    post = jnp.where(j < heads, GDN_HEAD_DIM ** -0.5, 1.0).astype(F32)
    win_rows = CONV_RB + 2 * CONV_HALO

    def body(i, c):
        r0 = pl.multiple_of(i * CONV_RB, CONV_RB)
        win = pad_ref[pl.ds(r0, win_rows), :]
        y = jnp.zeros((CONV_RB, LANES), F32)
        for tap in range(CONV_WIDTH):
            shift = (CONV_WIDTH // 2 - tap) % win_rows
            src = win if shift == 0 else pltpu.roll(win, shift, 0)
            y = y + src[CONV_HALO:CONV_HALO + CONV_RB, :] * w_ref[tap:tap + 1, :]
        y = y * _sigmoid(y)
        ss = jnp.sum(y * y, axis=-1, keepdims=True)
        fac = jnp.where(do_norm, lax.rsqrt(ss + RMS_EPS), 1.0) * post
        o_ref[0, pl.ds(r0, CONV_RB), :] = (y * fac).astype(BF16)
        return c

    lax.fori_loop(0, nblk, body, 0)


def _gdn_prep(qkv, conv_w8):
    b, seq, c = qkv.shape
    nblk = c // LANES
    return pl.pallas_call(
        functools.partial(_gdn_prep_kernel, seq=seq),
        grid=(b, nblk),
        in_specs=[pl.BlockSpec((1, seq, LANES), lambda i, j: (i, 0, j)),
                  pl.BlockSpec((8, LANES), lambda i, j: (0, j))],
        out_specs=pl.BlockSpec((1, seq, LANES), lambda i, j: (i, 0, j)),
        out_shape=jax.ShapeDtypeStruct((b, seq, c), BF16),
        scratch_shapes=[pltpu.VMEM((seq + 2 * CONV_HALO, LANES), F32)],
        compiler_params=_cparams(("parallel", "parallel")),
        name="gdn_prep",
    )(qkv, conv_w8)


def _block_diag(x, nblk):
    w = x.shape[1] // nblk
    blk = lax.broadcasted_iota(jnp.int32, x.shape, 1) // w
    zero = jnp.zeros_like(x)
    return jnp.concatenate([jnp.where(blk == h, x, zero) for h in range(nblk)], axis=0)


def _lane_blocks(cols, width):
    r, n = cols.shape
    if width % LANES == 0:
        return jnp.concatenate([jnp.broadcast_to(cols[:, j:j + 1], (r, width)) for j in range(n)], axis=1)
    blk = lax.broadcasted_iota(jnp.int32, (r, n * width), 1) // width
    out = jnp.broadcast_to(cols[:, n - 1:n], (r, n * width))
    for j in range(n - 2, -1, -1):
        out = jnp.where(blk == j, jnp.broadcast_to(cols[:, j:j + 1], (r, n * width)), out)
    return out


def _unit_tri_inverses(mats, nblk):
    c = mats[0].shape[0]
    ri = lax.broadcasted_iota(jnp.int32, (c, nblk * c), 0)
    ci = lax.broadcasted_iota(jnp.int32, (c, nblk * c), 1) % c
    eye = (ri == ci).astype(F32)
    ps = [-a for a in mats]
    xs = [eye + p for p in ps]
    pbs = [p.astype(BF16) for p in ps]
    ps = [jnp.dot(pb, _block_diag(pb, nblk), preferred_element_type=F32) for pb in pbs]
    m = 2
    while 2 * m < c:
        pbs = [p.astype(BF16) for p in ps]
        prods = [jnp.dot(jnp.concatenate([x.astype(BF16), pb], axis=0), _block_diag(pb, nblk),
                         preferred_element_type=F32) for x, pb in zip(xs, pbs)]
        xs = [x + prod[:c] for x, prod in zip(xs, prods)]
        ps = [prod[c:] for prod in prods]
        m *= 2
    return [x + jnp.dot(x.astype(BF16), _block_diag(p.astype(BF16), nblk), preferred_element_type=F32)
            for x, p in zip(xs, ps)]


def _gdn_scan_kernel(qf_ref, kf_ref, vf_ref, gf_ref, grf_ref, qb_ref, kb_ref, vb_ref, gb_ref, grb_ref,
                     of_ref, ob_ref, s_ref, stk_ref, wq_ref, vbeta_ref, *, tt, nb):
    i = pl.program_id(1)
    nc = tt // GDN_CHUNK
    C = GDN_CHUNK
    dh = GDN_HEAD_DIM
    H = GDN_HEADS

    @pl.when(i == 0)
    def _():
        s_ref[...] = jnp.zeros_like(s_ref)

    ri = lax.broadcasted_iota(jnp.int32, (C, H * C), 0)
    ci = lax.broadcasted_iota(jnp.int32, (C, H * C), 1) % C
    mask_incl = (ri >= ci, ri <= ci)
    mask_strict = (ri > ci, ri < ci)
    dirs = ((qf_ref, kf_ref, vf_ref, gf_ref, grf_ref, of_ref), (qb_ref, kb_ref, vb_ref, gb_ref, grb_ref, ob_ref))
    streams = [(b, d) for b in range(nb) for d in range(2)]
    half_lo = lax.broadcasted_iota(jnp.int32, (C, dh), 1) < C
    nt_dims = (((1,), (1,)), ((), ()))
    tn_dims = (((0,), (0,)), ((), ()))

    def chunk_end_decay(gt, d):
        return gt[C - 1:C, :] if d == 0 else gt[0:1, :]

    def prep_group(cg, carry):
        items = [(cg * GDN_PREP_CHUNKS + j, u) for j in range(GDN_PREP_CHUNKS) for u in range(len(streams))]
        lhs_l, bdk_l, gam_l, kd_l = [], [], [], []
        for c, u in items:
            b, d = streams[u]
            q_ref, k_ref, v_ref, g_ref, gr_ref, _ = dirs[d]
            r0 = pl.multiple_of(c * C, C)
            gt = g_ref[0, b, pl.ds(r0, C), :]
            gr = gr_ref[0, b, c]
            q = q_ref[b, pl.ds(r0, C), :]
            k = k_ref[b, pl.ds(r0, C), :]
            v = v_ref[b, pl.ds(r0, C), :].astype(F32)
            kf = k.astype(F32)
            gc_n = _lane_blocks(gt[:, 0:H], dh)
            beta = _lane_blocks(gt[:, H:2 * H], dh)
            gend_n = _lane_blocks(chunk_end_decay(gt, d)[:, 0:H], dh)
            e_n = jnp.exp(gc_n)
            kb = kf * beta
            kbe = (kb * e_n).astype(BF16)
            qd = (q.astype(F32) * e_n).astype(BF16)
            kdec = (kf * jnp.exp(gend_n - gc_n)).astype(BF16)
            gc_c = jnp.concatenate(
                [jnp.where(half_lo, gc_n[:, (2 * p) * dh:(2 * p + 1) * dh], gc_n[:, (2 * p + 1) * dh:(2 * p + 2) * dh])
                 for p in range(H // 2)], axis=1)
            for h in range(H):
                cols = slice(h * dh, (h + 1) * dh)
                stk_ref[c, u * H + h] = jnp.concatenate([kbe[:, cols], qd[:, cols]], axis=0)
            vbeta_ref[c, u] = v * beta
            lhs_l.append(jnp.concatenate([kb.astype(BF16), q], axis=0))
            bdk_l.append(_block_diag(k, H))
            kd_l.append(jnp.concatenate([kdec[:, h * dh:(h + 1) * dh] for h in range(H)], axis=0))
            gam_l.append(jnp.where(mask_incl[d], jnp.exp(gc_c - gr[0:1, :]), 0.0))
        kq_l = [lax.dot_general(lhs, bdk, nt_dims, preferred_element_type=F32)
                for lhs, bdk in zip(lhs_l, bdk_l)]
        a_l = [jnp.where(mask_strict[streams[u][1]], kq[:C] * gam, 0.0)
               for kq, gam, (_, u) in zip(kq_l, gam_l, items)]
        qk_l = [(kq[C:] * gam).astype(BF16) for kq, gam in zip(kq_l, gam_l)]
        bdt_l = [_block_diag(t.astype(BF16), H) for t in _unit_tri_inverses(a_l, H)]
        wc_l = [lax.dot_general(kd, bdt, tn_dims, preferred_element_type=F32) for kd, bdt in zip(kd_l, bdt_l)]
        qt_l = [jnp.dot(qk, bdt, preferred_element_type=F32) for qk, bdt in zip(qk_l, bdt_l)]
        for (c, u), wc, qt in zip(items, wc_l, qt_l):
            wq_ref[c, u] = jnp.concatenate([wc, qt], axis=0).astype(BF16)
        return carry

    lax.fori_loop(0, nc // GDN_PREP_CHUNKS, prep_group, 0)

    def scan_step(step, carry):
        cs = (step, nc - 1 - step)
        egs, x1_l = [], []
        for u, (b, d) in enumerate(streams):
            r0 = pl.multiple_of(cs[d] * C, C)
            egs.append(jnp.exp(chunk_end_decay(dirs[d][3][0, b, pl.ds(r0, C), :], d)))
        s_l = [[s_ref[u * H + h] for h in range(H)] for u in range(len(streams))]
        for u, (b, d) in enumerate(streams):
            x1_l.append([jnp.dot(stk_ref[cs[d], u * H + h], s_l[u][h].astype(BF16), preferred_element_type=F32)
                         for h in range(H)])
        res_l = [(vbeta_ref[cs[d], u] - jnp.concatenate([x1[:C] for x1 in x1_l[u]], axis=1)).astype(BF16)
                 for u, (b, d) in enumerate(streams)]
        z_l = [jnp.dot(wq_ref[cs[d], u], _block_diag(res, H), preferred_element_type=F32)
               for (u, (b, d)), res in zip(enumerate(streams), res_l)]
        for u, (b, d) in enumerate(streams):
            z = z_l[u]
            for h in range(H):
                s_ref[u * H + h] = s_l[u][h] * egs[u][:, h:h + 1] + z[:dh, h * dh:(h + 1) * dh]
            r0 = pl.multiple_of(cs[d] * C, C)
            dirs[d][5][b, pl.ds(r0, C), :] = jnp.concatenate([x1[C:] for x1 in x1_l[u]], axis=1) + z[dh:]
        return carry

    lax.fori_loop(0, nc, scan_step, 0)


def _gdn_scan(qkvp, gates, gcr):
    b, seq, _ = qkvp.shape
    tt = TT_SCAN
    nb = GDN_BATCH_PER_STEP if b % GDN_BATCH_PER_STEP == 0 else 1
    nt = seq // tt
    nc = tt // GDN_CHUNK
    C = GDN_CHUNK
    dh = GDN_HEAD_DIM
    H = GDN_HEADS
    nstream = 2 * nb

    def dir_specs(d):
        tmap = (lambda i: i) if d == 0 else (lambda i: nt - 1 - i)
        return [
            pl.BlockSpec((nb, tt, GDN_WIDTH), lambda bi, i: (bi, tmap(i), 0)),
            pl.BlockSpec((nb, tt, GDN_WIDTH), lambda bi, i: (bi, tmap(i), 1)),
            pl.BlockSpec((nb, tt, GDN_WIDTH), lambda bi, i: (bi, tmap(i), 2)),
            pl.BlockSpec((1, nb, tt, LANES), lambda bi, i: (d, bi, tmap(i), 0)),
            pl.BlockSpec((1, nb, nc, 8, H * C), lambda bi, i: (d, bi, tmap(i), 0, 0)),
        ]

    out_sds = jax.ShapeDtypeStruct((b, seq, GDN_WIDTH), F32)
    return pl.pallas_call(
        functools.partial(_gdn_scan_kernel, tt=tt, nb=nb),
        grid=(b // nb, nt),
        in_specs=dir_specs(0) + dir_specs(1),
        out_specs=[pl.BlockSpec((nb, tt, GDN_WIDTH), lambda bi, i: (bi, i, 0)),
                   pl.BlockSpec((nb, tt, GDN_WIDTH), lambda bi, i: (bi, nt - 1 - i, 0))],
        out_shape=[out_sds, out_sds],
        scratch_shapes=[
            pltpu.VMEM((nstream * H, dh, dh), F32),
            pltpu.VMEM((nc, nstream * H, 2 * C, dh), BF16),
            pltpu.VMEM((nc, nstream, dh + C, H * C), BF16),
            pltpu.VMEM((nc, nstream, C, H * dh), F32),
        ],
        compiler_params=_cparams(("parallel", "arbitrary")),
        name="gdn_scan",
    )(qkvp, qkvp, qkvp, gates, gcr, qkvp, qkvp, qkvp, gates, gcr)


def _na_kernel(q_ref, k_ref, v_ref, bm_ref, g_ref, o_ref, *, rows):
    W = GRID_W
    band = NA_WIN_R * W
    hd = NA_HEAD_DIM
    lane_q = lax.broadcasted_iota(jnp.int32, (W, LANES), 1)
    first = lane_q < hd
    scale = jnp.asarray(hd ** -0.5, BF16)

    nt_dims = (((1,), (1,)), ((), ()))

    def row_group(gi, carry):
        rs = [gi * NA_ROWS_PER_STEP + j for j in range(NA_ROWS_PER_STEP)]
        r0s = [jnp.clip(r - NA_WIN_R // 2, 0, rows - NA_WIN_R) for r in rs]
        qs_l, kb_l, vb_l = [], [], []
        for r, r0 in zip(rs, r0s):
            q2 = q_ref[0, pl.ds(pl.multiple_of(r * W, W), W), :] * scale
            zero = jnp.zeros_like(q2)
            qs_l.append(jnp.concatenate([jnp.where(first, q2, zero), jnp.where(first, zero, q2)], axis=0))
            kb_l.append(k_ref[0, pl.ds(pl.multiple_of(r0 * W, W), band), :])
            vb_l.append(v_ref[0, pl.ds(pl.multiple_of(r0 * W, W), band), :])
        s_l = [lax.dot_general(qs, kb, nt_dims, preferred_element_type=F32) for qs, kb in zip(qs_l, kb_l)]
        s_l = [s + bm_ref[0, r - r0] for s, r, r0 in zip(s_l, rs, r0s)]
        m_l = [jnp.max(s, axis=-1, keepdims=True) for s in s_l]
        p_l = [jnp.exp(s - m) for s, m in zip(s_l, m_l)]
        l_l = [jnp.sum(p, axis=-1, keepdims=True) for p in p_l]
        pb_l = [p.astype(BF16) for p in p_l]
        o0_l = [jnp.dot(pb[:W], vb, preferred_element_type=F32) for pb, vb in zip(pb_l, vb_l)]
        o1_l = [jnp.dot(pb[W:], vb, preferred_element_type=F32) for pb, vb in zip(pb_l, vb_l)]
        for r, o0, o1, l in zip(rs, o0_l, o1_l, l_l):
            o = jnp.where(first, o0 / l[:W], o1 / l[W:])
            sq = o * o
            ms0 = jnp.sum(jnp.where(first, sq, 0.0), axis=-1, keepdims=True)
            ms1 = jnp.sum(jnp.where(first, 0.0, sq), axis=-1, keepdims=True)
            ms = jnp.where(first, ms0, ms1) * (1.0 / hd)
            o_ref[0, pl.ds(pl.multiple_of(r * W, W), W), :] = (o * lax.rsqrt(ms + RMS_EPS) * g_ref[...]).astype(BF16)
        return carry

    lax.fori_loop(0, rows // NA_ROWS_PER_STEP, row_group, 0)


def _na(h_na, bias_tab, g_row):
    b, seq, _ = h_na.shape
    rows = seq // GRID_W
    assert rows >= NA_WIN_R
    npair = NA_WIDTH // LANES
    band = NA_WIN_R * GRID_W
    return pl.pallas_call(
        functools.partial(_na_kernel, rows=rows),
        grid=(b, npair),
        in_specs=[
            pl.BlockSpec((1, seq, LANES), lambda bi, p: (bi, 0, p)),
            pl.BlockSpec((1, seq, LANES), lambda bi, p: (bi, 0, npair + p)),
            pl.BlockSpec((1, seq, LANES), lambda bi, p: (bi, 0, 2 * npair + p)),
            pl.BlockSpec((1, NA_WIN_R, 2 * GRID_W, band), lambda bi, p: (p, 0, 0, 0)),
            pl.BlockSpec((1, LANES), lambda bi, p: (0, 0)),
        ],
        out_specs=pl.BlockSpec((1, seq, LANES), lambda bi, p: (bi, 0, p)),
        out_shape=jax.ShapeDtypeStruct((b, seq, NA_WIDTH), BF16),
        compiler_params=_cparams(("parallel", "parallel")),
        name="natten",
    )(h_na, h_na, h_na, bias_tab, g_row)


def _na_bias_table(rpb_l):
    W = GRID_W
    vi = np.arange(NA_WIN_R)
    kr = np.arange(NA_WIN_R)
    dr = kr[None, :] - vi[:, None] + NA_WIN_R - 1
    qc = np.arange(W)
    kc = np.arange(W)
    win_start = np.clip(qc - NA_WIN_C // 2, 0, W - NA_WIN_C)
    in_win = (kc[None, :] >= win_start[:, None]) & (kc[None, :] < win_start[:, None] + NA_WIN_C)
    dc = kc[None, :] - qc[:, None] + NA_WIN_C - 1
    rsel = (dr[:, :, None] == np.arange(2 * NA_WIN_R - 1)).astype(np.float32)
    csel = ((dc[:, :, None] == np.arange(2 * NA_WIN_C - 1)) & in_win[:, :, None]).astype(np.float32)
    tab = jnp.einsum("hab,vka,qcb->hvqkc", rpb_l.astype(F32), rsel, csel,
                     precision=lax.Precision.HIGHEST)
    tab = jnp.where(jnp.asarray(in_win)[None, None, :, None, :], tab, -jnp.inf)
    tab = tab.reshape(NA_HEADS // 2, 2, NA_WIN_R, W, NA_WIN_R * W)
    tab = jnp.transpose(tab, (0, 2, 1, 3, 4))
    return tab.reshape(NA_HEADS // 2, NA_WIN_R, 2 * W, NA_WIN_R * W)


def _mix_ffn_kernel(x_ref, of_ref, ob_ref, z_ref, ona_ref, gg_ref, wo_ref, l1g_ref, l1b_ref,
                    w1_ref, b1_ref, w2_ref, b2_ref, l2g_ref, l2b_ref, out_ref):
    dh = GDN_HEAD_DIM
    o = of_ref[...] + ob_ref[...]
    z = z_ref[...].astype(F32)
    gate = z * _sigmoid(z)
    parts = []
    for h in range(GDN_HEADS):
        cols = slice(h * dh, (h + 1) * dh)
        oh = o[:, cols]
        ms = jnp.mean(oh * oh, axis=-1, keepdims=True)
        parts.append((oh * lax.rsqrt(ms + RMS_EPS) * gg_ref[:, cols] * gate[:, cols]).astype(BF16))
    og = jnp.concatenate(parts, axis=-1)
    mix = jnp.dot(og, wo_ref[0:GDN_WIDTH, :], preferred_element_type=F32)
    mix = mix + jnp.dot(ona_ref[...], wo_ref[GDN_WIDTH:, :], preferred_element_type=F32)
    x1 = _layer_norm(DEEPNORM_ALPHA * x_ref[...] + mix, l1g_ref[...], l1b_ref[...])
    x1b = x1.astype(BF16)
    d_ff = w1_ref.shape[1]
    acc = jnp.zeros(x1.shape, F32)
    for f in range(d_ff // FF_CHUNK):
        fs = slice(f * FF_CHUNK, (f + 1) * FF_CHUNK)
        hf = jnp.dot(x1b, w1_ref[:, fs], preferred_element_type=F32) + b1_ref[:, fs]
        hf = jnp.square(jnp.maximum(hf, 0.0)).astype(BF16)
        acc = acc + jnp.dot(hf, w2_ref[fs, :], preferred_element_type=F32)
    y = DEEPNORM_ALPHA * x1 + (acc + b2_ref[...])
    out_ref[...] = _layer_norm(y, l2g_ref[...], l2b_ref[...])


def _mix_ffn(x2d, o_f, o_b, z, ona, gg, wo, l1g, l1b, w1, b1, w2, b2, l2g, l2b):
    bt, dm = x2d.shape
    d_ff = w1.shape[1]
    tm = TM_FFN
    row = lambda i: (i, 0)
    return pl.pallas_call(
        _mix_ffn_kernel,
        grid=(bt // tm,),
        in_specs=[
            pl.BlockSpec((tm, dm), row),
            pl.BlockSpec((tm, GDN_WIDTH), row),
            pl.BlockSpec((tm, GDN_WIDTH), row),
            pl.BlockSpec((tm, GDN_WIDTH), row),
            pl.BlockSpec((tm, NA_WIDTH), row),
            _const_spec((1, GDN_WIDTH)),
            _const_spec((dm, dm)),
            _const_spec((1, dm)), _const_spec((1, dm)),
            _const_spec((dm, d_ff)), _const_spec((1, d_ff)),
            _const_spec((d_ff, dm)), _const_spec((1, dm)),
            _const_spec((1, dm)), _const_spec((1, dm)),
        ],
        out_specs=pl.BlockSpec((tm, dm), row),
        out_shape=jax.ShapeDtypeStruct((bt, dm), F32),
        compiler_params=_cparams(("parallel",)),
        name="mix_ffn",
    )(x2d, o_f, o_b, z, ona, gg, wo, l1g, l1b, w1, b1, w2, b2, l2g, l2b)


def _pack_w_in(w_l):
    dm = w_l.shape[0]
    h = GDN_HEADS
    a0 = 4 * GDN_WIDTH
    b0 = a0 + 2 * h
    na0 = b0 + 2 * h
    pad = jnp.zeros((dm, LANES - 2 * h), w_l.dtype)
    gate_f = jnp.concatenate([w_l[:, a0:a0 + h], w_l[:, b0:b0 + h], pad], axis=1)
    gate_b = jnp.concatenate([w_l[:, a0 + h:a0 + 2 * h], w_l[:, b0 + h:b0 + 2 * h], pad], axis=1)
    w = jnp.concatenate([w_l[:, :a0], gate_f, gate_b, w_l[:, na0:]], axis=1)
    return w.astype(BF16)


def _gate_rows(p):
    return jnp.pad(p.astype(F32), ((0, 0), (0, LANES - p.shape[1])))


def kernel(x, ln_in_g, ln_in_b, w_in, conv_w, a_log, dt_bias, gdn_norm_g, rpb, na_norm_g, w_out,
           ln1_g, ln1_b, w1, b1, w2, b2, ln2_g, ln2_b):
    B, T, dm = x.shape
    bt = B * T
    nchunks = T // GDN_CHUNK
    row = lambda v: v.reshape(1, -1).astype(F32)
    xs = x.reshape(bt, dm)
    for l in range(DEPTH):
        w_l = _pack_w_in(w_in[l])
        outs = _inproj(xs, row(ln_in_g), row(ln_in_b), w_l, _gate_rows(a_log[l]), _gate_rows(dt_bias[l]),
                       apply_ln=(l == 0))
        if l == 0:
            xs, qkv, z, gates, h_na = outs
        else:
            qkv, z, gates, h_na = outs
        conv8 = jnp.pad(conv_w[l].astype(F32), ((0, 8 - CONV_WIDTH), (0, 0)))
        qkvp = _gdn_prep(qkv.reshape(B, T, C_QKV), conv8)
        gates4 = gates.reshape(2, B, T, LANES)
        gcr = jnp.swapaxes(gates4[..., :GDN_HEADS].reshape(2, B, nchunks, GDN_CHUNK, GDN_HEADS), -1, -2)
        gcr = jnp.pad(gcr.reshape(2, B, nchunks, 1, GDN_HEADS * GDN_CHUNK), ((0, 0),) * 3 + ((0, 7), (0, 0)))
        o_f, o_b = _gdn_scan(qkvp, gates4, gcr)
        ona = _na(h_na.reshape(B, T, C_NA), _na_bias_table(rpb[l]),
                  jnp.tile(na_norm_g[l].astype(F32), LANES // NA_HEAD_DIM).reshape(1, LANES))
        xs = _mix_ffn(xs, o_f.reshape(bt, GDN_WIDTH), o_b.reshape(bt, GDN_WIDTH), z, ona.reshape(bt, NA_WIDTH),
                      jnp.tile(gdn_norm_g[l].astype(F32), GDN_HEADS).reshape(1, GDN_WIDTH),
                      w_out[l].astype(BF16), row(ln1_g[l]), row(ln1_b[l]),
                      w1[l].astype(BF16), row(b1[l]), w2[l].astype(BF16), row(b2[l]),
                      row(ln2_g[l]), row(ln2_b[l]))
    return xs.reshape(B, T, dm)
```

```python
import functools
import math

import jax
import jax.numpy as jnp
import numpy as np
from jax import lax
from jax.experimental import pallas as pl
from jax.experimental.pallas import tpu as pltpu

F32 = jnp.float32
BF16 = jnp.bfloat16

GRID_W = 64
GDN_HEAD_DIM = 128
GDN_HEADS = 4
GDN_WIDTH = GDN_HEADS * GDN_HEAD_DIM
NA_HEAD_DIM = 64
NA_HEADS = 8
NA_WIDTH = NA_HEADS * NA_HEAD_DIM
CONV_WIDTH = 5
GDN_CHUNK = 64
NA_WIN_R = 8
NA_WIN_C = 16
DEPTH = 2
DEEPNORM_ALPHA = (2 * DEPTH) ** 0.25
LN_EPS = 1e-5
RMS_EPS = 1e-6

LANES = 128
VMEM_LIMIT = 56 * 1024 * 1024

TM_PROJ = 512
TM_FFN = 512
TT_SCAN = 512
CONV_RB = 512
FF_CHUNK = 1024
NA_ROWS_PER_STEP = 8
FFN_SUBTILES = 2
GDN_BATCH_PER_STEP = 2
GDN_PREP_CHUNKS = 4

C_QKV = 3 * GDN_WIDTH
C_Z = GDN_WIDTH
C_NA = 3 * NA_WIDTH
OFF_Z = C_QKV
OFF_GF = OFF_Z + C_Z
OFF_GB = OFF_GF + LANES
OFF_NA = OFF_GB + LANES
C_TOTAL = OFF_NA + C_NA


def _cparams(sem):
    return pltpu.CompilerParams(dimension_semantics=sem, vmem_limit_bytes=VMEM_LIMIT)


def _const_spec(shape):
    nd = len(shape)
    return pl.BlockSpec(shape, lambda *_: (0,) * nd, pipeline_mode=pl.Buffered(1))


def _layer_norm(y, g, b):
    mu = jnp.mean(y, axis=-1, keepdims=True)
    yc = y - mu
    var = jnp.mean(yc * yc, axis=-1, keepdims=True)
    return yc * lax.rsqrt(var + LN_EPS) * g + b


def _sigmoid(x):
    return 1.0 / (1.0 + jnp.exp(-x))


def _chunk_cumsum(g, tpos, reverse):
    n = g.shape[0]
    s = 1
    while s < GDN_CHUNK:
        if reverse:
            g = g + jnp.where(tpos < GDN_CHUNK - s, pltpu.roll(g, n - s, 0), 0.0)
        else:
            g = g + jnp.where(tpos >= s, pltpu.roll(g, s, 0), 0.0)
        s *= 2
    return g


def _inproj_kernel(x_ref, lng_ref, lnb_ref, w_ref, alog_ref, dtb_ref, *out_refs, apply_ln):
    if apply_ln:
        xn_ref, qkv_ref, z_ref, gates_ref, na_ref = out_refs
    else:
        qkv_ref, z_ref, gates_ref, na_ref = out_refs
    x = x_ref[...]
    if apply_ln:
        x = _layer_norm(x, lng_ref[...], lnb_ref[...])
        xn_ref[...] = x
    xb = x.astype(BF16)
    qkv_ref[...] = jnp.dot(xb, w_ref[:, 0:C_QKV], preferred_element_type=F32).astype(BF16)
    z_ref[...] = jnp.dot(xb, w_ref[:, OFF_Z:OFF_Z + C_Z], preferred_element_type=F32).astype(BF16)
    na_ref[...] = jnp.dot(xb, w_ref[:, OFF_NA:OFF_NA + C_NA], preferred_element_type=F32).astype(BF16)

    tm = x.shape[0]
    tpos = lax.broadcasted_iota(jnp.int32, (tm, LANES), 0) & (GDN_CHUNK - 1)
    lane = lax.broadcasted_iota(jnp.int32, (tm, LANES), 1)
    for d in range(2):
        off = OFF_GF if d == 0 else OFF_GB
        hab = jnp.dot(xb, w_ref[:, off:off + LANES], preferred_element_type=F32)
        sp_in = hab + dtb_ref[d:d + 1, :]
        softplus = jnp.maximum(sp_in, 0.0) + jnp.log1p(jnp.exp(-jnp.abs(sp_in)))
        g = -jnp.exp(alog_ref[d:d + 1, :]) * softplus
        g = jnp.where(lane < GDN_HEADS, g, 0.0)
        gc = _chunk_cumsum(g, tpos, reverse=(d == 1))
        gates_ref[d] = jnp.where(lane < GDN_HEADS, gc, _sigmoid(hab))


def _inproj(x2d, lng, lnb, w, alog_rows, dtb_rows, apply_ln):
    bt, dm = x2d.shape
    tm = TM_PROJ
    grid = (bt // tm,)
    row = lambda i: (i, 0)
    in_specs = [
        pl.BlockSpec((tm, dm), row),
        _const_spec((1, dm)), _const_spec((1, dm)),
        _const_spec((dm, C_TOTAL)),
        _const_spec((2, LANES)), _const_spec((2, LANES)),
    ]
    out_shape = [
        jax.ShapeDtypeStruct((bt, C_QKV), BF16),
        jax.ShapeDtypeStruct((bt, C_Z), BF16),
        jax.ShapeDtypeStruct((2, bt, LANES), F32),
        jax.ShapeDtypeStruct((bt, C_NA), BF16),
    ]
    out_specs = [
        pl.BlockSpec((tm, C_QKV), row),
        pl.BlockSpec((tm, C_Z), row),
        pl.BlockSpec((2, tm, LANES), lambda i: (0, i, 0)),
        pl.BlockSpec((tm, C_NA), row),
    ]
    if apply_ln:
        out_shape = [jax.ShapeDtypeStruct((bt, dm), F32)] + out_shape
        out_specs = [pl.BlockSpec((tm, dm), row)] + out_specs
    return pl.pallas_call(
        functools.partial(_inproj_kernel, apply_ln=apply_ln),
        grid=grid, in_specs=in_specs, out_specs=out_specs, out_shape=out_shape,
        compiler_params=_cparams(("parallel",)),
        name="inproj_ln" if apply_ln else "inproj",
    )(x2d, lng, lnb, w, alog_rows, dtb_rows)


CONV_HALO = 8


def _gdn_prep_kernel(x_ref, w_ref, o_ref, pad_ref, *, seq):
    j = pl.program_id(1)
    heads = GDN_HEADS
    zeros = jnp.zeros((CONV_HALO, LANES), F32)
    pad_ref[0:CONV_HALO, :] = zeros
    pad_ref[seq + CONV_HALO:seq + 2 * CONV_HALO, :] = zeros
    nblk = seq // CONV_RB

    def fill(i, c):
        r0 = pl.multiple_of(i * CONV_RB, CONV_RB)
        pad_ref[pl.ds(r0 + CONV_HALO, CONV_RB), :] = x_ref[0, pl.ds(r0, CONV_RB), :].astype(F32)
        return c

    lax.fori_loop(0, nblk, fill, 0)

    do_norm = j < 2 * heads
    post = jnp.where(j < heads, GDN_HEAD_DIM ** -0.5, 1.0).astype(F32)
    first_tap = CONV_HALO - CONV_WIDTH // 2

    def body(i, c):
        r0 = pl.multiple_of(i * CONV_RB, CONV_RB)
        y = pad_ref[pl.ds(r0 + first_tap, CONV_RB), :] * w_ref[0:1, :]
        for tap in range(1, CONV_WIDTH):
            y = y + pad_ref[pl.ds(r0 + first_tap + tap, CONV_RB), :] * w_ref[tap:tap + 1, :]
        y = y * _sigmoid(y)
        ss = jnp.sum(y * y, axis=-1, keepdims=True)
        fac = jnp.where(do_norm, lax.rsqrt(ss + RMS_EPS), 1.0) * post
        o_ref[0, pl.ds(r0, CONV_RB), :] = (y * fac).astype(BF16)
        return c

    lax.fori_loop(0, nblk, body, 0)


def _gdn_prep(qkv, conv_w8):
    b, seq, c = qkv.shape
    nblk = c // LANES
    return pl.pallas_call(
        functools.partial(_gdn_prep_kernel, seq=seq),
        grid=(b, nblk),
        in_specs=[pl.BlockSpec((1, seq, LANES), lambda i, j: (i, 0, j)),
                  pl.BlockSpec((8, LANES), lambda i, j: (0, j))],
        out_specs=pl.BlockSpec((1, seq, LANES), lambda i, j: (i, 0, j)),
        out_shape=jax.ShapeDtypeStruct((b, seq, c), BF16),
        scratch_shapes=[pltpu.VMEM((seq + 2 * CONV_HALO, LANES), F32)],
        compiler_params=_cparams(("parallel", "parallel")),
        name="gdn_prep",
    )(qkv, conv_w8)


def _block_diag(x, nblk):
    w = x.shape[1] // nblk
    blk = lax.broadcasted_iota(jnp.int32, x.shape, 1) // w
    zero = jnp.zeros_like(x)
    return jnp.concatenate([jnp.where(blk == h, x, zero) for h in range(nblk)], axis=0)


def _lane_blocks(cols, width):
    r, n = cols.shape
    return jnp.concatenate([jnp.broadcast_to(cols[:, j:j + 1], (r, width)) for j in range(n)], axis=1)


def _unit_tri_inverses(mats, nblk):
    c = mats[0].shape[0]
    ri = lax.broadcasted_iota(jnp.int32, (c, nblk * c), 0)
    ci = lax.broadcasted_iota(jnp.int32, (c, nblk * c), 1) % c
    eye = (ri == ci).astype(F32)
    ps = [-a for a in mats]
    xs = [eye + p for p in ps]
    pbs = [p.astype(BF16) for p in ps]
    ps = [jnp.dot(pb, _block_diag(pb, nblk), preferred_element_type=F32) for pb in pbs]
    m = 2
    while 2 * m < c:
        pbs = [p.astype(BF16) for p in ps]
        prods = [jnp.dot(jnp.concatenate([x.astype(BF16), pb], axis=0), _block_diag(pb, nblk),
                         preferred_element_type=F32) for x, pb in zip(xs, pbs)]
        xs = [x + prod[:c] for x, prod in zip(xs, prods)]
        ps = [prod[c:] for prod in prods]
        m *= 2
    return [x + jnp.dot(x.astype(BF16), _block_diag(p.astype(BF16), nblk), preferred_element_type=F32)
            for x, p in zip(xs, ps)]


def _gdn_scan_kernel(qf_ref, kf_ref, vf_ref, gf_ref, grf_ref, qb_ref, kb_ref, vb_ref, gb_ref, grb_ref,
                     of_ref, ob_ref, s_ref, stk_ref, wq_ref, vbeta_ref, *, tt, nb):
    i = pl.program_id(1)
    nc = tt // GDN_CHUNK
    C = GDN_CHUNK
    dh = GDN_HEAD_DIM
    H = GDN_HEADS

    @pl.when(i == 0)
    def _():
        s_ref[...] = jnp.zeros_like(s_ref)

    ri = lax.broadcasted_iota(jnp.int32, (C, H * C), 0)
    ci = lax.broadcasted_iota(jnp.int32, (C, H * C), 1) % C
    mask_incl = (ri >= ci, ri <= ci)
    mask_strict = (ri > ci, ri < ci)
    dirs = ((qf_ref, kf_ref, vf_ref, gf_ref, grf_ref, of_ref), (qb_ref, kb_ref, vb_ref, gb_ref, grb_ref, ob_ref))
    streams = [(b, d) for b in range(nb) for d in range(2)]
    half_lo = lax.broadcasted_iota(jnp.int32, (C, dh), 1) < C
    nt_dims = (((1,), (1,)), ((), ()))
    tn_dims = (((0,), (0,)), ((), ()))

    def chunk_end_decay(gt, d):
        return gt[C - 1:C, :] if d == 0 else gt[0:1, :]

    def prep_group(cg, carry):
        items = [(cg * GDN_PREP_CHUNKS + j, u) for j in range(GDN_PREP_CHUNKS) for u in range(len(streams))]
        lhs_l, bdk_l, gam_l, kd_l = [], [], [], []
        for c, u in items:
            b, d = streams[u]
            q_ref, k_ref, v_ref, g_ref, gr_ref, _ = dirs[d]
            r0 = pl.multiple_of(c * C, C)
            gt = g_ref[0, b, pl.ds(r0, C), :]
            gr = gr_ref[0, b, c]
            q = q_ref[b, pl.ds(r0, C), :]
            k = k_ref[b, pl.ds(r0, C), :]
            v = v_ref[b, pl.ds(r0, C), :].astype(F32)
            kf = k.astype(F32)
            gc_n = _lane_blocks(gt[:, 0:H], dh)
            beta = _lane_blocks(gt[:, H:2 * H], dh)
            gend_n = _lane_blocks(chunk_end_decay(gt, d)[:, 0:H], dh)
            e_n = jnp.exp(gc_n)
            kb = kf * beta
            kbe = (kb * e_n).astype(BF16)
            qd = (q.astype(F32) * e_n).astype(BF16)
            kdec = (kf * jnp.exp(gend_n - gc_n)).astype(BF16)
            gc_c = jnp.concatenate(
                [jnp.where(half_lo, gc_n[:, (2 * p) * dh:(2 * p + 1) * dh], gc_n[:, (2 * p + 1) * dh:(2 * p + 2) * dh])
                 for p in range(H // 2)], axis=1)
            for h in range(H):
                cols = slice(h * dh, (h + 1) * dh)
                stk_ref[c, u * H + h] = jnp.concatenate([kbe[:, cols], qd[:, cols]], axis=0)
            vbeta_ref[c, u] = v * beta
            lhs_l.append(jnp.concatenate([kb.astype(BF16), q], axis=0))
            bdk_l.append(_block_diag(k, H))
            kd_l.append(jnp.concatenate([kdec[:, h * dh:(h + 1) * dh] for h in range(H)], axis=0))
            gam_l.append(jnp.where(mask_incl[d], jnp.exp(gc_c - gr[0:1, :]), 0.0))
        kq_l = [lax.dot_general(lhs, bdk, nt_dims, preferred_element_type=F32)
                for lhs, bdk in zip(lhs_l, bdk_l)]
        a_l = [jnp.where(mask_strict[streams[u][1]], kq[:C] * gam, 0.0)
               for kq, gam, (_, u) in zip(kq_l, gam_l, items)]
        qk_l = [(kq[C:] * gam).astype(BF16) for kq, gam in zip(kq_l, gam_l)]
        bdt_l = [_block_diag(t.astype(BF16), H) for t in _unit_tri_inverses(a_l, H)]
        wc_l = [lax.dot_general(kd, bdt, tn_dims, preferred_element_type=F32) for kd, bdt in zip(kd_l, bdt_l)]
        qt_l = [jnp.dot(qk, bdt, preferred_element_type=F32) for qk, bdt in zip(qk_l, bdt_l)]
        for (c, u), wc, qt in zip(items, wc_l, qt_l):
            wq_ref[c, u] = jnp.concatenate([wc, qt], axis=0).astype(BF16)
        return carry

    lax.fori_loop(0, nc // GDN_PREP_CHUNKS, prep_group, 0)

    def scan_step(step, carry):
        cs = (step, nc - 1 - step)
        egs, x1_l = [], []
        for u, (b, d) in enumerate(streams):
            r0 = pl.multiple_of(cs[d] * C, C)
            egs.append(jnp.exp(chunk_end_decay(dirs[d][3][0, b, pl.ds(r0, C), :], d)))
        s_l = [[s_ref[u * H + h] for h in range(H)] for u in range(len(streams))]
        for u, (b, d) in enumerate(streams):
            x1_l.append([jnp.dot(stk_ref[cs[d], u * H + h], s_l[u][h].astype(BF16), preferred_element_type=F32)
                         for h in range(H)])
        res_l = [(vbeta_ref[cs[d], u] - jnp.concatenate([x1[:C] for x1 in x1_l[u]], axis=1)).astype(BF16)
                 for u, (b, d) in enumerate(streams)]
        z_l = [jnp.dot(wq_ref[cs[d], u], _block_diag(res, H), preferred_element_type=F32)
               for (u, (b, d)), res in zip(enumerate(streams), res_l)]
        for u, (b, d) in enumerate(streams):
            z = z_l[u]
            for h in range(H):
                s_ref[u * H + h] = s_l[u][h] * egs[u][:, h:h + 1] + z[:dh, h * dh:(h + 1) * dh]
            r0 = pl.multiple_of(cs[d] * C, C)
            dirs[d][5][b, pl.ds(r0, C), :] = jnp.concatenate([x1[C:] for x1 in x1_l[u]], axis=1) + z[dh:]
        return carry

    lax.fori_loop(0, nc, scan_step, 0)


def _gdn_scan(qkvp, gates, gcr):
    b, seq, _ = qkvp.shape
    tt = TT_SCAN
    nb = GDN_BATCH_PER_STEP if b % GDN_BATCH_PER_STEP == 0 else 1
    nt = seq // tt
    nc = tt // GDN_CHUNK
    C = GDN_CHUNK
    dh = GDN_HEAD_DIM
    H = GDN_HEADS
    nstream = 2 * nb

    def dir_specs(d):
        tmap = (lambda i: i) if d == 0 else (lambda i: nt - 1 - i)
        return [
            pl.BlockSpec((nb, tt, GDN_WIDTH), lambda bi, i: (bi, tmap(i), 0)),
            pl.BlockSpec((nb, tt, GDN_WIDTH), lambda bi, i: (bi, tmap(i), 1)),
            pl.BlockSpec((nb, tt, GDN_WIDTH), lambda bi, i: (bi, tmap(i), 2)),
            pl.BlockSpec((1, nb, tt, LANES), lambda bi, i: (d, bi, tmap(i), 0)),
            pl.BlockSpec((1, nb, nc, 8, H * C), lambda bi, i: (d, bi, tmap(i), 0, 0)),
        ]

    out_sds = jax.ShapeDtypeStruct((b, seq, GDN_WIDTH), F32)
    return pl.pallas_call(
        functools.partial(_gdn_scan_kernel, tt=tt, nb=nb),
        grid=(b // nb, nt),
        in_specs=dir_specs(0) + dir_specs(1),
        out_specs=[pl.BlockSpec((nb, tt, GDN_WIDTH), lambda bi, i: (bi, i, 0)),
                   pl.BlockSpec((nb, tt, GDN_WIDTH), lambda bi, i: (bi, nt - 1 - i, 0))],
        out_shape=[out_sds, out_sds],
        scratch_shapes=[
            pltpu.VMEM((nstream * H, dh, dh), F32),
            pltpu.VMEM((nc, nstream * H, 2 * C, dh), BF16),
            pltpu.VMEM((nc, nstream, dh + C, H * C), BF16),
            pltpu.VMEM((nc, nstream, C, H * dh), F32),
        ],
        compiler_params=_cparams(("parallel", "arbitrary")),
        name="gdn_scan",
    )(qkvp, qkvp, qkvp, gates, gcr, qkvp, qkvp, qkvp, gates, gcr)


def _na_kernel(q_ref, k_ref, v_ref, bm_ref, g_ref, o_ref, *, rows):
    W = GRID_W
    band = NA_WIN_R * W
    hd = NA_HEAD_DIM
    lane_q = lax.broadcasted_iota(jnp.int32, (W, LANES), 1)
    first = lane_q < hd
    scale = jnp.asarray(hd ** -0.5, BF16)

    nt_dims = (((1,), (1,)), ((), ()))

    def row_group(gi, carry):
        rs = [gi * NA_ROWS_PER_STEP + j for j in range(NA_ROWS_PER_STEP)]
        r0s = [jnp.clip(r - NA_WIN_R // 2, 0, rows - NA_WIN_R) for r in rs]
        qs_l, kb_l, vb_l = [], [], []
        for r, r0 in zip(rs, r0s):
            q2 = q_ref[0, pl.ds(pl.multiple_of(r * W, W), W), :] * scale
            zero = jnp.zeros_like(q2)
            qs_l.append(jnp.concatenate([jnp.where(first, q2, zero), jnp.where(first, zero, q2)], axis=0))
            kb_l.append(k_ref[0, pl.ds(pl.multiple_of(r0 * W, W), band), :])
            vb_l.append(v_ref[0, pl.ds(pl.multiple_of(r0 * W, W), band), :])
        s_l = [lax.dot_general(qs, kb, nt_dims, preferred_element_type=F32) for qs, kb in zip(qs_l, kb_l)]
        s_l = [s + bm_ref[0, r - r0] for s, r, r0 in zip(s_l, rs, r0s)]
        m_l = [jnp.max(s, axis=-1, keepdims=True) for s in s_l]
        p_l = [jnp.exp(s - m) for s, m in zip(s_l, m_l)]
        l_l = [jnp.sum(p, axis=-1, keepdims=True) for p in p_l]
        pb_l = [p.astype(BF16) for p in p_l]
        o0_l = [jnp.dot(pb[:W], vb, preferred_element_type=F32) for pb, vb in zip(pb_l, vb_l)]
        o1_l = [jnp.dot(pb[W:], vb, preferred_element_type=F32) for pb, vb in zip(pb_l, vb_l)]
        for r, o0, o1, l in zip(rs, o0_l, o1_l, l_l):
            o = jnp.where(first, o0 / l[:W], o1 / l[W:])
            sq = o * o
            ms0 = jnp.sum(jnp.where(first, sq, 0.0), axis=-1, keepdims=True)
            ms1 = jnp.sum(jnp.where(first, 0.0, sq), axis=-1, keepdims=True)
            ms = jnp.where(first, ms0, ms1) * (1.0 / hd)
            o_ref[0, pl.ds(pl.multiple_of(r * W, W), W), :] = (o * lax.rsqrt(ms + RMS_EPS) * g_ref[...]).astype(BF16)
        return carry

    lax.fori_loop(0, rows // NA_ROWS_PER_STEP, row_group, 0)


def _na(h_na, bias_tab, g_row):
    b, seq, _ = h_na.shape
    rows = seq // GRID_W
    assert rows >= NA_WIN_R
    npair = NA_WIDTH // LANES
    band = NA_WIN_R * GRID_W
    return pl.pallas_call(
        functools.partial(_na_kernel, rows=rows),
        grid=(b, npair),
        in_specs=[
            pl.BlockSpec((1, seq, LANES), lambda bi, p: (bi, 0, p)),
            pl.BlockSpec((1, seq, LANES), lambda bi, p: (bi, 0, npair + p)),
            pl.BlockSpec((1, seq, LANES), lambda bi, p: (bi, 0, 2 * npair + p)),
            pl.BlockSpec((1, NA_WIN_R, 2 * GRID_W, band), lambda bi, p: (p, 0, 0, 0)),
            pl.BlockSpec((1, LANES), lambda bi, p: (0, 0)),
        ],
        out_specs=pl.BlockSpec((1, seq, LANES), lambda bi, p: (bi, 0, p)),
        out_shape=jax.ShapeDtypeStruct((b, seq, NA_WIDTH), BF16),
        compiler_params=_cparams(("parallel", "parallel")),
        name="natten",
    )(h_na, h_na, h_na, bias_tab, g_row)


def _na_bias_table(rpb_l):
    W = GRID_W
    vi = np.arange(NA_WIN_R)
    kr = np.arange(NA_WIN_R)
    dr = kr[None, :] - vi[:, None] + NA_WIN_R - 1
    qc = np.arange(W)
    kc = np.arange(W)
    win_start = np.clip(qc - NA_WIN_C // 2, 0, W - NA_WIN_C)
    in_win = (kc[None, :] >= win_start[:, None]) & (kc[None, :] < win_start[:, None] + NA_WIN_C)
    dc = kc[None, :] - qc[:, None] + NA_WIN_C - 1
    rsel = (dr[:, :, None] == np.arange(2 * NA_WIN_R - 1)).astype(np.float32)
    csel = ((dc[:, :, None] == np.arange(2 * NA_WIN_C - 1)) & in_win[:, :, None]).astype(np.float32)
    tab = jnp.einsum("hab,vka,qcb->hvqkc", rpb_l.astype(F32), rsel, csel,
                     precision=lax.Precision.HIGHEST)
    tab = jnp.where(jnp.asarray(in_win)[None, None, :, None, :], tab, -jnp.inf)
    tab = tab.reshape(NA_HEADS // 2, 2, NA_WIN_R, W, NA_WIN_R * W)
    tab = jnp.transpose(tab, (0, 2, 1, 3, 4))
    return tab.reshape(NA_HEADS // 2, NA_WIN_R, 2 * W, NA_WIN_R * W)


def _mix_ffn_kernel(x_ref, of_ref, ob_ref, z_ref, ona_ref, gg_ref, wo_ref, l1g_ref, l1b_ref,
                    w1_ref, b1_ref, w2_ref, b2_ref, l2g_ref, l2b_ref, out_ref):
    dh = GDN_HEAD_DIM
    tm = x_ref.shape[0]
    d_ff = w1_ref.shape[1]
    rows = [slice(s * (tm // FFN_SUBTILES), (s + 1) * (tm // FFN_SUBTILES)) for s in range(FFN_SUBTILES)]

    def gated(rs):
        o = of_ref[rs, :] + ob_ref[rs, :]
        z = z_ref[rs, :].astype(F32)
        gate = z * _sigmoid(z)
        parts = []
        for h in range(GDN_HEADS):
            cols = slice(h * dh, (h + 1) * dh)
            oh = o[:, cols]
            ms = jnp.mean(oh * oh, axis=-1, keepdims=True)
            parts.append((oh * lax.rsqrt(ms + RMS_EPS) * gg_ref[:, cols] * gate[:, cols]).astype(BF16))
        return jnp.concatenate(parts, axis=-1)

    og_l = [gated(rs) for rs in rows]
    mix_l = [jnp.dot(og, wo_ref[0:GDN_WIDTH, :], preferred_element_type=F32)
             + jnp.dot(ona_ref[rs, :], wo_ref[GDN_WIDTH:, :], preferred_element_type=F32)
             for og, rs in zip(og_l, rows)]
    x1_l = [_layer_norm(DEEPNORM_ALPHA * x_ref[rs, :] + mix, l1g_ref[...], l1b_ref[...])
            for mix, rs in zip(mix_l, rows)]
    x1b_l = [x1.astype(BF16) for x1 in x1_l]
    acc_l = [jnp.zeros(x1.shape, F32) for x1 in x1_l]
    for f in range(d_ff // FF_CHUNK):
        fs = slice(f * FF_CHUNK, (f + 1) * FF_CHUNK)
        hf_l = [jnp.dot(x1b, w1_ref[:, fs], preferred_element_type=F32) + b1_ref[:, fs] for x1b in x1b_l]
        hf_l = [jnp.square(jnp.maximum(hf, 0.0)).astype(BF16) for hf in hf_l]
        acc_l = [acc + jnp.dot(hf, w2_ref[fs, :], preferred_element_type=F32) for acc, hf in zip(acc_l, hf_l)]
    for rs, x1, acc in zip(rows, x1_l, acc_l):
        y = DEEPNORM_ALPHA * x1 + (acc + b2_ref[...])
        out_ref[rs, :] = _layer_norm(y, l2g_ref[...], l2b_ref[...])


def _mix_ffn(x2d, o_f, o_b, z, ona, gg, wo, l1g, l1b, w1, b1, w2, b2, l2g, l2b):
    bt, dm = x2d.shape
    d_ff = w1.shape[1]
    tm = TM_FFN
    row = lambda i: (i, 0)
    return pl.pallas_call(
        _mix_ffn_kernel,
        grid=(bt // tm,),
        in_specs=[
            pl.BlockSpec((tm, dm), row),
            pl.BlockSpec((tm, GDN_WIDTH), row),
            pl.BlockSpec((tm, GDN_WIDTH), row),
            pl.BlockSpec((tm, GDN_WIDTH), row),
            pl.BlockSpec((tm, NA_WIDTH), row),
            _const_spec((1, GDN_WIDTH)),
            _const_spec((dm, dm)),
            _const_spec((1, dm)), _const_spec((1, dm)),
            _const_spec((dm, d_ff)), _const_spec((1, d_ff)),
            _const_spec((d_ff, dm)), _const_spec((1, dm)),
            _const_spec((1, dm)), _const_spec((1, dm)),
        ],
        out_specs=pl.BlockSpec((tm, dm), row),
        out_shape=jax.ShapeDtypeStruct((bt, dm), F32),
        compiler_params=_cparams(("parallel",)),
        name="mix_ffn",
    )(x2d, o_f, o_b, z, ona, gg, wo, l1g, l1b, w1, b1, w2, b2, l2g, l2b)


def _pack_w_in(w_l):
    dm = w_l.shape[0]
    h = GDN_HEADS
    a0 = 4 * GDN_WIDTH
    b0 = a0 + 2 * h
    na0 = b0 + 2 * h
    pad = jnp.zeros((dm, LANES - 2 * h), w_l.dtype)
    gate_f = jnp.concatenate([w_l[:, a0:a0 + h], w_l[:, b0:b0 + h], pad], axis=1)
    gate_b = jnp.concatenate([w_l[:, a0 + h:a0 + 2 * h], w_l[:, b0 + h:b0 + 2 * h], pad], axis=1)
    w = jnp.concatenate([w_l[:, :a0], gate_f, gate_b, w_l[:, na0:]], axis=1)
    return w.astype(BF16)


def _gate_rows(p):
    return jnp.pad(p.astype(F32), ((0, 0), (0, LANES - p.shape[1])))


def kernel(x, ln_in_g, ln_in_b, w_in, conv_w, a_log, dt_bias, gdn_norm_g, rpb, na_norm_g, w_out,
           ln1_g, ln1_b, w1, b1, w2, b2, ln2_g, ln2_b):
    B, T, dm = x.shape
    bt = B * T
    nchunks = T // GDN_CHUNK
    row = lambda v: v.reshape(1, -1).astype(F32)
    xs = x.reshape(bt, dm)
    for l in range(DEPTH):
        w_l = _pack_w_in(w_in[l])
        outs = _inproj(xs, row(ln_in_g), row(ln_in_b), w_l, _gate_rows(a_log[l]), _gate_rows(dt_bias[l]),
                       apply_ln=(l == 0))
        if l == 0:
            xs, qkv, z, gates, h_na = outs
        else:
            qkv, z, gates, h_na = outs
        conv8 = jnp.pad(conv_w[l].astype(F32), ((0, 8 - CONV_WIDTH), (0, 0)))
        qkvp = _gdn_prep(qkv.reshape(B, T, C_QKV), conv8)
        gates4 = gates.reshape(2, B, T, LANES)
        gcr = jnp.swapaxes(gates4[..., :GDN_HEADS].reshape(2, B, nchunks, GDN_CHUNK, GDN_HEADS), -1, -2)
        gcr = jnp.pad(gcr.reshape(2, B, nchunks, 1, GDN_HEADS * GDN_CHUNK), ((0, 0),) * 3 + ((0, 7), (0, 0)))
        o_f, o_b = _gdn_scan(qkvp, gates4, gcr)
        ona = _na(h_na.reshape(B, T, C_NA), _na_bias_table(rpb[l]),
                  jnp.tile(na_norm_g[l].astype(F32), LANES // NA_HEAD_DIM).reshape(1, LANES))
        xs = _mix_ffn(xs, o_f.reshape(bt, GDN_WIDTH), o_b.reshape(bt, GDN_WIDTH), z, ona.reshape(bt, NA_WIDTH),
                      jnp.tile(gdn_norm_g[l].astype(F32), GDN_HEADS).reshape(1, GDN_WIDTH),
                      w_out[l].astype(BF16), row(ln1_g[l]), row(ln1_b[l]),
                      w1[l].astype(BF16), row(b1[l]), w2[l].astype(BF16), row(b2[l]),
                      row(ln2_g[l]), row(ln2_b[l]))
    return xs.reshape(B, T, dm)
```

```python
import functools
import math

import jax
import jax.numpy as jnp
import numpy as np
from jax import lax
from jax.experimental import pallas as pl
from jax.experimental.pallas import tpu as pltpu

F32 = jnp.float32
BF16 = jnp.bfloat16

GRID_W = 64
GDN_HEAD_DIM = 128
GDN_HEADS = 4
GDN_WIDTH = GDN_HEADS * GDN_HEAD_DIM
NA_HEAD_DIM = 64
NA_HEADS = 8
NA_WIDTH = NA_HEADS * NA_HEAD_DIM
CONV_WIDTH = 5
GDN_CHUNK = 64
NA_WIN_R = 8
NA_WIN_C = 16
DEPTH = 2
DEEPNORM_ALPHA = (2 * DEPTH) ** 0.25
LN_EPS = 1e-5
RMS_EPS = 1e-6

LANES = 128
VMEM_LIMIT = 56 * 1024 * 1024

TM_PROJ = 512
TM_FFN = 512
TT_SCAN = 512
FF_CHUNK = 1024
NA_ROWS_PER_STEP = 8
FFN_SUBTILES = 2
GDN_BATCH_PER_STEP = 2
GDN_PREP_CHUNKS = 4

C_QKV = 3 * GDN_WIDTH
C_Z = GDN_WIDTH
C_NA = 3 * NA_WIDTH
OFF_Z = C_QKV
OFF_GF = OFF_Z + C_Z
OFF_GB = OFF_GF + LANES
OFF_NA = OFF_GB + LANES
C_TOTAL = OFF_NA + C_NA


def _cparams(sem):
    return pltpu.CompilerParams(dimension_semantics=sem, vmem_limit_bytes=VMEM_LIMIT)


def _const_spec(shape):
    nd = len(shape)
    return pl.BlockSpec(shape, lambda *_: (0,) * nd, pipeline_mode=pl.Buffered(1))


def _layer_norm(y, g, b):
    mu = jnp.mean(y, axis=-1, keepdims=True)
    yc = y - mu
    var = jnp.mean(yc * yc, axis=-1, keepdims=True)
    return yc * lax.rsqrt(var + LN_EPS) * g + b


def _sigmoid(x):
    return 1.0 / (1.0 + jnp.exp(-x))


def _chunk_cumsum(g, tpos, reverse):
    n = g.shape[0]
    s = 1
    while s < GDN_CHUNK:
        if reverse:
            g = g + jnp.where(tpos < GDN_CHUNK - s, pltpu.roll(g, n - s, 0), 0.0)
        else:
            g = g + jnp.where(tpos >= s, pltpu.roll(g, s, 0), 0.0)
        s *= 2
    return g


CONV_HALO = 8


def _inproj_kernel(x_ref, xp_ref, xn_ref, lng_ref, lnb_ref, w_ref, cw_ref, alog_ref, dtb_ref, *refs,
                   apply_ln, tiles_per_seq):
    if apply_ln:
        xo_ref, qkv_ref, z_ref, gates_ref, na_ref, hq_ref = refs
    else:
        qkv_ref, z_ref, gates_ref, na_ref, hq_ref = refs
    i = pl.program_id(0)
    tm = x_ref.shape[0]
    halo = CONV_HALO
    x, x_prev, x_next = x_ref[...], xp_ref[...], xn_ref[...]
    if apply_ln:
        x = _layer_norm(x, lng_ref[...], lnb_ref[...])
        x_prev = _layer_norm(x_prev, lng_ref[...], lnb_ref[...])
        x_next = _layer_norm(x_next, lng_ref[...], lnb_ref[...])
        xo_ref[...] = x
    xb = x.astype(BF16)
    x_ext = jnp.concatenate([x_prev, x, x_next], axis=0).astype(BF16)

    pos = i % tiles_per_seq
    keep_prev = jnp.where(pos == 0, 0.0, 1.0).astype(F32)
    keep_next = jnp.where(pos == tiles_per_seq - 1, 0.0, 1.0).astype(F32)
    hq = jnp.dot(x_ext, w_ref[:, 0:C_QKV], preferred_element_type=F32)
    hq_ref[0:halo, :] = hq[0:halo] * keep_prev
    hq_ref[halo:tm + halo, :] = hq[halo:tm + halo]
    hq_ref[tm + halo:tm + 2 * halo, :] = hq[tm + halo:tm + 2 * halo] * keep_next
    z_ref[...] = jnp.dot(xb, w_ref[:, OFF_Z:OFF_Z + C_Z], preferred_element_type=F32).astype(BF16)
    na_ref[...] = jnp.dot(xb, w_ref[:, OFF_NA:OFF_NA + C_NA], preferred_element_type=F32).astype(BF16)
    hab2 = jnp.dot(xb, w_ref[:, OFF_GF:OFF_GF + 2 * LANES], preferred_element_type=F32)

    first_tap = halo - CONV_WIDTH // 2
    for j in range(C_QKV // LANES):
        cols = slice(j * LANES, (j + 1) * LANES)
        y = hq_ref[first_tap:first_tap + tm, cols] * cw_ref[0:1, cols]
        for tap in range(1, CONV_WIDTH):
            y = y + hq_ref[first_tap + tap:first_tap + tap + tm, cols] * cw_ref[tap:tap + 1, cols]
        y = y * _sigmoid(y)
        if j < 2 * GDN_HEADS:
            fac = lax.rsqrt(jnp.sum(y * y, axis=-1, keepdims=True) + RMS_EPS)
            if j < GDN_HEADS:
                fac = fac * (GDN_HEAD_DIM ** -0.5)
            y = y * fac
        qkv_ref[:, cols] = y.astype(BF16)

    tpos = lax.broadcasted_iota(jnp.int32, (tm, LANES), 0) & (GDN_CHUNK - 1)
    lane = lax.broadcasted_iota(jnp.int32, (tm, LANES), 1)
    for d in range(2):
        hab = hab2[:, d * LANES:(d + 1) * LANES]
        sp_in = hab + dtb_ref[d:d + 1, :]
        softplus = jnp.maximum(sp_in, 0.0) + jnp.log1p(jnp.exp(-jnp.abs(sp_in)))
        g = -jnp.exp(alog_ref[d:d + 1, :]) * softplus
        g = jnp.where(lane < GDN_HEADS, g, 0.0)
        gc = _chunk_cumsum(g, tpos, reverse=(d == 1))
        gates_ref[d] = jnp.where(lane < GDN_HEADS, gc, _sigmoid(hab))


def _inproj(x2d, lng, lnb, w, conv_w8, alog_rows, dtb_rows, apply_ln, seq):
    bt, dm = x2d.shape
    tm = TM_PROJ
    grid = (bt // tm,)
    row = lambda i: (i, 0)
    hb = tm // CONV_HALO
    last_hb = bt // CONV_HALO - 1
    in_specs = [
        pl.BlockSpec((tm, dm), row),
        pl.BlockSpec((CONV_HALO, dm), lambda i: (jnp.maximum(i * hb - 1, 0), 0)),
        pl.BlockSpec((CONV_HALO, dm), lambda i: (jnp.minimum((i + 1) * hb, last_hb), 0)),
        _const_spec((1, dm)), _const_spec((1, dm)),
        _const_spec((dm, C_TOTAL)),
        _const_spec((8, C_QKV)),
        _const_spec((2, LANES)), _const_spec((2, LANES)),
    ]
    out_shape = [
        jax.ShapeDtypeStruct((bt, C_QKV), BF16),
        jax.ShapeDtypeStruct((bt, C_Z), BF16),
        jax.ShapeDtypeStruct((2, bt, LANES), F32),
        jax.ShapeDtypeStruct((bt, C_NA), BF16),
    ]
    out_specs = [
        pl.BlockSpec((tm, C_QKV), row),
        pl.BlockSpec((tm, C_Z), row),
        pl.BlockSpec((2, tm, LANES), lambda i: (0, i, 0)),
        pl.BlockSpec((tm, C_NA), row),
    ]
    if apply_ln:
        out_shape = [jax.ShapeDtypeStruct((bt, dm), F32)] + out_shape
        out_specs = [pl.BlockSpec((tm, dm), row)] + out_specs
    return pl.pallas_call(
        functools.partial(_inproj_kernel, apply_ln=apply_ln, tiles_per_seq=seq // tm),
        grid=grid, in_specs=in_specs, out_specs=out_specs, out_shape=out_shape,
        scratch_shapes=[pltpu.VMEM((tm + 2 * CONV_HALO, C_QKV), F32)],
        compiler_params=_cparams(("parallel",)),
        name="inproj_ln" if apply_ln else "inproj",
    )(x2d, x2d, x2d, lng, lnb, w, conv_w8, alog_rows, dtb_rows)


def _block_diag(x, nblk):
    w = x.shape[1] // nblk
    blk = lax.broadcasted_iota(jnp.int32, x.shape, 1) // w
    zero = jnp.zeros_like(x)
    return jnp.concatenate([jnp.where(blk == h, x, zero) for h in range(nblk)], axis=0)


def _lane_blocks(cols, width):
    r, n = cols.shape
    return jnp.concatenate([jnp.broadcast_to(cols[:, j:j + 1], (r, width)) for j in range(n)], axis=1)


def _unit_tri_inverses(mats, nblk):
    c = mats[0].shape[0]
    ri = lax.broadcasted_iota(jnp.int32, (c, nblk * c), 0)
    ci = lax.broadcasted_iota(jnp.int32, (c, nblk * c), 1) % c
    eye = (ri == ci).astype(F32)
    ps = [-a for a in mats]
    xs = [eye + p for p in ps]
    pbs = [p.astype(BF16) for p in ps]
    ps = [jnp.dot(pb, _block_diag(pb, nblk), preferred_element_type=F32) for pb in pbs]
    m = 2
    while 2 * m < c:
        pbs = [p.astype(BF16) for p in ps]
        prods = [jnp.dot(jnp.concatenate([x.astype(BF16), pb], axis=0), _block_diag(pb, nblk),
                         preferred_element_type=F32) for x, pb in zip(xs, pbs)]
        xs = [x + prod[:c] for x, prod in zip(xs, prods)]
        ps = [prod[c:] for prod in prods]
        m *= 2
    return [x + jnp.dot(x.astype(BF16), _block_diag(p.astype(BF16), nblk), preferred_element_type=F32)
            for x, p in zip(xs, ps)]


def _gdn_scan_kernel(qf_ref, kf_ref, vf_ref, gf_ref, grf_ref, qb_ref, kb_ref, vb_ref, gb_ref, grb_ref,
                     of_ref, ob_ref, s_ref, stk_ref, wq_ref, vbeta_ref, *, tt, nb):
    i = pl.program_id(1)
    nc = tt // GDN_CHUNK
    C = GDN_CHUNK
    dh = GDN_HEAD_DIM
    H = GDN_HEADS

    @pl.when(i == 0)
    def _():
        s_ref[...] = jnp.zeros_like(s_ref)

    ri = lax.broadcasted_iota(jnp.int32, (C, H * C), 0)
    ci = lax.broadcasted_iota(jnp.int32, (C, H * C), 1) % C
    mask_incl = (ri >= ci, ri <= ci)
    mask_strict = (ri > ci, ri < ci)
    dirs = ((qf_ref, kf_ref, vf_ref, gf_ref, grf_ref, of_ref), (qb_ref, kb_ref, vb_ref, gb_ref, grb_ref, ob_ref))
    streams = [(b, d) for b in range(nb) for d in range(2)]
    half_lo = lax.broadcasted_iota(jnp.int32, (C, dh), 1) < C
    nt_dims = (((1,), (1,)), ((), ()))
    tn_dims = (((0,), (0,)), ((), ()))

    def chunk_end_decay(gt, d):
        return gt[C - 1:C, :] if d == 0 else gt[0:1, :]

    def prep_group(cg, carry):
        items = [(cg * GDN_PREP_CHUNKS + j, u) for j in range(GDN_PREP_CHUNKS) for u in range(len(streams))]
        lhs_l, bdk_l, gam_l, kd_l = [], [], [], []
        for c, u in items:
            b, d = streams[u]
            q_ref, k_ref, v_ref, g_ref, gr_ref, _ = dirs[d]
            r0 = pl.multiple_of(c * C, C)
            gt = g_ref[0, b, pl.ds(r0, C), :]
            gr = gr_ref[0, b, c]
            q = q_ref[b, pl.ds(r0, C), :]
            k = k_ref[b, pl.ds(r0, C), :]
            v = v_ref[b, pl.ds(r0, C), :].astype(F32)
            kf = k.astype(F32)
            gc_n = _lane_blocks(gt[:, 0:H], dh)
            beta = _lane_blocks(gt[:, H:2 * H], dh)
            gend_n = _lane_blocks(chunk_end_decay(gt, d)[:, 0:H], dh)
            e_n = jnp.exp(gc_n)
            kb = kf * beta
            kbe = (kb * e_n).astype(BF16)
            qd = (q.astype(F32) * e_n).astype(BF16)
            kdec = (kf * jnp.exp(gend_n - gc_n)).astype(BF16)
            gc_c = jnp.concatenate(
                [jnp.where(half_lo, gc_n[:, (2 * p) * dh:(2 * p + 1) * dh], gc_n[:, (2 * p + 1) * dh:(2 * p + 2) * dh])
                 for p in range(H // 2)], axis=1)
            for h in range(H):
                cols = slice(h * dh, (h + 1) * dh)
                stk_ref[c, u * H + h] = jnp.concatenate([kbe[:, cols], qd[:, cols]], axis=0)
            vbeta_ref[c, u] = v * beta
            lhs_l.append(jnp.concatenate([kb.astype(BF16), q], axis=0))
            bdk_l.append(_block_diag(k, H))
            kd_l.append(jnp.concatenate([kdec[:, h * dh:(h + 1) * dh] for h in range(H)], axis=0))
            gam_l.append(jnp.where(mask_incl[d], jnp.exp(gc_c - gr[0:1, :]), 0.0))
        kq_l = [lax.dot_general(lhs, bdk, nt_dims, preferred_element_type=F32)
                for lhs, bdk in zip(lhs_l, bdk_l)]
        a_l = [jnp.where(mask_strict[streams[u][1]], kq[:C] * gam, 0.0)
               for kq, gam, (_, u) in zip(kq_l, gam_l, items)]
        qk_l = [(kq[C:] * gam).astype(BF16) for kq, gam in zip(kq_l, gam_l)]
        bdt_l = [_block_diag(t.astype(BF16), H) for t in _unit_tri_inverses(a_l, H)]
        wc_l = [lax.dot_general(kd, bdt, tn_dims, preferred_element_type=F32) for kd, bdt in zip(kd_l, bdt_l)]
        qt_l = [jnp.dot(qk, bdt, preferred_element_type=F32) for qk, bdt in zip(qk_l, bdt_l)]
        for (c, u), wc, qt in zip(items, wc_l, qt_l):
            wq_ref[c, u] = jnp.concatenate([wc, qt], axis=0).astype(BF16)
        return carry

    lax.fori_loop(0, nc // GDN_PREP_CHUNKS, prep_group, 0)

    def scan_step(step, carry):
        cs = (step, nc - 1 - step)
        egs, x1_l = [], []
        for u, (b, d) in enumerate(streams):
            r0 = pl.multiple_of(cs[d] * C, C)
            egs.append(jnp.exp(chunk_end_decay(dirs[d][3][0, b, pl.ds(r0, C), :], d)))
        s_l = [[s_ref[u * H + h] for h in range(H)] for u in range(len(streams))]
        for u, (b, d) in enumerate(streams):
            x1_l.append([jnp.dot(stk_ref[cs[d], u * H + h], s_l[u][h].astype(BF16), preferred_element_type=F32)
                         for h in range(H)])
        res_l = [(vbeta_ref[cs[d], u] - jnp.concatenate([x1[:C] for x1 in x1_l[u]], axis=1)).astype(BF16)
                 for u, (b, d) in enumerate(streams)]
        z_l = [jnp.dot(wq_ref[cs[d], u], _block_diag(res, H), preferred_element_type=F32)
               for (u, (b, d)), res in zip(enumerate(streams), res_l)]
        for u, (b, d) in enumerate(streams):
            z = z_l[u]
            for h in range(H):
                s_ref[u * H + h] = s_l[u][h] * egs[u][:, h:h + 1] + z[:dh, h * dh:(h + 1) * dh]
            r0 = pl.multiple_of(cs[d] * C, C)
            dirs[d][5][b, pl.ds(r0, C), :] = jnp.concatenate([x1[C:] for x1 in x1_l[u]], axis=1) + z[dh:]
        return carry

    lax.fori_loop(0, nc, scan_step, 0)


def _gdn_scan(qkvp, gates, gcr):
    b, seq, _ = qkvp.shape
    tt = TT_SCAN
    nb = GDN_BATCH_PER_STEP if b % GDN_BATCH_PER_STEP == 0 else 1
    nt = seq // tt
    nc = tt // GDN_CHUNK
    C = GDN_CHUNK
    dh = GDN_HEAD_DIM
    H = GDN_HEADS
    nstream = 2 * nb

    def dir_specs(d):
        tmap = (lambda i: i) if d == 0 else (lambda i: nt - 1 - i)
        return [
            pl.BlockSpec((nb, tt, GDN_WIDTH), lambda bi, i: (bi, tmap(i), 0)),
            pl.BlockSpec((nb, tt, GDN_WIDTH), lambda bi, i: (bi, tmap(i), 1)),
            pl.BlockSpec((nb, tt, GDN_WIDTH), lambda bi, i: (bi, tmap(i), 2)),
            pl.BlockSpec((1, nb, tt, LANES), lambda bi, i: (d, bi, tmap(i), 0)),
            pl.BlockSpec((1, nb, nc, 8, H * C), lambda bi, i: (d, bi, tmap(i), 0, 0)),
        ]

    out_sds = jax.ShapeDtypeStruct((b, seq, GDN_WIDTH), F32)
    return pl.pallas_call(
        functools.partial(_gdn_scan_kernel, tt=tt, nb=nb),
        grid=(b // nb, nt),
        in_specs=dir_specs(0) + dir_specs(1),
        out_specs=[pl.BlockSpec((nb, tt, GDN_WIDTH), lambda bi, i: (bi, i, 0)),
                   pl.BlockSpec((nb, tt, GDN_WIDTH), lambda bi, i: (bi, nt - 1 - i, 0))],
        out_shape=[out_sds, out_sds],
        scratch_shapes=[
            pltpu.VMEM((nstream * H, dh, dh), F32),
            pltpu.VMEM((nc, nstream * H, 2 * C, dh), BF16),
            pltpu.VMEM((nc, nstream, dh + C, H * C), BF16),
            pltpu.VMEM((nc, nstream, C, H * dh), F32),
        ],
        compiler_params=_cparams(("parallel", "arbitrary")),
        name="gdn_scan",
    )(qkvp, qkvp, qkvp, gates, gcr, qkvp, qkvp, qkvp, gates, gcr)


def _na_kernel(q_ref, k_ref, v_ref, bm_ref, g_ref, o_ref, *, rows):
    W = GRID_W
    band = NA_WIN_R * W
    hd = NA_HEAD_DIM
    lane_q = lax.broadcasted_iota(jnp.int32, (W, LANES), 1)
    first = lane_q < hd
    scale = jnp.asarray(hd ** -0.5, BF16)

    nt_dims = (((1,), (1,)), ((), ()))

    def row_group(gi, carry):
        rs = [gi * NA_ROWS_PER_STEP + j for j in range(NA_ROWS_PER_STEP)]
        r0s = [jnp.clip(r - NA_WIN_R // 2, 0, rows - NA_WIN_R) for r in rs]
        qs_l, kb_l, vb_l = [], [], []
        for r, r0 in zip(rs, r0s):
            q2 = q_ref[0, pl.ds(pl.multiple_of(r * W, W), W), :] * scale
            zero = jnp.zeros_like(q2)
            qs_l.append(jnp.concatenate([jnp.where(first, q2, zero), jnp.where(first, zero, q2)], axis=0))
            kb_l.append(k_ref[0, pl.ds(pl.multiple_of(r0 * W, W), band), :])
            vb_l.append(v_ref[0, pl.ds(pl.multiple_of(r0 * W, W), band), :])
        s_l = [lax.dot_general(qs, kb, nt_dims, preferred_element_type=F32) for qs, kb in zip(qs_l, kb_l)]
        s_l = [s + bm_ref[0, r - r0] for s, r, r0 in zip(s_l, rs, r0s)]
        m_l = [jnp.max(s, axis=-1, keepdims=True) for s in s_l]
        p_l = [jnp.exp(s - m) for s, m in zip(s_l, m_l)]
        l_l = [jnp.sum(p, axis=-1, keepdims=True) for p in p_l]
        pb_l = [p.astype(BF16) for p in p_l]
        o0_l = [jnp.dot(pb[:W], vb, preferred_element_type=F32) for pb, vb in zip(pb_l, vb_l)]
        o1_l = [jnp.dot(pb[W:], vb, preferred_element_type=F32) for pb, vb in zip(pb_l, vb_l)]
        for r, o0, o1, l in zip(rs, o0_l, o1_l, l_l):
            o = jnp.where(first, o0 / l[:W], o1 / l[W:])
            sq = o * o
            ms0 = jnp.sum(jnp.where(first, sq, 0.0), axis=-1, keepdims=True)
            ms1 = jnp.sum(jnp.where(first, 0.0, sq), axis=-1, keepdims=True)
            ms = jnp.where(first, ms0, ms1) * (1.0 / hd)
            o_ref[0, pl.ds(pl.multiple_of(r * W, W), W), :] = (o * lax.rsqrt(ms + RMS_EPS) * g_ref[...]).astype(BF16)
        return carry

    lax.fori_loop(0, rows // NA_ROWS_PER_STEP, row_group, 0)


def _na(h_na, bias_tab, g_row):
    b, seq, _ = h_na.shape
    rows = seq // GRID_W
    assert rows >= NA_WIN_R
    npair = NA_WIDTH // LANES
    band = NA_WIN_R * GRID_W
    return pl.pallas_call(
        functools.partial(_na_kernel, rows=rows),
        grid=(b, npair),
        in_specs=[
            pl.BlockSpec((1, seq, LANES), lambda bi, p: (bi, 0, p)),
            pl.BlockSpec((1, seq, LANES), lambda bi, p: (bi, 0, npair + p)),
            pl.BlockSpec((1, seq, LANES), lambda bi, p: (bi, 0, 2 * npair + p)),
            pl.BlockSpec((1, NA_WIN_R, 2 * GRID_W, band), lambda bi, p: (p, 0, 0, 0)),
            pl.BlockSpec((1, LANES), lambda bi, p: (0, 0)),
        ],
        out_specs=pl.BlockSpec((1, seq, LANES), lambda bi, p: (bi, 0, p)),
        out_shape=jax.ShapeDtypeStruct((b, seq, NA_WIDTH), BF16),
        compiler_params=_cparams(("parallel", "parallel")),
        name="natten",
    )(h_na, h_na, h_na, bias_tab, g_row)


def _na_bias_table(rpb_l):
    W = GRID_W
    vi = np.arange(NA_WIN_R)
    kr = np.arange(NA_WIN_R)
    dr = kr[None, :] - vi[:, None] + NA_WIN_R - 1
    qc = np.arange(W)
    kc = np.arange(W)
    win_start = np.clip(qc - NA_WIN_C // 2, 0, W - NA_WIN_C)
    in_win = (kc[None, :] >= win_start[:, None]) & (kc[None, :] < win_start[:, None] + NA_WIN_C)
    dc = kc[None, :] - qc[:, None] + NA_WIN_C - 1
    rsel = (dr[:, :, None] == np.arange(2 * NA_WIN_R - 1)).astype(np.float32)
    csel = ((dc[:, :, None] == np.arange(2 * NA_WIN_C - 1)) & in_win[:, :, None]).astype(np.float32)
    tab = jnp.einsum("hab,vka,qcb->hvqkc", rpb_l.astype(F32), rsel, csel,
                     precision=lax.Precision.HIGHEST)
    tab = jnp.where(jnp.asarray(in_win)[None, None, :, None, :], tab, -jnp.inf)
    tab = tab.reshape(NA_HEADS // 2, 2, NA_WIN_R, W, NA_WIN_R * W)
    tab = jnp.transpose(tab, (0, 2, 1, 3, 4))
    return tab.reshape(NA_HEADS // 2, NA_WIN_R, 2 * W, NA_WIN_R * W)


def _mix_ffn_kernel(x_ref, of_ref, ob_ref, z_ref, ona_ref, gg_ref, wo_ref, l1g_ref, l1b_ref,
                    w1_ref, b1_ref, w2_ref, b2_ref, l2g_ref, l2b_ref, out_ref):
    dh = GDN_HEAD_DIM
    tm = x_ref.shape[0]
    d_ff = w1_ref.shape[1]
    rows = [slice(s * (tm // FFN_SUBTILES), (s + 1) * (tm // FFN_SUBTILES)) for s in range(FFN_SUBTILES)]

    def gated(rs):
        o = of_ref[rs, :] + ob_ref[rs, :]
        z = z_ref[rs, :].astype(F32)
        gate = z * _sigmoid(z)
        parts = []
        for h in range(GDN_HEADS):
            cols = slice(h * dh, (h + 1) * dh)
            oh = o[:, cols]
            ms = jnp.mean(oh * oh, axis=-1, keepdims=True)
            parts.append((oh * lax.rsqrt(ms + RMS_EPS) * gg_ref[:, cols] * gate[:, cols]).astype(BF16))
        return jnp.concatenate(parts, axis=-1)

    og_l = [gated(rs) for rs in rows]
    mix_l = [jnp.dot(og, wo_ref[0:GDN_WIDTH, :], preferred_element_type=F32)
             + jnp.dot(ona_ref[rs, :], wo_ref[GDN_WIDTH:, :], preferred_element_type=F32)
             for og, rs in zip(og_l, rows)]
    x1_l = [_layer_norm(DEEPNORM_ALPHA * x_ref[rs, :] + mix, l1g_ref[...], l1b_ref[...])
            for mix, rs in zip(mix_l, rows)]
    x1b_l = [x1.astype(BF16) for x1 in x1_l]
    acc_l = [jnp.zeros(x1.shape, F32) for x1 in x1_l]
    for f in range(d_ff // FF_CHUNK):
        fs = slice(f * FF_CHUNK, (f + 1) * FF_CHUNK)
        hf_l = [jnp.dot(x1b, w1_ref[:, fs], preferred_element_type=F32) + b1_ref[:, fs] for x1b in x1b_l]
        hf_l = [jnp.square(jnp.maximum(hf, 0.0)).astype(BF16) for hf in hf_l]
        acc_l = [acc + jnp.dot(hf, w2_ref[fs, :], preferred_element_type=F32) for acc, hf in zip(acc_l, hf_l)]
    for rs, x1, acc in zip(rows, x1_l, acc_l):
        y = DEEPNORM_ALPHA * x1 + (acc + b2_ref[...])
        out_ref[rs, :] = _layer_norm(y, l2g_ref[...], l2b_ref[...])


def _mix_ffn(x2d, o_f, o_b, z, ona, gg, wo, l1g, l1b, w1, b1, w2, b2, l2g, l2b):
    bt, dm = x2d.shape
    d_ff = w1.shape[1]
    tm = TM_FFN
    row = lambda i: (i, 0)
    return pl.pallas_call(
        _mix_ffn_kernel,
        grid=(bt // tm,),
        in_specs=[
            pl.BlockSpec((tm, dm), row),
            pl.BlockSpec((tm, GDN_WIDTH), row),
            pl.BlockSpec((tm, GDN_WIDTH), row),
            pl.BlockSpec((tm, GDN_WIDTH), row),
            pl.BlockSpec((tm, NA_WIDTH), row),
            _const_spec((1, GDN_WIDTH)),
            _const_spec((dm, dm)),
            _const_spec((1, dm)), _const_spec((1, dm)),
            _const_spec((dm, d_ff)), _const_spec((1, d_ff)),
            _const_spec((d_ff, dm)), _const_spec((1, dm)),
            _const_spec((1, dm)), _const_spec((1, dm)),
        ],
        out_specs=pl.BlockSpec((tm, dm), row),
        out_shape=jax.ShapeDtypeStruct((bt, dm), F32),
        compiler_params=_cparams(("parallel",)),
        name="mix_ffn",
    )(x2d, o_f, o_b, z, ona, gg, wo, l1g, l1b, w1, b1, w2, b2, l2g, l2b)


def _pack_w_in(w_l):
    dm = w_l.shape[0]
    h = GDN_HEADS
    a0 = 4 * GDN_WIDTH
    b0 = a0 + 2 * h
    na0 = b0 + 2 * h
    pad = jnp.zeros((dm, LANES - 2 * h), w_l.dtype)
    gate_f = jnp.concatenate([w_l[:, a0:a0 + h], w_l[:, b0:b0 + h], pad], axis=1)
    gate_b = jnp.concatenate([w_l[:, a0 + h:a0 + 2 * h], w_l[:, b0 + h:b0 + 2 * h], pad], axis=1)
    w = jnp.concatenate([w_l[:, :a0], gate_f, gate_b, w_l[:, na0:]], axis=1)
    return w.astype(BF16)


def _gate_rows(p):
    return jnp.pad(p.astype(F32), ((0, 0), (0, LANES - p.shape[1])))


def kernel(x, ln_in_g, ln_in_b, w_in, conv_w, a_log, dt_bias, gdn_norm_g, rpb, na_norm_g, w_out,
           ln1_g, ln1_b, w1, b1, w2, b2, ln2_g, ln2_b):
    B, T, dm = x.shape
    bt = B * T
    nchunks = T // GDN_CHUNK
    row = lambda v: v.reshape(1, -1).astype(F32)
    xs = x.reshape(bt, dm)
    for l in range(DEPTH):
        w_l = _pack_w_in(w_in[l])
        conv8 = jnp.pad(conv_w[l].astype(F32), ((0, 8 - CONV_WIDTH), (0, 0)))
        outs = _inproj(xs, row(ln_in_g), row(ln_in_b), w_l, conv8, _gate_rows(a_log[l]), _gate_rows(dt_bias[l]),
                       apply_ln=(l == 0), seq=T)
        if l == 0:
            xs, qkvp, z, gates, h_na = outs
        else:
            qkvp, z, gates, h_na = outs
        qkvp = qkvp.reshape(B, T, C_QKV)
        gates4 = gates.reshape(2, B, T, LANES)
        gcr = jnp.swapaxes(gates4[..., :GDN_HEADS].reshape(2, B, nchunks, GDN_CHUNK, GDN_HEADS), -1, -2)
        gcr = jnp.pad(gcr.reshape(2, B, nchunks, 1, GDN_HEADS * GDN_CHUNK), ((0, 0),) * 3 + ((0, 7), (0, 0)))
        o_f, o_b = _gdn_scan(qkvp, gates4, gcr)
        ona = _na(h_na.reshape(B, T, C_NA), _na_bias_table(rpb[l]),
                  jnp.tile(na_norm_g[l].astype(F32), LANES // NA_HEAD_DIM).reshape(1, LANES))
        xs = _mix_ffn(xs, o_f.reshape(bt, GDN_WIDTH), o_b.reshape(bt, GDN_WIDTH), z, ona.reshape(bt, NA_WIDTH),
                      jnp.tile(gdn_norm_g[l].astype(F32), GDN_HEADS).reshape(1, GDN_WIDTH),
                      w_out[l].astype(BF16), row(ln1_g[l]), row(ln1_b[l]),
                      w1[l].astype(BF16), row(b1[l]), w2[l].astype(BF16), row(b2[l]),
                      row(ln2_g[l]), row(ln2_b[l]))
    return xs.reshape(B, T, dm)
```

```python
import functools
import math

import jax
import jax.numpy as jnp
import numpy as np
from jax import lax
from jax.experimental import pallas as pl
from jax.experimental.pallas import tpu as pltpu

F32 = jnp.float32
BF16 = jnp.bfloat16

GRID_W = 64
GDN_HEAD_DIM = 128
GDN_HEADS = 4
GDN_WIDTH = GDN_HEADS * GDN_HEAD_DIM
NA_HEAD_DIM = 64
NA_HEADS = 8
NA_WIDTH = NA_HEADS * NA_HEAD_DIM
CONV_WIDTH = 5
GDN_CHUNK = 64
NA_WIN_R = 8
NA_WIN_C = 16
DEPTH = 2
DEEPNORM_ALPHA = (2 * DEPTH) ** 0.25
LN_EPS = 1e-5
RMS_EPS = 1e-6

LANES = 128
VMEM_LIMIT = 56 * 1024 * 1024

TM_PROJ = 512
TM_FFN = 512
TT_SCAN = 512
FF_CHUNK = 1024
NA_ROWS_PER_STEP = 16
FFN_SUBTILES = 2
GDN_BATCH_PER_STEP = 2
GDN_PREP_CHUNKS = 4

C_QKV = 3 * GDN_WIDTH
C_Z = GDN_WIDTH
C_NA = 3 * NA_WIDTH


def _cparams(sem):
    return pltpu.CompilerParams(dimension_semantics=sem, vmem_limit_bytes=VMEM_LIMIT)


def _const_spec(shape):
    nd = len(shape)
    return pl.BlockSpec(shape, lambda *_: (0,) * nd, pipeline_mode=pl.Buffered(1))


def _layer_norm(y, g, b):
    mu = jnp.mean(y, axis=-1, keepdims=True)
    yc = y - mu
    var = jnp.mean(yc * yc, axis=-1, keepdims=True)
    return yc * lax.rsqrt(var + LN_EPS) * g + b


def _sigmoid(x):
    return 1.0 / (1.0 + jnp.exp(-x))


def _chunk_cumsum(g, tpos, reverse):
    n = g.shape[0]
    s = 1
    while s < GDN_CHUNK:
        if reverse:
            g = g + jnp.where(tpos < GDN_CHUNK - s, pltpu.roll(g, n - s, 0), 0.0)
        else:
            g = g + jnp.where(tpos >= s, pltpu.roll(g, s, 0), 0.0)
        s *= 2
    return g


CONV_HALO = 8


def _inproj_kernel(x_ref, xp_ref, xn_ref, lng_ref, lnb_ref, wq_ref, wz_ref, wg_ref, wna_ref, cw_ref, alog_ref, dtb_ref,
                   *refs, apply_ln, tiles_per_seq):
    if apply_ln:
        xo_ref, qkv_ref, z_ref, gates_ref, gcr_ref, na_ref, hq_ref = refs
    else:
        qkv_ref, z_ref, gates_ref, gcr_ref, na_ref, hq_ref = refs
    i = pl.program_id(0)
    tm = x_ref.shape[0]
    halo = CONV_HALO
    x, x_prev, x_next = x_ref[...], xp_ref[...], xn_ref[...]
    if apply_ln:
        x = _layer_norm(x, lng_ref[...], lnb_ref[...])
        x_prev = _layer_norm(x_prev, lng_ref[...], lnb_ref[...])
        x_next = _layer_norm(x_next, lng_ref[...], lnb_ref[...])
        xo_ref[...] = x
    xb = x.astype(BF16)
    x_ext = jnp.concatenate([x_prev, x, x_next], axis=0).astype(BF16)

    pos = i % tiles_per_seq
    keep_prev = jnp.where(pos == 0, 0.0, 1.0).astype(F32)
    keep_next = jnp.where(pos == tiles_per_seq - 1, 0.0, 1.0).astype(F32)
    hq = jnp.dot(x_ext, wq_ref[...], preferred_element_type=F32)
    hq_ref[0:halo, :] = hq[0:halo] * keep_prev
    hq_ref[halo:tm + halo, :] = hq[halo:tm + halo]
    hq_ref[tm + halo:tm + 2 * halo, :] = hq[tm + halo:tm + 2 * halo] * keep_next
    z_ref[...] = jnp.dot(xb, wz_ref[...], preferred_element_type=F32).astype(BF16)
    na_ref[...] = jnp.dot(xb, wna_ref[...], preferred_element_type=F32).astype(BF16)
    hab2 = jnp.dot(xb, wg_ref[...], preferred_element_type=F32)

    first_tap = halo - CONV_WIDTH // 2
    for j in range(C_QKV // LANES):
        cols = slice(j * LANES, (j + 1) * LANES)
        y = hq_ref[first_tap:first_tap + tm, cols] * cw_ref[0:1, cols]
        for tap in range(1, CONV_WIDTH):
            y = y + hq_ref[first_tap + tap:first_tap + tap + tm, cols] * cw_ref[tap:tap + 1, cols]
        y = y * _sigmoid(y)
        if j < 2 * GDN_HEADS:
            fac = lax.rsqrt(jnp.sum(y * y, axis=-1, keepdims=True) + RMS_EPS)
            if j < GDN_HEADS:
                fac = fac * (GDN_HEAD_DIM ** -0.5)
            y = y * fac
        qkv_ref[:, cols] = y.astype(BF16)

    tpos = lax.broadcasted_iota(jnp.int32, (tm, LANES), 0) & (GDN_CHUNK - 1)
    lane = lax.broadcasted_iota(jnp.int32, (tm, LANES), 1)
    for d in range(2):
        hab = hab2[:, d * LANES:(d + 1) * LANES]
        sp_in = hab + dtb_ref[d:d + 1, :]
        softplus = jnp.maximum(sp_in, 0.0) + jnp.log1p(jnp.exp(-jnp.abs(sp_in)))
        g = -jnp.exp(alog_ref[d:d + 1, :]) * softplus
        g = jnp.where(lane < GDN_HEADS, g, 0.0)
        gc = _chunk_cumsum(g, tpos, reverse=(d == 1))
        gates_ref[d] = jnp.where(lane < GDN_HEADS, gc, _sigmoid(hab))
        gct = gc.T
        pad_rows = jnp.zeros((7, GDN_HEADS * GDN_CHUNK), F32)
        for c in range(tm // GDN_CHUNK):
            row_c = jnp.concatenate([gct[h:h + 1, c * GDN_CHUNK:(c + 1) * GDN_CHUNK] for h in range(GDN_HEADS)], axis=1)
            gcr_ref[d, c] = jnp.concatenate([row_c, pad_rows], axis=0)


def _inproj(x2d, lng, lnb, ws, conv_w8, alog_rows, dtb_rows, apply_ln, seq):
    bt, dm = x2d.shape
    tm = TM_PROJ
    grid = (bt // tm,)
    row = lambda i: (i, 0)
    hb = tm // CONV_HALO
    last_hb = bt // CONV_HALO - 1
    in_specs = [
        pl.BlockSpec((tm, dm), row),
        pl.BlockSpec((CONV_HALO, dm), lambda i: (jnp.maximum(i * hb - 1, 0), 0)),
        pl.BlockSpec((CONV_HALO, dm), lambda i: (jnp.minimum((i + 1) * hb, last_hb), 0)),
        _const_spec((1, dm)), _const_spec((1, dm)),
        _const_spec((dm, C_QKV)), _const_spec((dm, C_Z)), _const_spec((dm, 2 * LANES)), _const_spec((dm, C_NA)),
        _const_spec((8, C_QKV)),
        _const_spec((2, LANES)), _const_spec((2, LANES)),
    ]
    out_shape = [
        jax.ShapeDtypeStruct((bt, C_QKV), BF16),
        jax.ShapeDtypeStruct((bt, C_Z), BF16),
        jax.ShapeDtypeStruct((2, bt, LANES), F32),
        jax.ShapeDtypeStruct((2, bt // GDN_CHUNK, 8, GDN_HEADS * GDN_CHUNK), F32),
        jax.ShapeDtypeStruct((bt, C_NA), BF16),
    ]
    out_specs = [
        pl.BlockSpec((tm, C_QKV), row),
        pl.BlockSpec((tm, C_Z), row),
        pl.BlockSpec((2, tm, LANES), lambda i: (0, i, 0)),
        pl.BlockSpec((2, tm // GDN_CHUNK, 8, GDN_HEADS * GDN_CHUNK), lambda i: (0, i, 0, 0)),
        pl.BlockSpec((tm, C_NA), row),
    ]
    if apply_ln:
        out_shape = [jax.ShapeDtypeStruct((bt, dm), F32)] + out_shape
        out_specs = [pl.BlockSpec((tm, dm), row)] + out_specs
    return pl.pallas_call(
        functools.partial(_inproj_kernel, apply_ln=apply_ln, tiles_per_seq=seq // tm),
        grid=grid, in_specs=in_specs, out_specs=out_specs, out_shape=out_shape,
        scratch_shapes=[pltpu.VMEM((tm + 2 * CONV_HALO, C_QKV), F32)],
        compiler_params=_cparams(("parallel",)),
        name="inproj_ln" if apply_ln else "inproj",
    )(x2d, x2d, x2d, lng, lnb, *ws, conv_w8, alog_rows, dtb_rows)


def _block_diag(x, nblk):
    w = x.shape[1] // nblk
    blk = lax.broadcasted_iota(jnp.int32, x.shape, 1) // w
    zero = jnp.zeros_like(x)
    return jnp.concatenate([jnp.where(blk == h, x, zero) for h in range(nblk)], axis=0)


def _lane_blocks(cols, width):
    r, n = cols.shape
    return jnp.concatenate([jnp.broadcast_to(cols[:, j:j + 1], (r, width)) for j in range(n)], axis=1)


def _unit_tri_inverses(mats, nblk):
    c = mats[0].shape[0]
    ri = lax.broadcasted_iota(jnp.int32, (c, nblk * c), 0)
    ci = lax.broadcasted_iota(jnp.int32, (c, nblk * c), 1) % c
    eye = (ri == ci).astype(F32)
    ps = [-a for a in mats]
    xs = [eye + p for p in ps]
    pbs = [p.astype(BF16) for p in ps]
    ps = [jnp.dot(pb, _block_diag(pb, nblk), preferred_element_type=F32) for pb in pbs]
    m = 2
    while 2 * m < c:
        pbs = [p.astype(BF16) for p in ps]
        prods = [jnp.dot(jnp.concatenate([x.astype(BF16), pb], axis=0), _block_diag(pb, nblk),
                         preferred_element_type=F32) for x, pb in zip(xs, pbs)]
        xs = [x + prod[:c] for x, prod in zip(xs, prods)]
        ps = [prod[c:] for prod in prods]
        m *= 2
    return [x + jnp.dot(x.astype(BF16), _block_diag(p.astype(BF16), nblk), preferred_element_type=F32)
            for x, p in zip(xs, ps)]


def _gdn_scan_kernel(qf_ref, kf_ref, vf_ref, gf_ref, grf_ref, qb_ref, kb_ref, vb_ref, gb_ref, grb_ref,
                     of_ref, ob_ref, s_ref, stk_ref, wq_ref, vbeta_ref, *, tt, nb):
    i = pl.program_id(1)
    nc = tt // GDN_CHUNK
    C = GDN_CHUNK
    dh = GDN_HEAD_DIM
    H = GDN_HEADS

    @pl.when(i == 0)
    def _():
        s_ref[...] = jnp.zeros_like(s_ref)

    ri = lax.broadcasted_iota(jnp.int32, (C, H * C), 0)
    ci = lax.broadcasted_iota(jnp.int32, (C, H * C), 1) % C
    mask_incl = (ri >= ci, ri <= ci)
    mask_strict = (ri > ci, ri < ci)
    dirs = ((qf_ref, kf_ref, vf_ref, gf_ref, grf_ref, of_ref), (qb_ref, kb_ref, vb_ref, gb_ref, grb_ref, ob_ref))
    streams = [(b, d) for b in range(nb) for d in range(2)]
    half_lo = lax.broadcasted_iota(jnp.int32, (C, dh), 1) < C
    nt_dims = (((1,), (1,)), ((), ()))
    tn_dims = (((0,), (0,)), ((), ()))

    def chunk_end_decay(gt, d):
        return gt[C - 1:C, :] if d == 0 else gt[0:1, :]

    def prep_group(cg, carry):
        items = [(cg * GDN_PREP_CHUNKS + j, u) for j in range(GDN_PREP_CHUNKS) for u in range(len(streams))]
        lhs_l, bdk_l, gam_l, kd_l = [], [], [], []
        for c, u in items:
            b, d = streams[u]
            q_ref, k_ref, v_ref, g_ref, gr_ref, _ = dirs[d]
            r0 = pl.multiple_of(c * C, C)
            gt = g_ref[0, b, pl.ds(r0, C), :]
            gr = gr_ref[0, b, c]
            q = q_ref[b, pl.ds(r0, C), :]
            k = k_ref[b, pl.ds(r0, C), :]
            v = v_ref[b, pl.ds(r0, C), :].astype(F32)
            kf = k.astype(F32)
            gc_n = _lane_blocks(gt[:, 0:H], dh)
            beta = _lane_blocks(gt[:, H:2 * H], dh)
            gend_n = _lane_blocks(chunk_end_decay(gt, d)[:, 0:H], dh)
            e_n = jnp.exp(gc_n)
            kb = kf * beta
            kbe = (kb * e_n).astype(BF16)
            qd = (q.astype(F32) * e_n).astype(BF16)
            kdec = (kf * jnp.exp(gend_n - gc_n)).astype(BF16)
            gc_c = jnp.concatenate(
                [jnp.where(half_lo, gc_n[:, (2 * p) * dh:(2 * p + 1) * dh], gc_n[:, (2 * p + 1) * dh:(2 * p + 2) * dh])
                 for p in range(H // 2)], axis=1)
            for h in range(H):
                cols = slice(h * dh, (h + 1) * dh)
                stk_ref[c, u * H + h] = jnp.concatenate([kbe[:, cols], qd[:, cols]], axis=0)
            vbeta_ref[c, u] = v * beta
            lhs_l.append(jnp.concatenate([kb.astype(BF16), q], axis=0))
            bdk_l.append(_block_diag(k, H))
            kd_l.append(jnp.concatenate([kdec[:, h * dh:(h + 1) * dh] for h in range(H)], axis=0))
            gam_l.append(jnp.where(mask_incl[d], jnp.exp(gc_c - gr[0:1, :]), 0.0))
        kq_l = [lax.dot_general(lhs, bdk, nt_dims, preferred_element_type=F32)
                for lhs, bdk in zip(lhs_l, bdk_l)]
        a_l = [jnp.where(mask_strict[streams[u][1]], kq[:C] * gam, 0.0)
               for kq, gam, (_, u) in zip(kq_l, gam_l, items)]
        qk_l = [(kq[C:] * gam).astype(BF16) for kq, gam in zip(kq_l, gam_l)]
        bdt_l = [_block_diag(t.astype(BF16), H) for t in _unit_tri_inverses(a_l, H)]
        wc_l = [lax.dot_general(kd, bdt, tn_dims, preferred_element_type=F32) for kd, bdt in zip(kd_l, bdt_l)]
        qt_l = [jnp.dot(qk, bdt, preferred_element_type=F32) for qk, bdt in zip(qk_l, bdt_l)]
        for (c, u), wc, qt in zip(items, wc_l, qt_l):
            wq_ref[c, u] = jnp.concatenate([wc, qt], axis=0).astype(BF16)
        return carry

    lax.fori_loop(0, nc // GDN_PREP_CHUNKS, prep_group, 0)

    def scan_step(step, carry):
        cs = (step, nc - 1 - step)
        egs, x1_l = [], []
        for u, (b, d) in enumerate(streams):
            r0 = pl.multiple_of(cs[d] * C, C)
            egs.append(jnp.exp(chunk_end_decay(dirs[d][3][0, b, pl.ds(r0, C), :], d)))
        s_l = [[s_ref[u * H + h] for h in range(H)] for u in range(len(streams))]
        for u, (b, d) in enumerate(streams):
            x1_l.append([jnp.dot(stk_ref[cs[d], u * H + h], s_l[u][h].astype(BF16), preferred_element_type=F32)
                         for h in range(H)])
        res_l = [(vbeta_ref[cs[d], u] - jnp.concatenate([x1[:C] for x1 in x1_l[u]], axis=1)).astype(BF16)
                 for u, (b, d) in enumerate(streams)]
        z_l = [jnp.dot(wq_ref[cs[d], u], _block_diag(res, H), preferred_element_type=F32)
               for (u, (b, d)), res in zip(enumerate(streams), res_l)]
        for u, (b, d) in enumerate(streams):
            z = z_l[u]
            for h in range(H):
                s_ref[u * H + h] = s_l[u][h] * egs[u][:, h:h + 1] + z[:dh, h * dh:(h + 1) * dh]
            r0 = pl.multiple_of(cs[d] * C, C)
            dirs[d][5][b, pl.ds(r0, C), :] = jnp.concatenate([x1[C:] for x1 in x1_l[u]], axis=1) + z[dh:]
        return carry

    lax.fori_loop(0, nc, scan_step, 0)


def _gdn_scan(qkvp, gates, gcr):
    b, seq, _ = qkvp.shape
    tt = TT_SCAN
    nb = GDN_BATCH_PER_STEP if b % GDN_BATCH_PER_STEP == 0 else 1
    nt = seq // tt
    nc = tt // GDN_CHUNK
    C = GDN_CHUNK
    dh = GDN_HEAD_DIM
    H = GDN_HEADS
    nstream = 2 * nb

    def dir_specs(d):
        tmap = (lambda i: i) if d == 0 else (lambda i: nt - 1 - i)
        return [
            pl.BlockSpec((nb, tt, GDN_WIDTH), lambda bi, i: (bi, tmap(i), 0)),
            pl.BlockSpec((nb, tt, GDN_WIDTH), lambda bi, i: (bi, tmap(i), 1)),
            pl.BlockSpec((nb, tt, GDN_WIDTH), lambda bi, i: (bi, tmap(i), 2)),
            pl.BlockSpec((1, nb, tt, LANES), lambda bi, i: (d, bi, tmap(i), 0)),
            pl.BlockSpec((1, nb, nc, 8, H * C), lambda bi, i: (d, bi, tmap(i), 0, 0)),
        ]

    out_sds = jax.ShapeDtypeStruct((b, seq, GDN_WIDTH), F32)
    return pl.pallas_call(
        functools.partial(_gdn_scan_kernel, tt=tt, nb=nb),
        grid=(b // nb, nt),
        in_specs=dir_specs(0) + dir_specs(1),
        out_specs=[pl.BlockSpec((nb, tt, GDN_WIDTH), lambda bi, i: (bi, i, 0)),
                   pl.BlockSpec((nb, tt, GDN_WIDTH), lambda bi, i: (bi, nt - 1 - i, 0))],
        out_shape=[out_sds, out_sds],
        scratch_shapes=[
            pltpu.VMEM((nstream * H, dh, dh), F32),
            pltpu.VMEM((nc, nstream * H, 2 * C, dh), BF16),
            pltpu.VMEM((nc, nstream, dh + C, H * C), BF16),
            pltpu.VMEM((nc, nstream, C, H * dh), F32),
        ],
        compiler_params=_cparams(("parallel", "arbitrary")),
        name="gdn_scan",
    )(qkvp, qkvp, qkvp, gates, gcr, qkvp, qkvp, qkvp, gates, gcr)


def _na_kernel(q_ref, k_ref, v_ref, bm_ref, g_ref, o_ref, *, rows):
    W = GRID_W
    band = NA_WIN_R * W
    hd = NA_HEAD_DIM
    lane_q = lax.broadcasted_iota(jnp.int32, (W, LANES), 1)
    first = lane_q < hd
    scale = jnp.asarray(hd ** -0.5, BF16)

    nt_dims = (((1,), (1,)), ((), ()))

    def row_group(gi, carry):
        rs = [gi * NA_ROWS_PER_STEP + j for j in range(NA_ROWS_PER_STEP)]
        r0s = [jnp.clip(r - NA_WIN_R // 2, 0, rows - NA_WIN_R) for r in rs]
        qs_l, kb_l, vb_l = [], [], []
        for r, r0 in zip(rs, r0s):
            q2 = q_ref[0, pl.ds(pl.multiple_of(r * W, W), W), :] * scale
            zero = jnp.zeros_like(q2)
            qs_l.append(jnp.concatenate([jnp.where(first, q2, zero), jnp.where(first, zero, q2)], axis=0))
            kb_l.append(k_ref[0, pl.ds(pl.multiple_of(r0 * W, W), band), :])
            vb_l.append(v_ref[0, pl.ds(pl.multiple_of(r0 * W, W), band), :])
        s_l = [lax.dot_general(qs, kb, nt_dims, preferred_element_type=F32) for qs, kb in zip(qs_l, kb_l)]
        s_l = [s + bm_ref[0, r - r0] for s, r, r0 in zip(s_l, rs, r0s)]
        m_l = [jnp.max(s, axis=-1, keepdims=True) for s in s_l]
        p_l = [jnp.exp(s - m) for s, m in zip(s_l, m_l)]
        l_l = [jnp.sum(p, axis=-1, keepdims=True) for p in p_l]
        pb_l = [p.astype(BF16) for p in p_l]
        o0_l = [jnp.dot(pb[:W], vb, preferred_element_type=F32) for pb, vb in zip(pb_l, vb_l)]
        o1_l = [jnp.dot(pb[W:], vb, preferred_element_type=F32) for pb, vb in zip(pb_l, vb_l)]
        for r, o0, o1, l in zip(rs, o0_l, o1_l, l_l):
            o = jnp.where(first, o0 / l[:W], o1 / l[W:])
            sq = o * o
            ms0 = jnp.sum(jnp.where(first, sq, 0.0), axis=-1, keepdims=True)
            ms1 = jnp.sum(jnp.where(first, 0.0, sq), axis=-1, keepdims=True)
            ms = jnp.where(first, ms0, ms1) * (1.0 / hd)
            o_ref[0, pl.ds(pl.multiple_of(r * W, W), W), :] = (o * lax.rsqrt(ms + RMS_EPS) * g_ref[...]).astype(BF16)
        return carry

    lax.fori_loop(0, rows // NA_ROWS_PER_STEP, row_group, 0)


def _na(h_na, bias_tab, g_row):
    b, seq, _ = h_na.shape
    rows = seq // GRID_W
    assert rows >= NA_WIN_R
    npair = NA_WIDTH // LANES
    band = NA_WIN_R * GRID_W
    return pl.pallas_call(
        functools.partial(_na_kernel, rows=rows),
        grid=(b, npair),
        in_specs=[
            pl.BlockSpec((1, seq, LANES), lambda bi, p: (bi, 0, p)),
            pl.BlockSpec((1, seq, LANES), lambda bi, p: (bi, 0, npair + p)),
            pl.BlockSpec((1, seq, LANES), lambda bi, p: (bi, 0, 2 * npair + p)),
            pl.BlockSpec((1, NA_WIN_R, 2 * GRID_W, band), lambda bi, p: (p, 0, 0, 0)),
            pl.BlockSpec((1, LANES), lambda bi, p: (0, 0)),
        ],
        out_specs=pl.BlockSpec((1, seq, LANES), lambda bi, p: (bi, 0, p)),
        out_shape=jax.ShapeDtypeStruct((b, seq, NA_WIDTH), BF16),
        compiler_params=_cparams(("parallel", "parallel")),
        name="natten",
    )(h_na, h_na, h_na, bias_tab, g_row)


def _na_bias_table(rpb):
    W = GRID_W
    vi = np.arange(NA_WIN_R)
    kr = np.arange(NA_WIN_R)
    dr = kr[None, :] - vi[:, None] + NA_WIN_R - 1
    qc = np.arange(W)
    kc = np.arange(W)
    win_start = np.clip(qc - NA_WIN_C // 2, 0, W - NA_WIN_C)
    in_win = (kc[None, :] >= win_start[:, None]) & (kc[None, :] < win_start[:, None] + NA_WIN_C)
    dc = kc[None, :] - qc[:, None] + NA_WIN_C - 1
    rsel = (dr[:, :, None] == np.arange(2 * NA_WIN_R - 1)).astype(np.float32)
    csel = ((dc[:, :, None] == np.arange(2 * NA_WIN_C - 1)) & in_win[:, :, None]).astype(np.float32)
    tab = jnp.einsum("lhab,vka,qcb->lhvqkc", rpb.astype(F32), rsel, csel,
                     precision=lax.Precision.HIGHEST)
    tab = jnp.where(jnp.asarray(in_win)[:, None, :], tab, -jnp.inf)
    nl = rpb.shape[0]
    tab = tab.reshape(nl, NA_HEADS // 2, 2, NA_WIN_R, W, NA_WIN_R * W)
    tab = jnp.transpose(tab, (0, 1, 3, 2, 4, 5))
    return tab.reshape(nl, NA_HEADS // 2, NA_WIN_R, 2 * W, NA_WIN_R * W)


def _mix_ffn_kernel(x_ref, of_ref, ob_ref, z_ref, ona_ref, gg_ref, wo_ref, l1g_ref, l1b_ref,
                    w1_ref, b1_ref, w2_ref, b2_ref, l2g_ref, l2b_ref, out_ref):
    dh = GDN_HEAD_DIM
    tm = x_ref.shape[0]
    d_ff = w1_ref.shape[1]
    rows = [slice(s * (tm // FFN_SUBTILES), (s + 1) * (tm // FFN_SUBTILES)) for s in range(FFN_SUBTILES)]

    def gated(rs):
        o = of_ref[rs, :] + ob_ref[rs, :]
        z = z_ref[rs, :].astype(F32)
        gate = z * _sigmoid(z)
        parts = []
        for h in range(GDN_HEADS):
            cols = slice(h * dh, (h + 1) * dh)
            oh = o[:, cols]
            ms = jnp.mean(oh * oh, axis=-1, keepdims=True)
            parts.append((oh * lax.rsqrt(ms + RMS_EPS) * gg_ref[:, cols] * gate[:, cols]).astype(BF16))
        return jnp.concatenate(parts, axis=-1)

    og_l = [gated(rs) for rs in rows]
    mix_l = [jnp.dot(og, wo_ref[0:GDN_WIDTH, :], preferred_element_type=F32)
             + jnp.dot(ona_ref[rs, :], wo_ref[GDN_WIDTH:, :], preferred_element_type=F32)
             for og, rs in zip(og_l, rows)]
    x1_l = [_layer_norm(DEEPNORM_ALPHA * x_ref[rs, :] + mix, l1g_ref[...], l1b_ref[...])
            for mix, rs in zip(mix_l, rows)]
    x1b_l = [x1.astype(BF16) for x1 in x1_l]
    acc_l = [jnp.zeros(x1.shape, F32) for x1 in x1_l]
    for f in range(d_ff // FF_CHUNK):
        fs = slice(f * FF_CHUNK, (f + 1) * FF_CHUNK)
        hf_l = [jnp.dot(x1b, w1_ref[:, fs], preferred_element_type=F32) + b1_ref[:, fs] for x1b in x1b_l]
        hf_l = [jnp.square(jnp.maximum(hf, 0.0)).astype(BF16) for hf in hf_l]
        acc_l = [acc + jnp.dot(hf, w2_ref[fs, :], preferred_element_type=F32) for acc, hf in zip(acc_l, hf_l)]
    for rs, x1, acc in zip(rows, x1_l, acc_l):
        y = DEEPNORM_ALPHA * x1 + (acc + b2_ref[...])
        out_ref[rs, :] = _layer_norm(y, l2g_ref[...], l2b_ref[...])


def _mix_ffn(x2d, o_f, o_b, z, ona, gg, wo, l1g, l1b, w1, b1, w2, b2, l2g, l2b):
    bt, dm = x2d.shape
    d_ff = w1.shape[1]
    tm = TM_FFN
    row = lambda i: (i, 0)
    return pl.pallas_call(
        _mix_ffn_kernel,
        grid=(bt // tm,),
        in_specs=[
            pl.BlockSpec((tm, dm), row),
            pl.BlockSpec((tm, GDN_WIDTH), row),
            pl.BlockSpec((tm, GDN_WIDTH), row),
            pl.BlockSpec((tm, GDN_WIDTH), row),
            pl.BlockSpec((tm, NA_WIDTH), row),
            _const_spec((1, GDN_WIDTH)),
            _const_spec((dm, dm)),
            _const_spec((1, dm)), _const_spec((1, dm)),
            _const_spec((dm, d_ff)), _const_spec((1, d_ff)),
            _const_spec((d_ff, dm)), _const_spec((1, dm)),
            _const_spec((1, dm)), _const_spec((1, dm)),
        ],
        out_specs=pl.BlockSpec((tm, dm), row),
        out_shape=jax.ShapeDtypeStruct((bt, dm), F32),
        compiler_params=_cparams(("parallel",)),
        name="mix_ffn",
    )(x2d, o_f, o_b, z, ona, gg, wo, l1g, l1b, w1, b1, w2, b2, l2g, l2b)


def _split_w_in(w_l):
    dm = w_l.shape[0]
    h = GDN_HEADS
    a0 = 4 * GDN_WIDTH
    b0 = a0 + 2 * h
    na0 = b0 + 2 * h
    pad = jnp.zeros((dm, LANES - 2 * h), w_l.dtype)
    gates = jnp.concatenate([w_l[:, a0:a0 + h], w_l[:, b0:b0 + h], pad,
                             w_l[:, a0 + h:a0 + 2 * h], w_l[:, b0 + h:b0 + 2 * h], pad], axis=1)
    return tuple(w.astype(BF16) for w in (w_l[:, :C_QKV], w_l[:, C_QKV:a0], gates, w_l[:, na0:]))


def _gate_rows(p):
    return jnp.pad(p.astype(F32), ((0, 0), (0, LANES - p.shape[1])))


def kernel(x, ln_in_g, ln_in_b, w_in, conv_w, a_log, dt_bias, gdn_norm_g, rpb, na_norm_g, w_out,
           ln1_g, ln1_b, w1, b1, w2, b2, ln2_g, ln2_b):
    B, T, dm = x.shape
    bt = B * T
    nchunks = T // GDN_CHUNK
    row = lambda v: v.reshape(1, -1).astype(F32)
    xs = x.reshape(bt, dm)
    bias_tabs = _na_bias_table(rpb)
    for l in range(DEPTH):
        w_l = _split_w_in(w_in[l])
        conv8 = jnp.pad(conv_w[l].astype(F32), ((0, 8 - CONV_WIDTH), (0, 0)))
        outs = _inproj(xs, row(ln_in_g), row(ln_in_b), w_l, conv8, _gate_rows(a_log[l]), _gate_rows(dt_bias[l]),
                       apply_ln=(l == 0), seq=T)
        if l == 0:
            xs, qkvp, z, gates, gcr, h_na = outs
        else:
            qkvp, z, gates, gcr, h_na = outs
        qkvp = qkvp.reshape(B, T, C_QKV)
        gates4 = gates.reshape(2, B, T, LANES)
        gcr = gcr.reshape(2, B, nchunks, 8, GDN_HEADS * GDN_CHUNK)
        o_f, o_b = _gdn_scan(qkvp, gates4, gcr)
        ona = _na(h_na.reshape(B, T, C_NA), bias_tabs[l],
                  jnp.tile(na_norm_g[l].astype(F32), LANES // NA_HEAD_DIM).reshape(1, LANES))
        xs = _mix_ffn(xs, o_f.reshape(bt, GDN_WIDTH), o_b.reshape(bt, GDN_WIDTH), z, ona.reshape(bt, NA_WIDTH),
                      jnp.tile(gdn_norm_g[l].astype(F32), GDN_HEADS).reshape(1, GDN_WIDTH),
                      w_out[l].astype(BF16), row(ln1_g[l]), row(ln1_b[l]),
                      w1[l].astype(BF16), row(b1[l]), w2[l].astype(BF16), row(b2[l]),
                      row(ln2_g[l]), row(ln2_b[l]))
    return xs.reshape(B, T, dm)
```

```python
import functools
import math

import jax
import jax.numpy as jnp
import numpy as np
from jax import lax
from jax.experimental import pallas as pl
from jax.experimental.pallas import tpu as pltpu

F32 = jnp.float32
BF16 = jnp.bfloat16

GRID_W = 64
GDN_HEAD_DIM = 128
GDN_HEADS = 4
GDN_WIDTH = GDN_HEADS * GDN_HEAD_DIM
NA_HEAD_DIM = 64
NA_HEADS = 8
NA_WIDTH = NA_HEADS * NA_HEAD_DIM
CONV_WIDTH = 5
GDN_CHUNK = 64
NA_WIN_R = 8
NA_WIN_C = 16
DEPTH = 2
DEEPNORM_ALPHA = (2 * DEPTH) ** 0.25
LN_EPS = 1e-5
RMS_EPS = 1e-6

LANES = 128
VMEM_LIMIT = 56 * 1024 * 1024

TM_PROJ = 512
TM_FFN = 512
TT_SCAN = 256
FF_CHUNK = 1024
NA_ROWS_PER_STEP = 8
FFN_SUBTILES = 2
GDN_BATCH_PER_STEP = 4
GDN_PREP_CHUNKS = 2

C_QKV = 3 * GDN_WIDTH
C_Z = GDN_WIDTH
C_NA = 3 * NA_WIDTH


def _cparams(sem):
    return pltpu.CompilerParams(dimension_semantics=sem, vmem_limit_bytes=VMEM_LIMIT)


def _const_spec(shape):
    nd = len(shape)
    return pl.BlockSpec(shape, lambda *_: (0,) * nd, pipeline_mode=pl.Buffered(1))


def _layer_norm(y, g, b):
    mu = jnp.mean(y, axis=-1, keepdims=True)
    yc = y - mu
    var = jnp.mean(yc * yc, axis=-1, keepdims=True)
    return yc * lax.rsqrt(var + LN_EPS) * g + b


def _sigmoid(x):
    return 1.0 / (1.0 + jnp.exp(-x))


def _chunk_cumsum(g, tpos, reverse):
    n = g.shape[0]
    s = 1
    while s < GDN_CHUNK:
        if reverse:
            g = g + jnp.where(tpos < GDN_CHUNK - s, pltpu.roll(g, n - s, 0), 0.0)
        else:
            g = g + jnp.where(tpos >= s, pltpu.roll(g, s, 0), 0.0)
        s *= 2
    return g


CONV_HALO = 8


def _inproj_kernel(x_ref, xp_ref, xn_ref, lng_ref, lnb_ref, wq_ref, wz_ref, wg_ref, wna_ref, cw_ref, alog_ref, dtb_ref,
                   *refs, apply_ln, tiles_per_seq):
    if apply_ln:
        xo_ref, qkv_ref, z_ref, gates_ref, gcr_ref, na_ref, hq_ref = refs
    else:
        qkv_ref, z_ref, gates_ref, gcr_ref, na_ref, hq_ref = refs
    i = pl.program_id(0)
    tm = x_ref.shape[0]
    halo = CONV_HALO
    x, x_prev, x_next = x_ref[...], xp_ref[...], xn_ref[...]
    if apply_ln:
        x = _layer_norm(x, lng_ref[...], lnb_ref[...])
        x_prev = _layer_norm(x_prev, lng_ref[...], lnb_ref[...])
        x_next = _layer_norm(x_next, lng_ref[...], lnb_ref[...])
        xo_ref[...] = x
    xb = x.astype(BF16)
    x_ext = jnp.concatenate([x_prev, x, x_next], axis=0).astype(BF16)

    pos = i % tiles_per_seq
    keep_prev = jnp.where(pos == 0, 0.0, 1.0).astype(F32)
    keep_next = jnp.where(pos == tiles_per_seq - 1, 0.0, 1.0).astype(F32)
    hq = jnp.dot(x_ext, wq_ref[...], preferred_element_type=F32)
    hq_ref[0:halo, :] = hq[0:halo] * keep_prev
    hq_ref[halo:tm + halo, :] = hq[halo:tm + halo]
    hq_ref[tm + halo:tm + 2 * halo, :] = hq[tm + halo:tm + 2 * halo] * keep_next
    z_ref[...] = jnp.dot(xb, wz_ref[...], preferred_element_type=F32).astype(BF16)
    na_ref[...] = jnp.dot(xb, wna_ref[...], preferred_element_type=F32).astype(BF16)
    hab2 = jnp.dot(xb, wg_ref[...], preferred_element_type=F32)

    first_tap = halo - CONV_WIDTH // 2
    for j in range(C_QKV // LANES):
        cols = slice(j * LANES, (j + 1) * LANES)
        y = hq_ref[first_tap:first_tap + tm, cols] * cw_ref[0:1, cols]
        for tap in range(1, CONV_WIDTH):
            y = y + hq_ref[first_tap + tap:first_tap + tap + tm, cols] * cw_ref[tap:tap + 1, cols]
        y = y * _sigmoid(y)
        if j < 2 * GDN_HEADS:
            fac = lax.rsqrt(jnp.sum(y * y, axis=-1, keepdims=True) + RMS_EPS)
            if j < GDN_HEADS:
                fac = fac * (GDN_HEAD_DIM ** -0.5)
            y = y * fac
        qkv_ref[:, cols] = y.astype(BF16)

    tpos = lax.broadcasted_iota(jnp.int32, (tm, LANES), 0) & (GDN_CHUNK - 1)
    lane = lax.broadcasted_iota(jnp.int32, (tm, LANES), 1)
    for d in range(2):
        hab = hab2[:, d * LANES:(d + 1) * LANES]
        sp_in = hab + dtb_ref[d:d + 1, :]
        softplus = jnp.maximum(sp_in, 0.0) + jnp.log1p(jnp.exp(-jnp.abs(sp_in)))
        g = -jnp.exp(alog_ref[d:d + 1, :]) * softplus
        g = jnp.where(lane < GDN_HEADS, g, 0.0)
        gc = _chunk_cumsum(g, tpos, reverse=(d == 1))
        gates_ref[d] = jnp.where(lane < GDN_HEADS, gc, _sigmoid(hab))
        gct = gc.T
        pad_rows = jnp.zeros((7, GDN_HEADS * GDN_CHUNK), F32)
        for c in range(tm // GDN_CHUNK):
            row_c = jnp.concatenate([gct[h:h + 1, c * GDN_CHUNK:(c + 1) * GDN_CHUNK] for h in range(GDN_HEADS)], axis=1)
            gcr_ref[d, c] = jnp.concatenate([row_c, pad_rows], axis=0)


def _inproj(x2d, lng, lnb, ws, conv_w8, alog_rows, dtb_rows, apply_ln, seq):
    bt, dm = x2d.shape
    tm = TM_PROJ
    grid = (bt // tm,)
    row = lambda i: (i, 0)
    hb = tm // CONV_HALO
    last_hb = bt // CONV_HALO - 1
    in_specs = [
        pl.BlockSpec((tm, dm), row),
        pl.BlockSpec((CONV_HALO, dm), lambda i: (jnp.maximum(i * hb - 1, 0), 0)),
        pl.BlockSpec((CONV_HALO, dm), lambda i: (jnp.minimum((i + 1) * hb, last_hb), 0)),
        _const_spec((1, dm)), _const_spec((1, dm)),
        _const_spec((dm, C_QKV)), _const_spec((dm, C_Z)), _const_spec((dm, 2 * LANES)), _const_spec((dm, C_NA)),
        _const_spec((8, C_QKV)),
        _const_spec((2, LANES)), _const_spec((2, LANES)),
    ]
    out_shape = [
        jax.ShapeDtypeStruct((bt, C_QKV), BF16),
        jax.ShapeDtypeStruct((bt, C_Z), BF16),
        jax.ShapeDtypeStruct((2, bt, LANES), F32),
        jax.ShapeDtypeStruct((2, bt // GDN_CHUNK, 8, GDN_HEADS * GDN_CHUNK), F32),
        jax.ShapeDtypeStruct((bt, C_NA), BF16),
    ]
    out_specs = [
        pl.BlockSpec((tm, C_QKV), row),
        pl.BlockSpec((tm, C_Z), row),
        pl.BlockSpec((2, tm, LANES), lambda i: (0, i, 0)),
        pl.BlockSpec((2, tm // GDN_CHUNK, 8, GDN_HEADS * GDN_CHUNK), lambda i: (0, i, 0, 0)),
        pl.BlockSpec((tm, C_NA), row),
    ]
    if apply_ln:
        out_shape = [jax.ShapeDtypeStruct((bt, dm), F32)] + out_shape
        out_specs = [pl.BlockSpec((tm, dm), row)] + out_specs
    return pl.pallas_call(
        functools.partial(_inproj_kernel, apply_ln=apply_ln, tiles_per_seq=seq // tm),
        grid=grid, in_specs=in_specs, out_specs=out_specs, out_shape=out_shape,
        scratch_shapes=[pltpu.VMEM((tm + 2 * CONV_HALO, C_QKV), F32)],
        compiler_params=_cparams(("parallel",)),
        name="inproj_ln" if apply_ln else "inproj",
    )(x2d, x2d, x2d, lng, lnb, *ws, conv_w8, alog_rows, dtb_rows)


def _block_diag(x, nblk):
    w = x.shape[1] // nblk
    blk = lax.broadcasted_iota(jnp.int32, x.shape, 1) // w
    zero = jnp.zeros_like(x)
    return jnp.concatenate([jnp.where(blk == h, x, zero) for h in range(nblk)], axis=0)


def _lane_blocks(cols, width):
    r, n = cols.shape
    return jnp.concatenate([jnp.broadcast_to(cols[:, j:j + 1], (r, width)) for j in range(n)], axis=1)


def _unit_tri_inverses(mats, nblk):
    c = mats[0].shape[0]
    ri = lax.broadcasted_iota(jnp.int32, (c, nblk * c), 0)
    ci = lax.broadcasted_iota(jnp.int32, (c, nblk * c), 1) % c
    eye = (ri == ci).astype(F32)
    ps = [-a for a in mats]
    xs = [eye + p for p in ps]
    pbs = [p.astype(BF16) for p in ps]
    ps = [jnp.dot(pb, _block_diag(pb, nblk), preferred_element_type=F32) for pb in pbs]
    m = 2
    while 2 * m < c:
        pbs = [p.astype(BF16) for p in ps]
        prods = [jnp.dot(jnp.concatenate([x.astype(BF16), pb], axis=0), _block_diag(pb, nblk),
                         preferred_element_type=F32) for x, pb in zip(xs, pbs)]
        xs = [x + prod[:c] for x, prod in zip(xs, prods)]
        ps = [prod[c:] for prod in prods]
        m *= 2
    return [x + jnp.dot(x.astype(BF16), _block_diag(p.astype(BF16), nblk), preferred_element_type=F32)
            for x, p in zip(xs, ps)]


def _gdn_scan_kernel(qf_ref, kf_ref, vf_ref, gf_ref, grf_ref, qb_ref, kb_ref, vb_ref, gb_ref, grb_ref,
                     of_ref, ob_ref, s_ref, stk_ref, wq_ref, vbeta_ref, *, tt, nb):
    i = pl.program_id(1)
    nc = tt // GDN_CHUNK
    C = GDN_CHUNK
    dh = GDN_HEAD_DIM
    H = GDN_HEADS

    @pl.when(i == 0)
    def _():
        s_ref[...] = jnp.zeros_like(s_ref)

    ri = lax.broadcasted_iota(jnp.int32, (C, H * C), 0)
    ci = lax.broadcasted_iota(jnp.int32, (C, H * C), 1) % C
    mask_incl = (ri >= ci, ri <= ci)
    mask_strict = (ri > ci, ri < ci)
    dirs = ((qf_ref, kf_ref, vf_ref, gf_ref, grf_ref, of_ref), (qb_ref, kb_ref, vb_ref, gb_ref, grb_ref, ob_ref))
    streams = [(b, d) for b in range(nb) for d in range(2)]
    half_lo = lax.broadcasted_iota(jnp.int32, (C, dh), 1) < C
    nt_dims = (((1,), (1,)), ((), ()))
    tn_dims = (((0,), (0,)), ((), ()))

    def chunk_end_decay(gt, d):
        return gt[C - 1:C, :] if d == 0 else gt[0:1, :]

    def prep_group(cg, carry):
        items = [(cg * GDN_PREP_CHUNKS + j, u) for j in range(GDN_PREP_CHUNKS) for u in range(len(streams))]
        lhs_l, bdk_l, gam_l, kd_l = [], [], [], []
        for c, u in items:
            b, d = streams[u]
            q_ref, k_ref, v_ref, g_ref, gr_ref, _ = dirs[d]
            r0 = pl.multiple_of(c * C, C)
            gt = g_ref[0, b, pl.ds(r0, C), :]
            gr = gr_ref[0, b, c]
            q = q_ref[b, pl.ds(r0, C), :]
            k = k_ref[b, pl.ds(r0, C), :]
            v = v_ref[b, pl.ds(r0, C), :].astype(F32)
            kf = k.astype(F32)
            gc_n = _lane_blocks(gt[:, 0:H], dh)
            beta = _lane_blocks(gt[:, H:2 * H], dh)
            gend_n = _lane_blocks(chunk_end_decay(gt, d)[:, 0:H], dh)
            e_n = jnp.exp(gc_n)
            kb = kf * beta
            kbe = (kb * e_n).astype(BF16)
            qd = (q.astype(F32) * e_n).astype(BF16)
            kdec = (kf * jnp.exp(gend_n - gc_n)).astype(BF16)
            gc_c = jnp.concatenate(
                [jnp.where(half_lo, gc_n[:, (2 * p) * dh:(2 * p + 1) * dh], gc_n[:, (2 * p + 1) * dh:(2 * p + 2) * dh])
                 for p in range(H // 2)], axis=1)
            for h in range(H):
                cols = slice(h * dh, (h + 1) * dh)
                stk_ref[c, u * H + h] = jnp.concatenate([kbe[:, cols], qd[:, cols]], axis=0)
            vbeta_ref[c, u] = v * beta
            lhs_l.append(jnp.concatenate([kb.astype(BF16), q], axis=0))
            bdk_l.append(_block_diag(k, H))
            kd_l.append(jnp.concatenate([kdec[:, h * dh:(h + 1) * dh] for h in range(H)], axis=0))
            gam_l.append(jnp.where(mask_incl[d], jnp.exp(gc_c - gr[0:1, :]), 0.0))
        kq_l = [lax.dot_general(lhs, bdk, nt_dims, preferred_element_type=F32)
                for lhs, bdk in zip(lhs_l, bdk_l)]
        a_l = [jnp.where(mask_strict[streams[u][1]], kq[:C] * gam, 0.0)
               for kq, gam, (_, u) in zip(kq_l, gam_l, items)]
        qk_l = [(kq[C:] * gam).astype(BF16) for kq, gam in zip(kq_l, gam_l)]
        bdt_l = [_block_diag(t.astype(BF16), H) for t in _unit_tri_inverses(a_l, H)]
        wc_l = [lax.dot_general(kd, bdt, tn_dims, preferred_element_type=F32) for kd, bdt in zip(kd_l, bdt_l)]
        qt_l = [jnp.dot(qk, bdt, preferred_element_type=F32) for qk, bdt in zip(qk_l, bdt_l)]
        for (c, u), wc, qt in zip(items, wc_l, qt_l):
            wq_ref[c, u] = jnp.concatenate([wc, qt], axis=0).astype(BF16)
        return carry

    lax.fori_loop(0, nc // GDN_PREP_CHUNKS, prep_group, 0)

    def scan_step(step, carry):
        cs = (step, nc - 1 - step)
        egs, x1_l = [], []
        for u, (b, d) in enumerate(streams):
            r0 = pl.multiple_of(cs[d] * C, C)
            egs.append(jnp.exp(chunk_end_decay(dirs[d][3][0, b, pl.ds(r0, C), :], d)))
        s_l = [[s_ref[u * H + h] for h in range(H)] for u in range(len(streams))]
        for u, (b, d) in enumerate(streams):
            x1_l.append([jnp.dot(stk_ref[cs[d], u * H + h], s_l[u][h].astype(BF16), preferred_element_type=F32)
                         for h in range(H)])
        res_l = [(vbeta_ref[cs[d], u] - jnp.concatenate([x1[:C] for x1 in x1_l[u]], axis=1)).astype(BF16)
                 for u, (b, d) in enumerate(streams)]
        z_l = [jnp.dot(wq_ref[cs[d], u], _block_diag(res, H), preferred_element_type=F32)
               for (u, (b, d)), res in zip(enumerate(streams), res_l)]
        for u, (b, d) in enumerate(streams):
            z = z_l[u]
            for h in range(H):
                s_ref[u * H + h] = s_l[u][h] * egs[u][:, h:h + 1] + z[:dh, h * dh:(h + 1) * dh]
            r0 = pl.multiple_of(cs[d] * C, C)
            dirs[d][5][b, pl.ds(r0, C), :] = jnp.concatenate([x1[C:] for x1 in x1_l[u]], axis=1) + z[dh:]
        return carry

    lax.fori_loop(0, nc, scan_step, 0)


def _gdn_scan(qkvp, gates, gcr):
    b, seq, _ = qkvp.shape
    tt = TT_SCAN
    nb = GDN_BATCH_PER_STEP if b % GDN_BATCH_PER_STEP == 0 else 1
    nt = seq // tt
    nc = tt // GDN_CHUNK
    C = GDN_CHUNK
    dh = GDN_HEAD_DIM
    H = GDN_HEADS
    nstream = 2 * nb

    def dir_specs(d):
        tmap = (lambda i: i) if d == 0 else (lambda i: nt - 1 - i)
        return [
            pl.BlockSpec((nb, tt, GDN_WIDTH), lambda bi, i: (bi, tmap(i), 0)),
            pl.BlockSpec((nb, tt, GDN_WIDTH), lambda bi, i: (bi, tmap(i), 1)),
            pl.BlockSpec((nb, tt, GDN_WIDTH), lambda bi, i: (bi, tmap(i), 2)),
            pl.BlockSpec((1, nb, tt, LANES), lambda bi, i: (d, bi, tmap(i), 0)),
            pl.BlockSpec((1, nb, nc, 8, H * C), lambda bi, i: (d, bi, tmap(i), 0, 0)),
        ]

    out_sds = jax.ShapeDtypeStruct((b, seq, GDN_WIDTH), F32)
    return pl.pallas_call(
        functools.partial(_gdn_scan_kernel, tt=tt, nb=nb),
        grid=(b // nb, nt),
        in_specs=dir_specs(0) + dir_specs(1),
        out_specs=[pl.BlockSpec((nb, tt, GDN_WIDTH), lambda bi, i: (bi, i, 0)),
                   pl.BlockSpec((nb, tt, GDN_WIDTH), lambda bi, i: (bi, nt - 1 - i, 0))],
        out_shape=[out_sds, out_sds],
        scratch_shapes=[
            pltpu.VMEM((nstream * H, dh, dh), F32),
            pltpu.VMEM((nc, nstream * H, 2 * C, dh), BF16),
            pltpu.VMEM((nc, nstream, dh + C, H * C), BF16),
            pltpu.VMEM((nc, nstream, C, H * dh), F32),
        ],
        compiler_params=_cparams(("parallel", "arbitrary")),
        name="gdn_scan",
    )(qkvp, qkvp, qkvp, gates, gcr, qkvp, qkvp, qkvp, gates, gcr)


def _na_kernel(q_ref, k_ref, v_ref, bm_ref, g_ref, o_ref, *, rows):
    W = GRID_W
    band = NA_WIN_R * W
    hd = NA_HEAD_DIM
    lane_q = lax.broadcasted_iota(jnp.int32, (W, LANES), 1)
    first = lane_q < hd
    scale = jnp.asarray(hd ** -0.5, BF16)

    nt_dims = (((1,), (1,)), ((), ()))

    def row_group(gi, carry):
        rs = [gi * NA_ROWS_PER_STEP + j for j in range(NA_ROWS_PER_STEP)]
        r0s = [jnp.clip(r - NA_WIN_R // 2, 0, rows - NA_WIN_R) for r in rs]
        qs_l, kb_l, vb_l = [], [], []
        for r, r0 in zip(rs, r0s):
            q2 = q_ref[0, pl.ds(pl.multiple_of(r * W, W), W), :] * scale
            zero = jnp.zeros_like(q2)
            qs_l.append(jnp.concatenate([jnp.where(first, q2, zero), jnp.where(first, zero, q2)], axis=0))
            kb_l.append(k_ref[0, pl.ds(pl.multiple_of(r0 * W, W), band), :])
            vb_l.append(v_ref[0, pl.ds(pl.multiple_of(r0 * W, W), band), :])
        s_l = [lax.dot_general(qs, kb, nt_dims, preferred_element_type=F32) for qs, kb in zip(qs_l, kb_l)]
        s_l = [s + bm_ref[0, r - r0] for s, r, r0 in zip(s_l, rs, r0s)]
        m_l = [jnp.max(s, axis=-1, keepdims=True) for s in s_l]
        p_l = [jnp.exp(s - m) for s, m in zip(s_l, m_l)]
        l_l = [jnp.sum(p, axis=-1, keepdims=True) for p in p_l]
        pb_l = [p.astype(BF16) for p in p_l]
        o0_l = [jnp.dot(pb[:W], vb, preferred_element_type=F32) for pb, vb in zip(pb_l, vb_l)]
        o1_l = [jnp.dot(pb[W:], vb, preferred_element_type=F32) for pb, vb in zip(pb_l, vb_l)]
        for r, o0, o1, l in zip(rs, o0_l, o1_l, l_l):
            o = jnp.where(first, o0 / l[:W], o1 / l[W:])
            sq = o * o
            ms0 = jnp.sum(jnp.where(first, sq, 0.0), axis=-1, keepdims=True)
            ms1 = jnp.sum(jnp.where(first, 0.0, sq), axis=-1, keepdims=True)
            ms = jnp.where(first, ms0, ms1) * (1.0 / hd)
            o_ref[0, pl.ds(pl.multiple_of(r * W, W), W), :] = (o * lax.rsqrt(ms + RMS_EPS) * g_ref[...]).astype(BF16)
        return carry

    lax.fori_loop(0, rows // NA_ROWS_PER_STEP, row_group, 0)


def _na(h_na, bias_tab, g_row):
    b, seq, _ = h_na.shape
    rows = seq // GRID_W
    assert rows >= NA_WIN_R
    npair = NA_WIDTH // LANES
    band = NA_WIN_R * GRID_W
    return pl.pallas_call(
        functools.partial(_na_kernel, rows=rows),
        grid=(b, npair),
        in_specs=[
            pl.BlockSpec((1, seq, LANES), lambda bi, p: (bi, 0, p)),
            pl.BlockSpec((1, seq, LANES), lambda bi, p: (bi, 0, npair + p)),
            pl.BlockSpec((1, seq, LANES), lambda bi, p: (bi, 0, 2 * npair + p)),
            pl.BlockSpec((1, NA_WIN_R, 2 * GRID_W, band), lambda bi, p: (p, 0, 0, 0)),
            pl.BlockSpec((1, LANES), lambda bi, p: (0, 0)),
        ],
        out_specs=pl.BlockSpec((1, seq, LANES), lambda bi, p: (bi, 0, p)),
        out_shape=jax.ShapeDtypeStruct((b, seq, NA_WIDTH), BF16),
        compiler_params=_cparams(("parallel", "parallel")),
        name="natten",
    )(h_na, h_na, h_na, bias_tab, g_row)


def _na_bias_table(rpb):
    W = GRID_W
    vi = np.arange(NA_WIN_R)
    kr = np.arange(NA_WIN_R)
    dr = kr[None, :] - vi[:, None] + NA_WIN_R - 1
    qc = np.arange(W)
    kc = np.arange(W)
    win_start = np.clip(qc - NA_WIN_C // 2, 0, W - NA_WIN_C)
    in_win = (kc[None, :] >= win_start[:, None]) & (kc[None, :] < win_start[:, None] + NA_WIN_C)
    dc = kc[None, :] - qc[:, None] + NA_WIN_C - 1
    rsel = (dr[:, :, None] == np.arange(2 * NA_WIN_R - 1)).astype(np.float32)
    csel = ((dc[:, :, None] == np.arange(2 * NA_WIN_C - 1)) & in_win[:, :, None]).astype(np.float32)
    tab = jnp.einsum("lhab,vka,qcb->lhvqkc", rpb.astype(F32), rsel, csel,
                     precision=lax.Precision.HIGHEST)
    tab = jnp.where(jnp.asarray(in_win)[:, None, :], tab, -jnp.inf)
    nl = rpb.shape[0]
    tab = tab.reshape(nl, NA_HEADS // 2, 2, NA_WIN_R, W, NA_WIN_R * W)
    tab = jnp.transpose(tab, (0, 1, 3, 2, 4, 5))
    return tab.reshape(nl, NA_HEADS // 2, NA_WIN_R, 2 * W, NA_WIN_R * W)


def _mix_ffn_kernel(x_ref, of_ref, ob_ref, z_ref, ona_ref, gg_ref, wo_ref, l1g_ref, l1b_ref,
                    w1_ref, b1_ref, w2_ref, b2_ref, l2g_ref, l2b_ref, out_ref):
    dh = GDN_HEAD_DIM
    tm = x_ref.shape[0]
    d_ff = w1_ref.shape[1]
    rows = [slice(s * (tm // FFN_SUBTILES), (s + 1) * (tm // FFN_SUBTILES)) for s in range(FFN_SUBTILES)]

    def gated(rs):
        o = of_ref[rs, :] + ob_ref[rs, :]
        z = z_ref[rs, :].astype(F32)
        gate = z * _sigmoid(z)
        parts = []
        for h in range(GDN_HEADS):
            cols = slice(h * dh, (h + 1) * dh)
            oh = o[:, cols]
            ms = jnp.mean(oh * oh, axis=-1, keepdims=True)
            parts.append((oh * lax.rsqrt(ms + RMS_EPS) * gg_ref[:, cols] * gate[:, cols]).astype(BF16))
        return jnp.concatenate(parts, axis=-1)

    og_l = [gated(rs) for rs in rows]
    mix_l = [jnp.dot(og, wo_ref[0:GDN_WIDTH, :], preferred_element_type=F32)
             + jnp.dot(ona_ref[rs, :], wo_ref[GDN_WIDTH:, :], preferred_element_type=F32)
             for og, rs in zip(og_l, rows)]
    x1_l = [_layer_norm(DEEPNORM_ALPHA * x_ref[rs, :] + mix, l1g_ref[...], l1b_ref[...])
            for mix, rs in zip(mix_l, rows)]
    x1b_l = [x1.astype(BF16) for x1 in x1_l]
    acc_l = [jnp.zeros(x1.shape, F32) for x1 in x1_l]
    for f in range(d_ff // FF_CHUNK):
        fs = slice(f * FF_CHUNK, (f + 1) * FF_CHUNK)
        hf_l = [jnp.dot(x1b, w1_ref[:, fs], preferred_element_type=F32) + b1_ref[:, fs] for x1b in x1b_l]
        hf_l = [jnp.square(jnp.maximum(hf, 0.0)).astype(BF16) for hf in hf_l]
        acc_l = [acc + jnp.dot(hf, w2_ref[fs, :], preferred_element_type=F32) for acc, hf in zip(acc_l, hf_l)]
    for rs, x1, acc in zip(rows, x1_l, acc_l):
        y = DEEPNORM_ALPHA * x1 + (acc + b2_ref[...])
        out_ref[rs, :] = _layer_norm(y, l2g_ref[...], l2b_ref[...])


def _mix_ffn(x2d, o_f, o_b, z, ona, gg, wo, l1g, l1b, w1, b1, w2, b2, l2g, l2b):
    bt, dm = x2d.shape
    d_ff = w1.shape[1]
    tm = TM_FFN
    row = lambda i: (i, 0)
    return pl.pallas_call(
        _mix_ffn_kernel,
        grid=(bt // tm,),
        in_specs=[
            pl.BlockSpec((tm, dm), row),
            pl.BlockSpec((tm, GDN_WIDTH), row),
            pl.BlockSpec((tm, GDN_WIDTH), row),
            pl.BlockSpec((tm, GDN_WIDTH), row),
            pl.BlockSpec((tm, NA_WIDTH), row),
            _const_spec((1, GDN_WIDTH)),
            _const_spec((dm, dm)),
            _const_spec((1, dm)), _const_spec((1, dm)),
            _const_spec((dm, d_ff)), _const_spec((1, d_ff)),
            _const_spec((d_ff, dm)), _const_spec((1, dm)),
            _const_spec((1, dm)), _const_spec((1, dm)),
        ],
        out_specs=pl.BlockSpec((tm, dm), row),
        out_shape=jax.ShapeDtypeStruct((bt, dm), F32),
        compiler_params=_cparams(("parallel",)),
        name="mix_ffn",
    )(x2d, o_f, o_b, z, ona, gg, wo, l1g, l1b, w1, b1, w2, b2, l2g, l2b)


def _split_w_in(w_l):
    dm = w_l.shape[0]
    h = GDN_HEADS
    a0 = 4 * GDN_WIDTH
    b0 = a0 + 2 * h
    na0 = b0 + 2 * h
    pad = jnp.zeros((dm, LANES - 2 * h), w_l.dtype)
    gates = jnp.concatenate([w_l[:, a0:a0 + h], w_l[:, b0:b0 + h], pad,
                             w_l[:, a0 + h:a0 + 2 * h], w_l[:, b0 + h:b0 + 2 * h], pad], axis=1)
    return tuple(w.astype(BF16) for w in (w_l[:, :C_QKV], w_l[:, C_QKV:a0], gates, w_l[:, na0:]))


def _gate_rows(p):
    return jnp.pad(p.astype(F32), ((0, 0), (0, LANES - p.shape[1])))


def kernel(x, ln_in_g, ln_in_b, w_in, conv_w, a_log, dt_bias, gdn_norm_g, rpb, na_norm_g, w_out,
           ln1_g, ln1_b, w1, b1, w2, b2, ln2_g, ln2_b):
    B, T, dm = x.shape
    bt = B * T
    nchunks = T // GDN_CHUNK
    row = lambda v: v.reshape(1, -1).astype(F32)
    xs = x.reshape(bt, dm)
    bias_tabs = _na_bias_table(rpb)
    for l in range(DEPTH):
        w_l = _split_w_in(w_in[l])
        conv8 = jnp.pad(conv_w[l].astype(F32), ((0, 8 - CONV_WIDTH), (0, 0)))
        outs = _inproj(xs, row(ln_in_g), row(ln_in_b), w_l, conv8, _gate_rows(a_log[l]), _gate_rows(dt_bias[l]),
                       apply_ln=(l == 0), seq=T)
        if l == 0:
            xs, qkvp, z, gates, gcr, h_na = outs
        else:
            qkvp, z, gates, gcr, h_na = outs
        qkvp = qkvp.reshape(B, T, C_QKV)
        gates4 = gates.reshape(2, B, T, LANES)
        gcr = gcr.reshape(2, B, nchunks, 8, GDN_HEADS * GDN_CHUNK)
        o_f, o_b = _gdn_scan(qkvp, gates4, gcr)
        ona = _na(h_na.reshape(B, T, C_NA), bias_tabs[l],
                  jnp.tile(na_norm_g[l].astype(F32), LANES // NA_HEAD_DIM).reshape(1, LANES))
        xs = _mix_ffn(xs, o_f.reshape(bt, GDN_WIDTH), o_b.reshape(bt, GDN_WIDTH), z, ona.reshape(bt, NA_WIDTH),
                      jnp.tile(gdn_norm_g[l].astype(F32), GDN_HEADS).reshape(1, GDN_WIDTH),
                      w_out[l].astype(BF16), row(ln1_g[l]), row(ln1_b[l]),
                      w1[l].astype(BF16), row(b1[l]), w2[l].astype(BF16), row(b2[l]),
                      row(ln2_g[l]), row(ln2_b[l]))
    return xs.reshape(B, T, dm)
```

```python
import functools
import math

import jax
import jax.numpy as jnp
import numpy as np
from jax import lax
from jax.experimental import pallas as pl
from jax.experimental.pallas import tpu as pltpu

F32 = jnp.float32
BF16 = jnp.bfloat16

GRID_W = 64
GDN_HEAD_DIM = 128
GDN_HEADS = 4
GDN_WIDTH = GDN_HEADS * GDN_HEAD_DIM
NA_HEAD_DIM = 64
NA_HEADS = 8
NA_WIDTH = NA_HEADS * NA_HEAD_DIM
CONV_WIDTH = 5
GDN_CHUNK = 64
NA_WIN_R = 8
NA_WIN_C = 16
DEPTH = 2
DEEPNORM_ALPHA = (2 * DEPTH) ** 0.25
LN_EPS = 1e-5
RMS_EPS = 1e-6

LANES = 128
VMEM_LIMIT = 56 * 1024 * 1024

TM_PROJ = 512
TM_FFN = 512
TT_SCAN = 256
FF_CHUNK = 1024
NA_ROWS_PER_STEP = 8
FFN_SUBTILES = 2
GDN_BATCH_PER_STEP = 4
GDN_PREP_CHUNKS = 2

C_QKV = 3 * GDN_WIDTH
C_Z = GDN_WIDTH
C_NA = 3 * NA_WIDTH


def _cparams(sem):
    return pltpu.CompilerParams(dimension_semantics=sem, vmem_limit_bytes=VMEM_LIMIT)


def _const_spec(shape):
    nd = len(shape)
    return pl.BlockSpec(shape, lambda *_: (0,) * nd, pipeline_mode=pl.Buffered(1))


def _layer_norm(y, g, b):
    mu = jnp.mean(y, axis=-1, keepdims=True)
    yc = y - mu
    var = jnp.mean(yc * yc, axis=-1, keepdims=True)
    return yc * lax.rsqrt(var + LN_EPS) * g + b


def _sigmoid(x):
    return 1.0 / (1.0 + jnp.exp(-x))


def _chunk_cumsum(g, tpos, reverse):
    n = g.shape[0]
    s = 1
    while s < GDN_CHUNK:
        if reverse:
            g = g + jnp.where(tpos < GDN_CHUNK - s, pltpu.roll(g, n - s, 0), 0.0)
        else:
            g = g + jnp.where(tpos >= s, pltpu.roll(g, s, 0), 0.0)
        s *= 2
    return g


CONV_HALO = 8


def _inproj_kernel(x_ref, xp_ref, xn_ref, lng_ref, lnb_ref, wq_ref, wz_ref, wg_ref, wna_ref, cw_ref, alog_ref, dtb_ref,
                   *refs, apply_ln, tiles_per_seq):
    if apply_ln:
        xo_ref, qkv_ref, z_ref, gates_ref, gcr_ref, na_ref, hq_ref = refs
    else:
        qkv_ref, z_ref, gates_ref, gcr_ref, na_ref, hq_ref = refs
    i = pl.program_id(0)
    tm = x_ref.shape[0]
    halo = CONV_HALO
    x, x_prev, x_next = x_ref[...], xp_ref[...], xn_ref[...]
    if apply_ln:
        x = _layer_norm(x, lng_ref[...], lnb_ref[...])
        x_prev = _layer_norm(x_prev, lng_ref[...], lnb_ref[...])
        x_next = _layer_norm(x_next, lng_ref[...], lnb_ref[...])
        xo_ref[...] = x
    xb = x.astype(BF16)
    x_ext = jnp.concatenate([x_prev, x, x_next], axis=0).astype(BF16)

    pos = i % tiles_per_seq
    keep_prev = jnp.where(pos == 0, 0.0, 1.0).astype(F32)
    keep_next = jnp.where(pos == tiles_per_seq - 1, 0.0, 1.0).astype(F32)
    hq = jnp.dot(x_ext, wq_ref[...], preferred_element_type=F32)
    hq_ref[0:halo, :] = hq[0:halo] * keep_prev
    hq_ref[halo:tm + halo, :] = hq[halo:tm + halo]
    hq_ref[tm + halo:tm + 2 * halo, :] = hq[tm + halo:tm + 2 * halo] * keep_next
    z_ref[...] = jnp.dot(xb, wz_ref[...], preferred_element_type=F32).astype(BF16)
    na_ref[...] = jnp.dot(xb, wna_ref[...], preferred_element_type=F32).astype(BF16)
    hab2 = jnp.dot(xb, wg_ref[...], preferred_element_type=F32)

    first_tap = halo - CONV_WIDTH // 2
    for j in range(C_QKV // LANES):
        cols = slice(j * LANES, (j + 1) * LANES)
        y = hq_ref[first_tap:first_tap + tm, cols] * cw_ref[0:1, cols]
        for tap in range(1, CONV_WIDTH):
            y = y + hq_ref[first_tap + tap:first_tap + tap + tm, cols] * cw_ref[tap:tap + 1, cols]
        y = y * _sigmoid(y)
        if j < 2 * GDN_HEADS:
            fac = lax.rsqrt(jnp.sum(y * y, axis=-1, keepdims=True) + RMS_EPS)
            if j < GDN_HEADS:
                fac = fac * (GDN_HEAD_DIM ** -0.5)
            y = y * fac
        qkv_ref[:, cols] = y.astype(BF16)

    tpos = lax.broadcasted_iota(jnp.int32, (tm, LANES), 0) & (GDN_CHUNK - 1)
    lane = lax.broadcasted_iota(jnp.int32, (tm, LANES), 1)
    for d in range(2):
        hab = hab2[:, d * LANES:(d + 1) * LANES]
        sp_in = hab + dtb_ref[d:d + 1, :]
        softplus = jnp.maximum(sp_in, 0.0) + jnp.log1p(jnp.exp(-jnp.abs(sp_in)))
        g = -jnp.exp(alog_ref[d:d + 1, :]) * softplus
        g = jnp.where(lane < GDN_HEADS, g, 0.0)
        gc = _chunk_cumsum(g, tpos, reverse=(d == 1))
        gates_ref[d] = jnp.where(lane < GDN_HEADS, gc, _sigmoid(hab))
        gct = gc.T
        pad_rows = jnp.zeros((7, GDN_HEADS * GDN_CHUNK), F32)
        for c in range(tm // GDN_CHUNK):
            row_c = jnp.concatenate([gct[h:h + 1, c * GDN_CHUNK:(c + 1) * GDN_CHUNK] for h in range(GDN_HEADS)], axis=1)
            gcr_ref[d, c] = jnp.concatenate([row_c, pad_rows], axis=0)


def _inproj(x2d, lng, lnb, ws, conv_w8, alog_rows, dtb_rows, apply_ln, seq):
    bt, dm = x2d.shape
    tm = TM_PROJ
    grid = (bt // tm,)
    row = lambda i: (i, 0)
    hb = tm // CONV_HALO
    last_hb = bt // CONV_HALO - 1
    in_specs = [
        pl.BlockSpec((tm, dm), row),
        pl.BlockSpec((CONV_HALO, dm), lambda i: (jnp.maximum(i * hb - 1, 0), 0)),
        pl.BlockSpec((CONV_HALO, dm), lambda i: (jnp.minimum((i + 1) * hb, last_hb), 0)),
        _const_spec((1, dm)), _const_spec((1, dm)),
        _const_spec((dm, C_QKV)), _const_spec((dm, C_Z)), _const_spec((dm, 2 * LANES)), _const_spec((dm, C_NA)),
        _const_spec((8, C_QKV)),
        _const_spec((2, LANES)), _const_spec((2, LANES)),
    ]
    out_shape = [
        jax.ShapeDtypeStruct((bt, C_QKV), BF16),
        jax.ShapeDtypeStruct((bt, C_Z), BF16),
        jax.ShapeDtypeStruct((2, bt, LANES), F32),
        jax.ShapeDtypeStruct((2, bt // GDN_CHUNK, 8, GDN_HEADS * GDN_CHUNK), F32),
        jax.ShapeDtypeStruct((bt, C_NA), BF16),
    ]
    out_specs = [
        pl.BlockSpec((tm, C_QKV), row),
        pl.BlockSpec((tm, C_Z), row),
        pl.BlockSpec((2, tm, LANES), lambda i: (0, i, 0)),
        pl.BlockSpec((2, tm // GDN_CHUNK, 8, GDN_HEADS * GDN_CHUNK), lambda i: (0, i, 0, 0)),
        pl.BlockSpec((tm, C_NA), row),
    ]
    if apply_ln:
        out_shape = [jax.ShapeDtypeStruct((bt, dm), F32)] + out_shape
        out_specs = [pl.BlockSpec((tm, dm), row)] + out_specs
    return pl.pallas_call(
        functools.partial(_inproj_kernel, apply_ln=apply_ln, tiles_per_seq=seq // tm),
        grid=grid, in_specs=in_specs, out_specs=out_specs, out_shape=out_shape,
        scratch_shapes=[pltpu.VMEM((tm + 2 * CONV_HALO, C_QKV), F32)],
        compiler_params=_cparams(("parallel",)),
        name="inproj_ln" if apply_ln else "inproj",
    )(x2d, x2d, x2d, lng, lnb, *ws, conv_w8, alog_rows, dtb_rows)


def _block_diag(x, nblk):
    w = x.shape[1] // nblk
    blk = lax.broadcasted_iota(jnp.int32, x.shape, 1) // w
    zero = jnp.zeros_like(x)
    return jnp.concatenate([jnp.where(blk == h, x, zero) for h in range(nblk)], axis=0)


def _lane_blocks(cols, width):
    r, n = cols.shape
    return jnp.concatenate([jnp.broadcast_to(cols[:, j:j + 1], (r, width)) for j in range(n)], axis=1)


def _unit_tri_inverses(mats, nblk):
    c = mats[0].shape[0]
    ri = lax.broadcasted_iota(jnp.int32, (c, nblk * c), 0)
    ci = lax.broadcasted_iota(jnp.int32, (c, nblk * c), 1) % c
    eye = (ri == ci).astype(F32)
    ps = [-a for a in mats]
    xs = [eye + p for p in ps]
    pbs = [p.astype(BF16) for p in ps]
    ps = [jnp.dot(pb, _block_diag(pb, nblk), preferred_element_type=F32) for pb in pbs]
    m = 2
    while 2 * m < c:
        pbs = [p.astype(BF16) for p in ps]
        prods = [jnp.dot(jnp.concatenate([x.astype(BF16), pb], axis=0), _block_diag(pb, nblk),
                         preferred_element_type=F32) for x, pb in zip(xs, pbs)]
        xs = [x + prod[:c] for x, prod in zip(xs, prods)]
        ps = [prod[c:] for prod in prods]
        m *= 2
    return [x + jnp.dot(x.astype(BF16), _block_diag(p.astype(BF16), nblk), preferred_element_type=F32)
            for x, p in zip(xs, ps)]


def _gdn_scan_kernel(qf_ref, kf_ref, vf_ref, gf_ref, grf_ref, qb_ref, kb_ref, vb_ref, gb_ref, grb_ref,
                     of_ref, ob_ref, s_ref, stk_ref, wq_ref, vbeta_ref, *, tt, nb):
    i = pl.program_id(1)
    nc = tt // GDN_CHUNK
    C = GDN_CHUNK
    dh = GDN_HEAD_DIM
    H = GDN_HEADS

    @pl.when(i == 0)
    def _():
        s_ref[...] = jnp.zeros_like(s_ref)

    ri = lax.broadcasted_iota(jnp.int32, (C, H * C), 0)
    ci = lax.broadcasted_iota(jnp.int32, (C, H * C), 1) % C
    mask_incl = (ri >= ci, ri <= ci)
    mask_strict = (ri > ci, ri < ci)
    dirs = ((qf_ref, kf_ref, vf_ref, gf_ref, grf_ref, of_ref), (qb_ref, kb_ref, vb_ref, gb_ref, grb_ref, ob_ref))
    streams = [(b, d) for b in range(nb) for d in range(2)]
    half_lo = lax.broadcasted_iota(jnp.int32, (C, dh), 1) < C
    nt_dims = (((1,), (1,)), ((), ()))
    tn_dims = (((0,), (0,)), ((), ()))

    def chunk_end_decay(gt, d):
        return gt[C - 1:C, :] if d == 0 else gt[0:1, :]

    def prep_group(cg, carry):
        items = [(cg * GDN_PREP_CHUNKS + j, u) for j in range(GDN_PREP_CHUNKS) for u in range(len(streams))]
        lhs_l, bdk_l, gam_l, kd_l = [], [], [], []
        for c, u in items:
            b, d = streams[u]
            q_ref, k_ref, v_ref, g_ref, gr_ref, _ = dirs[d]
            r0 = pl.multiple_of(c * C, C)
            gt = g_ref[0, b, pl.ds(r0, C), :]
            gr = gr_ref[0, b, c]
            q = q_ref[b, pl.ds(r0, C), :]
            k = k_ref[b, pl.ds(r0, C), :]
            v = v_ref[b, pl.ds(r0, C), :].astype(F32)
            kf = k.astype(F32)
            gc_n = _lane_blocks(gt[:, 0:H], dh)
            beta = _lane_blocks(gt[:, H:2 * H], dh)
            gend_n = _lane_blocks(chunk_end_decay(gt, d)[:, 0:H], dh)
            e_n = jnp.exp(gc_n)
            kb = kf * beta
            kbe = (kb * e_n).astype(BF16)
            qd = (q.astype(F32) * e_n).astype(BF16)
            kdec = (kf * jnp.exp(gend_n - gc_n)).astype(BF16)
            gc_c = jnp.concatenate(
                [jnp.where(half_lo, gc_n[:, (2 * p) * dh:(2 * p + 1) * dh], gc_n[:, (2 * p + 1) * dh:(2 * p + 2) * dh])
                 for p in range(H // 2)], axis=1)
            for h in range(H):
                cols = slice(h * dh, (h + 1) * dh)
                stk_ref[c, u * H + h] = jnp.concatenate([kbe[:, cols], qd[:, cols]], axis=0)
            vbeta_ref[c, u] = v * beta
            lhs_l.append(jnp.concatenate([kb.astype(BF16), q], axis=0))
            bdk_l.append(_block_diag(k, H))
            kd_l.append(jnp.concatenate([kdec[:, h * dh:(h + 1) * dh] for h in range(H)], axis=0))
            gam_l.append(jnp.where(mask_incl[d], jnp.exp(gc_c - gr[0:1, :]), 0.0))
        kq_l = [lax.dot_general(lhs, bdk, nt_dims, preferred_element_type=F32)
                for lhs, bdk in zip(lhs_l, bdk_l)]
        a_l = [jnp.where(mask_strict[streams[u][1]], kq[:C] * gam, 0.0)
               for kq, gam, (_, u) in zip(kq_l, gam_l, items)]
        qk_l = [(kq[C:] * gam).astype(BF16) for kq, gam in zip(kq_l, gam_l)]
        bdt_l = [_block_diag(t.astype(BF16), H) for t in _unit_tri_inverses(a_l, H)]
        wc_l = [lax.dot_general(kd, bdt, tn_dims, preferred_element_type=F32) for kd, bdt in zip(kd_l, bdt_l)]
        qt_l = [jnp.dot(qk, bdt, preferred_element_type=F32) for qk, bdt in zip(qk_l, bdt_l)]
        for (c, u), wc, qt in zip(items, wc_l, qt_l):
            wq_ref[c, u] = jnp.concatenate([wc, qt], axis=0).astype(BF16)
        return carry

    lax.fori_loop(0, nc // GDN_PREP_CHUNKS, prep_group, 0)

    def scan_step(step, carry):
        cs = (step, nc - 1 - step)
        egs, x1_l = [], []
        for u, (b, d) in enumerate(streams):
            r0 = pl.multiple_of(cs[d] * C, C)
            egs.append(jnp.exp(chunk_end_decay(dirs[d][3][0, b, pl.ds(r0, C), :], d)))
        s_l = [[s_ref[u * H + h] for h in range(H)] for u in range(len(streams))]
        for u, (b, d) in enumerate(streams):
            x1_l.append([jnp.dot(stk_ref[cs[d], u * H + h], s_l[u][h].astype(BF16), preferred_element_type=F32)
                         for h in range(H)])
        res_l = [(vbeta_ref[cs[d], u] - jnp.concatenate([x1[:C] for x1 in x1_l[u]], axis=1)).astype(BF16)
                 for u, (b, d) in enumerate(streams)]
        z_l = [jnp.dot(wq_ref[cs[d], u], _block_diag(res, H), preferred_element_type=F32)
               for (u, (b, d)), res in zip(enumerate(streams), res_l)]
        for u, (b, d) in enumerate(streams):
            z = z_l[u]
            for h in range(H):
                s_ref[u * H + h] = s_l[u][h] * egs[u][:, h:h + 1] + z[:dh, h * dh:(h + 1) * dh]
            r0 = pl.multiple_of(cs[d] * C, C)
            dirs[d][5][b, pl.ds(r0, C), :] = jnp.concatenate([x1[C:] for x1 in x1_l[u]], axis=1) + z[dh:]
        return carry

    lax.fori_loop(0, nc, scan_step, 0)


def _gdn_scan(qkvp, gates, gcr):
    b, seq, _ = qkvp.shape
    tt = TT_SCAN
    nb = GDN_BATCH_PER_STEP if b % GDN_BATCH_PER_STEP == 0 else 1
    nt = seq // tt
    nc = tt // GDN_CHUNK
    C = GDN_CHUNK
    dh = GDN_HEAD_DIM
    H = GDN_HEADS
    nstream = 2 * nb

    def dir_specs(d):
        tmap = (lambda i: i) if d == 0 else (lambda i: nt - 1 - i)
        return [
            pl.BlockSpec((nb, tt, GDN_WIDTH), lambda bi, i: (bi, tmap(i), 0)),
            pl.BlockSpec((nb, tt, GDN_WIDTH), lambda bi, i: (bi, tmap(i), 1)),
            pl.BlockSpec((nb, tt, GDN_WIDTH), lambda bi, i: (bi, tmap(i), 2)),
            pl.BlockSpec((1, nb, tt, LANES), lambda bi, i: (d, bi, tmap(i), 0)),
            pl.BlockSpec((1, nb, nc, 8, H * C), lambda bi, i: (d, bi, tmap(i), 0, 0)),
        ]

    out_sds = jax.ShapeDtypeStruct((b, seq, GDN_WIDTH), F32)
    return pl.pallas_call(
        functools.partial(_gdn_scan_kernel, tt=tt, nb=nb),
        grid=(b // nb, nt),
        in_specs=dir_specs(0) + dir_specs(1),
        out_specs=[pl.BlockSpec((nb, tt, GDN_WIDTH), lambda bi, i: (bi, i, 0)),
                   pl.BlockSpec((nb, tt, GDN_WIDTH), lambda bi, i: (bi, nt - 1 - i, 0))],
        out_shape=[out_sds, out_sds],
        scratch_shapes=[
            pltpu.VMEM((nstream * H, dh, dh), F32),
            pltpu.VMEM((nc, nstream * H, 2 * C, dh), BF16),
            pltpu.VMEM((nc, nstream, dh + C, H * C), BF16),
            pltpu.VMEM((nc, nstream, C, H * dh), F32),
        ],
        compiler_params=_cparams(("parallel", "arbitrary")),
        name="gdn_scan",
    )(qkvp, qkvp, qkvp, gates, gcr, qkvp, qkvp, qkvp, gates, gcr)


def _na_kernel(q_ref, k_ref, v_ref, bm_ref, g_ref, o_ref, *, rows):
    W = GRID_W
    band = NA_WIN_R * W
    hd = NA_HEAD_DIM
    lane_q = lax.broadcasted_iota(jnp.int32, (W, LANES), 1)
    first = lane_q < hd
    scale = jnp.asarray(hd ** -0.5, BF16)

    nt_dims = (((1,), (1,)), ((), ()))

    def row_group(gi, carry):
        rs = [gi * NA_ROWS_PER_STEP + j for j in range(NA_ROWS_PER_STEP)]
        r0s = [jnp.clip(r - NA_WIN_R // 2, 0, rows - NA_WIN_R) for r in rs]
        qs_l, kb_l, vb_l = [], [], []
        for r, r0 in zip(rs, r0s):
            q2 = q_ref[0, pl.ds(pl.multiple_of(r * W, W), W), :] * scale
            zero = jnp.zeros_like(q2)
            qs_l.append(jnp.concatenate([jnp.where(first, q2, zero), jnp.where(first, zero, q2)], axis=0))
            kb_l.append(k_ref[0, pl.ds(pl.multiple_of(r0 * W, W), band), :])
            vb_l.append(v_ref[0, pl.ds(pl.multiple_of(r0 * W, W), band), :])
        s_l = [lax.dot_general(qs, kb, nt_dims, preferred_element_type=F32) for qs, kb in zip(qs_l, kb_l)]
        s_l = [s + bm_ref[0, r - r0] for s, r, r0 in zip(s_l, rs, r0s)]
        m_l = [jnp.max(s, axis=-1, keepdims=True) for s in s_l]
        p_l = [jnp.exp(s - m) for s, m in zip(s_l, m_l)]
        l_l = [jnp.sum(p, axis=-1, keepdims=True) for p in p_l]
        pb_l = [p.astype(BF16) for p in p_l]
        o0_l = [jnp.dot(pb[:W], vb, preferred_element_type=F32) for pb, vb in zip(pb_l, vb_l)]
        o1_l = [jnp.dot(pb[W:], vb, preferred_element_type=F32) for pb, vb in zip(pb_l, vb_l)]
        for r, o0, o1, l in zip(rs, o0_l, o1_l, l_l):
            o = jnp.where(first, o0 / l[:W], o1 / l[W:])
            sq = o * o
            ms0 = jnp.sum(jnp.where(first, sq, 0.0), axis=-1, keepdims=True)
            ms1 = jnp.sum(jnp.where(first, 0.0, sq), axis=-1, keepdims=True)
            ms = jnp.where(first, ms0, ms1) * (1.0 / hd)
            o_ref[0, pl.ds(pl.multiple_of(r * W, W), W), :] = (o * lax.rsqrt(ms + RMS_EPS) * g_ref[...]).astype(BF16)
        return carry

    lax.fori_loop(0, rows // NA_ROWS_PER_STEP, row_group, 0)


def _na(h_na, bias_tab, g_row):
    b, seq, _ = h_na.shape
    rows = seq // GRID_W
    assert rows >= NA_WIN_R
    npair = NA_WIDTH // LANES
    band = NA_WIN_R * GRID_W
    return pl.pallas_call(
        functools.partial(_na_kernel, rows=rows),
        grid=(b, npair),
        in_specs=[
            pl.BlockSpec((1, seq, LANES), lambda bi, p: (bi, 0, p)),
            pl.BlockSpec((1, seq, LANES), lambda bi, p: (bi, 0, npair + p)),
            pl.BlockSpec((1, seq, LANES), lambda bi, p: (bi, 0, 2 * npair + p)),
            pl.BlockSpec((1, NA_WIN_R, 2 * GRID_W, band), lambda bi, p: (p, 0, 0, 0)),
            pl.BlockSpec((1, LANES), lambda bi, p: (0, 0)),
        ],
        out_specs=pl.BlockSpec((1, seq, LANES), lambda bi, p: (bi, 0, p)),
        out_shape=jax.ShapeDtypeStruct((b, seq, NA_WIDTH), BF16),
        compiler_params=_cparams(("parallel", "parallel")),
        name="natten",
    )(h_na, h_na, h_na, bias_tab, g_row)


def _na_bias_table(rpb_l):
    W = GRID_W
    vi = np.arange(NA_WIN_R)
    kr = np.arange(NA_WIN_R)
    dr = kr[None, :] - vi[:, None] + NA_WIN_R - 1
    qc = np.arange(W)
    kc = np.arange(W)
    win_start = np.clip(qc - NA_WIN_C // 2, 0, W - NA_WIN_C)
    in_win = (kc[None, :] >= win_start[:, None]) & (kc[None, :] < win_start[:, None] + NA_WIN_C)
    dc = kc[None, :] - qc[:, None] + NA_WIN_C - 1
    rsel = (dr[:, :, None] == np.arange(2 * NA_WIN_R - 1)).astype(np.float32)
    csel = ((dc[:, :, None] == np.arange(2 * NA_WIN_C - 1)) & in_win[:, :, None]).astype(np.float32)
    tab = jnp.einsum("hab,vka,qcb->hvqkc", rpb_l.astype(F32), rsel, csel,
                     precision=lax.Precision.HIGHEST)
    tab = jnp.where(jnp.asarray(in_win)[:, None, :], tab, -jnp.inf)
    tab = tab.reshape(NA_HEADS // 2, 2, NA_WIN_R, W, NA_WIN_R * W)
    tab = jnp.transpose(tab, (0, 2, 1, 3, 4))
    return tab.reshape(NA_HEADS // 2, NA_WIN_R, 2 * W, NA_WIN_R * W)


def _mix_ffn_kernel(x_ref, of_ref, ob_ref, z_ref, ona_ref, gg_ref, wo_ref, l1g_ref, l1b_ref,
                    w1_ref, b1_ref, w2_ref, b2_ref, l2g_ref, l2b_ref, out_ref):
    dh = GDN_HEAD_DIM
    tm = x_ref.shape[0]
    d_ff = w1_ref.shape[1]
    rows = [slice(s * (tm // FFN_SUBTILES), (s + 1) * (tm // FFN_SUBTILES)) for s in range(FFN_SUBTILES)]

    def gated(rs):
        o = of_ref[rs, :] + ob_ref[rs, :]
        z = z_ref[rs, :].astype(F32)
        gate = z * _sigmoid(z)
        parts = []
        for h in range(GDN_HEADS):
            cols = slice(h * dh, (h + 1) * dh)
            oh = o[:, cols]
            ms = jnp.mean(oh * oh, axis=-1, keepdims=True)
            parts.append((oh * lax.rsqrt(ms + RMS_EPS) * gg_ref[:, cols] * gate[:, cols]).astype(BF16))
        return jnp.concatenate(parts, axis=-1)

    og_l = [gated(rs) for rs in rows]
    mix_l = [jnp.dot(og, wo_ref[0:GDN_WIDTH, :], preferred_element_type=F32)
             + jnp.dot(ona_ref[rs, :], wo_ref[GDN_WIDTH:, :], preferred_element_type=F32)
             for og, rs in zip(og_l, rows)]
    x1_l = [_layer_norm(DEEPNORM_ALPHA * x_ref[rs, :] + mix, l1g_ref[...], l1b_ref[...])
            for mix, rs in zip(mix_l, rows)]
    x1b_l = [x1.astype(BF16) for x1 in x1_l]
    acc_l = [jnp.zeros(x1.shape, F32) for x1 in x1_l]
    for f in range(d_ff // FF_CHUNK):
        fs = slice(f * FF_CHUNK, (f + 1) * FF_CHUNK)
        hf_l = [jnp.dot(x1b, w1_ref[:, fs], preferred_element_type=F32) + b1_ref[:, fs] for x1b in x1b_l]
        hf_l = [jnp.square(jnp.maximum(hf, 0.0)).astype(BF16) for hf in hf_l]
        acc_l = [acc + jnp.dot(hf, w2_ref[fs, :], preferred_element_type=F32) for acc, hf in zip(acc_l, hf_l)]
    for rs, x1, acc in zip(rows, x1_l, acc_l):
        y = DEEPNORM_ALPHA * x1 + (acc + b2_ref[...])
        out_ref[rs, :] = _layer_norm(y, l2g_ref[...], l2b_ref[...])


def _mix_ffn(x2d, o_f, o_b, z, ona, gg, wo, l1g, l1b, w1, b1, w2, b2, l2g, l2b):
    bt, dm = x2d.shape
    d_ff = w1.shape[1]
    tm = TM_FFN
    row = lambda i: (i, 0)
    return pl.pallas_call(
        _mix_ffn_kernel,
        grid=(bt // tm,),
        in_specs=[
            pl.BlockSpec((tm, dm), row),
            pl.BlockSpec((tm, GDN_WIDTH), row),
            pl.BlockSpec((tm, GDN_WIDTH), row),
            pl.BlockSpec((tm, GDN_WIDTH), row),
            pl.BlockSpec((tm, NA_WIDTH), row),
            _const_spec((1, GDN_WIDTH)),
            _const_spec((dm, dm)),
            _const_spec((1, dm)), _const_spec((1, dm)),
            _const_spec((dm, d_ff)), _const_spec((1, d_ff)),
            _const_spec((d_ff, dm)), _const_spec((1, dm)),
            _const_spec((1, dm)), _const_spec((1, dm)),
        ],
        out_specs=pl.BlockSpec((tm, dm), row),
        out_shape=jax.ShapeDtypeStruct((bt, dm), F32),
        compiler_params=_cparams(("parallel",)),
        name="mix_ffn",
    )(x2d, o_f, o_b, z, ona, gg, wo, l1g, l1b, w1, b1, w2, b2, l2g, l2b)


def _split_w_in(w_l):
    dm = w_l.shape[0]
    h = GDN_HEADS
    a0 = 4 * GDN_WIDTH
    b0 = a0 + 2 * h
    na0 = b0 + 2 * h
    pad = jnp.zeros((dm, LANES - 2 * h), w_l.dtype)
    gates = jnp.concatenate([w_l[:, a0:a0 + h], w_l[:, b0:b0 + h], pad,
                             w_l[:, a0 + h:a0 + 2 * h], w_l[:, b0 + h:b0 + 2 * h], pad], axis=1)
    return tuple(w.astype(BF16) for w in (w_l[:, :C_QKV], w_l[:, C_QKV:a0], gates, w_l[:, na0:]))


def _gate_rows(p):
    return jnp.pad(p.astype(F32), ((0, 0), (0, LANES - p.shape[1])))


def kernel(x, ln_in_g, ln_in_b, w_in, conv_w, a_log, dt_bias, gdn_norm_g, rpb, na_norm_g, w_out,
           ln1_g, ln1_b, w1, b1, w2, b2, ln2_g, ln2_b):
    B, T, dm = x.shape
    bt = B * T
    nchunks = T // GDN_CHUNK
    row = lambda v: v.reshape(1, -1).astype(F32)
    xs = x.reshape(bt, dm)
    for l in range(DEPTH):
        w_l = _split_w_in(w_in[l])
        conv8 = jnp.pad(conv_w[l].astype(F32), ((0, 8 - CONV_WIDTH), (0, 0)))
        outs = _inproj(xs, row(ln_in_g), row(ln_in_b), w_l, conv8, _gate_rows(a_log[l]), _gate_rows(dt_bias[l]),
                       apply_ln=(l == 0), seq=T)
        if l == 0:
            xs, qkvp, z, gates, gcr, h_na = outs
        else:
            qkvp, z, gates, gcr, h_na = outs
        qkvp = qkvp.reshape(B, T, C_QKV)
        gates4 = gates.reshape(2, B, T, LANES)
        gcr = gcr.reshape(2, B, nchunks, 8, GDN_HEADS * GDN_CHUNK)
        o_f, o_b = _gdn_scan(qkvp, gates4, gcr)
        ona = _na(h_na.reshape(B, T, C_NA), _na_bias_table(rpb[l]),
                  jnp.tile(na_norm_g[l].astype(F32), LANES // NA_HEAD_DIM).reshape(1, LANES))
        xs = _mix_ffn(xs, o_f.reshape(bt, GDN_WIDTH), o_b.reshape(bt, GDN_WIDTH), z, ona.reshape(bt, NA_WIDTH),
                      jnp.tile(gdn_norm_g[l].astype(F32), GDN_HEADS).reshape(1, GDN_WIDTH),
                      w_out[l].astype(BF16), row(ln1_g[l]), row(ln1_b[l]),
                      w1[l].astype(BF16), row(b1[l]), w2[l].astype(BF16), row(b2[l]),
                      row(ln2_g[l]), row(ln2_b[l]))
    return xs.reshape(B, T, dm)
```

```python
import functools
import math

import jax
import jax.numpy as jnp
import numpy as np
from jax import lax
from jax.experimental import pallas as pl
from jax.experimental.pallas import tpu as pltpu

F32 = jnp.float32
BF16 = jnp.bfloat16

GRID_W = 64
GDN_HEAD_DIM = 128
GDN_HEADS = 4
GDN_WIDTH = GDN_HEADS * GDN_HEAD_DIM
NA_HEAD_DIM = 64
NA_HEADS = 8
NA_WIDTH = NA_HEADS * NA_HEAD_DIM
CONV_WIDTH = 5
GDN_CHUNK = 64
NA_WIN_R = 8
NA_WIN_C = 16
DEPTH = 2
DEEPNORM_ALPHA = (2 * DEPTH) ** 0.25
LN_EPS = 1e-5
RMS_EPS = 1e-6

LANES = 128
VMEM_LIMIT = 56 * 1024 * 1024

TM_PROJ = 512
TM_FFN = 512
TT_SCAN = 256
FF_CHUNK = 1024
NA_ROWS_PER_STEP = 8
FFN_SUBTILES = 2
GDN_BATCH_PER_STEP = 4
GDN_PREP_CHUNKS = 2

C_QKV = 3 * GDN_WIDTH
C_Z = GDN_WIDTH
C_NA = 3 * NA_WIDTH


def _cparams(sem):
    return pltpu.CompilerParams(dimension_semantics=sem, vmem_limit_bytes=VMEM_LIMIT)


def _const_spec(shape):
    nd = len(shape)
    return pl.BlockSpec(shape, lambda *_: (0,) * nd, pipeline_mode=pl.Buffered(1))


def _layer_norm(y, g, b):
    mu = jnp.mean(y, axis=-1, keepdims=True)
    yc = y - mu
    var = jnp.mean(yc * yc, axis=-1, keepdims=True)
    return yc * lax.rsqrt(var + LN_EPS) * g + b


def _sigmoid(x):
    return 1.0 / (1.0 + jnp.exp(-x))


def _chunk_cumsum(g, tpos, reverse):
    n = g.shape[0]
    s = 1
    while s < GDN_CHUNK:
        if reverse:
            g = g + jnp.where(tpos < GDN_CHUNK - s, pltpu.roll(g, n - s, 0), 0.0)
        else:
            g = g + jnp.where(tpos >= s, pltpu.roll(g, s, 0), 0.0)
        s *= 2
    return g


CONV_HALO = 8


def _inproj_kernel(x_ref, xp_ref, xn_ref, lng_ref, lnb_ref, wq_ref, wz_ref, wg_ref, wna_ref, cw_ref, alog_ref, dtb_ref,
                   *refs, apply_ln, tiles_per_seq):
    if apply_ln:
        xo_ref, qkv_ref, z_ref, gates_ref, gcr_ref, na_ref, hq_ref = refs
    else:
        qkv_ref, z_ref, gates_ref, gcr_ref, na_ref, hq_ref = refs
    i = pl.program_id(0)
    tm = x_ref.shape[0]
    halo = CONV_HALO
    x, x_prev, x_next = x_ref[...], xp_ref[...], xn_ref[...]
    if apply_ln:
        x = _layer_norm(x, lng_ref[...], lnb_ref[...])
        x_prev = _layer_norm(x_prev, lng_ref[...], lnb_ref[...])
        x_next = _layer_norm(x_next, lng_ref[...], lnb_ref[...])
        xo_ref[...] = x
    xb = x.astype(BF16)
    x_ext = jnp.concatenate([x_prev, x, x_next], axis=0).astype(BF16)

    pos = i % tiles_per_seq
    keep_prev = jnp.where(pos == 0, 0.0, 1.0).astype(F32)
    keep_next = jnp.where(pos == tiles_per_seq - 1, 0.0, 1.0).astype(F32)
    hq = jnp.dot(x_ext, wq_ref[...], preferred_element_type=F32)
    hq_ref[0:halo, :] = hq[0:halo] * keep_prev
    hq_ref[halo:tm + halo, :] = hq[halo:tm + halo]
    hq_ref[tm + halo:tm + 2 * halo, :] = hq[tm + halo:tm + 2 * halo] * keep_next
    z_ref[...] = jnp.dot(xb, wz_ref[...], preferred_element_type=F32).astype(BF16)
    na_ref[...] = jnp.dot(xb, wna_ref[...], preferred_element_type=F32).astype(BF16)
    hab2 = jnp.dot(xb, wg_ref[...], preferred_element_type=F32)

    first_tap = halo - CONV_WIDTH // 2
    for j in range(C_QKV // LANES):
        cols = slice(j * LANES, (j + 1) * LANES)
        y = hq_ref[first_tap:first_tap + tm, cols] * cw_ref[0:1, cols]
        for tap in range(1, CONV_WIDTH):
            y = y + hq_ref[first_tap + tap:first_tap + tap + tm, cols] * cw_ref[tap:tap + 1, cols]
        y = y * _sigmoid(y)
        if j < 2 * GDN_HEADS:
            fac = lax.rsqrt(jnp.sum(y * y, axis=-1, keepdims=True) + RMS_EPS)
            if j < GDN_HEADS:
                fac = fac * (GDN_HEAD_DIM ** -0.5)
            y = y * fac
        qkv_ref[:, cols] = y.astype(BF16)

    tpos = lax.broadcasted_iota(jnp.int32, (tm, LANES), 0) & (GDN_CHUNK - 1)
    lane = lax.broadcasted_iota(jnp.int32, (tm, LANES), 1)
    for d in range(2):
        hab = hab2[:, d * LANES:(d + 1) * LANES]
        sp_in = hab + dtb_ref[d:d + 1, :]
        softplus = jnp.maximum(sp_in, 0.0) + jnp.log1p(jnp.exp(-jnp.abs(sp_in)))
        g = -jnp.exp(alog_ref[d:d + 1, :]) * softplus
        g = jnp.where(lane < GDN_HEADS, g, 0.0)
        gc = _chunk_cumsum(g, tpos, reverse=(d == 1))
        gates_ref[d] = jnp.where(lane < GDN_HEADS, gc, _sigmoid(hab))
        gct = gc.T
        pad_rows = jnp.zeros((7, GDN_HEADS * GDN_CHUNK), F32)
        for c in range(tm // GDN_CHUNK):
            row_c = jnp.concatenate([gct[h:h + 1, c * GDN_CHUNK:(c + 1) * GDN_CHUNK] for h in range(GDN_HEADS)], axis=1)
            gcr_ref[d, c] = jnp.concatenate([row_c, pad_rows], axis=0)


def _inproj(x2d, lng, lnb, ws, conv_w8, alog_rows, dtb_rows, apply_ln, seq):
    bt, dm = x2d.shape
    tm = TM_PROJ
    grid = (bt // tm,)
    row = lambda i: (i, 0)
    hb = tm // CONV_HALO
    last_hb = bt // CONV_HALO - 1
    in_specs = [
        pl.BlockSpec((tm, dm), row),
        pl.BlockSpec((CONV_HALO, dm), lambda i: (jnp.maximum(i * hb - 1, 0), 0)),
        pl.BlockSpec((CONV_HALO, dm), lambda i: (jnp.minimum((i + 1) * hb, last_hb), 0)),
        _const_spec((1, dm)), _const_spec((1, dm)),
        _const_spec((dm, C_QKV)), _const_spec((dm, C_Z)), _const_spec((dm, 2 * LANES)), _const_spec((dm, C_NA)),
        _const_spec((8, C_QKV)),
        _const_spec((2, LANES)), _const_spec((2, LANES)),
    ]
    out_shape = [
        jax.ShapeDtypeStruct((bt, C_QKV), BF16),
        jax.ShapeDtypeStruct((bt, C_Z), BF16),
        jax.ShapeDtypeStruct((2, bt, LANES), F32),
        jax.ShapeDtypeStruct((2, bt // GDN_CHUNK, 8, GDN_HEADS * GDN_CHUNK), F32),
        jax.ShapeDtypeStruct((bt, C_NA), BF16),
    ]
    out_specs = [
        pl.BlockSpec((tm, C_QKV), row),
        pl.BlockSpec((tm, C_Z), row),
        pl.BlockSpec((2, tm, LANES), lambda i: (0, i, 0)),
        pl.BlockSpec((2, tm // GDN_CHUNK, 8, GDN_HEADS * GDN_CHUNK), lambda i: (0, i, 0, 0)),
        pl.BlockSpec((tm, C_NA), row),
    ]
    if apply_ln:
        out_shape = [jax.ShapeDtypeStruct((bt, dm), F32)] + out_shape
        out_specs = [pl.BlockSpec((tm, dm), row)] + out_specs
    return pl.pallas_call(
        functools.partial(_inproj_kernel, apply_ln=apply_ln, tiles_per_seq=seq // tm),
        grid=grid, in_specs=in_specs, out_specs=out_specs, out_shape=out_shape,
        scratch_shapes=[pltpu.VMEM((tm + 2 * CONV_HALO, C_QKV), F32)],
        compiler_params=_cparams(("parallel",)),
        name="inproj_ln" if apply_ln else "inproj",
    )(x2d, x2d, x2d, lng, lnb, *ws, conv_w8, alog_rows, dtb_rows)


def _block_diag(x, nblk):
    w = x.shape[1] // nblk
    blk = lax.broadcasted_iota(jnp.int32, x.shape, 1) // w
    zero = jnp.zeros_like(x)
    return jnp.concatenate([jnp.where(blk == h, x, zero) for h in range(nblk)], axis=0)


def _lane_blocks(cols, width):
    r, n = cols.shape
    return jnp.concatenate([jnp.broadcast_to(cols[:, j:j + 1], (r, width)) for j in range(n)], axis=1)


def _unit_tri_inverses(mats, nblk):
    c = mats[0].shape[0]
    ri = lax.broadcasted_iota(jnp.int32, (c, nblk * c), 0)
    ci = lax.broadcasted_iota(jnp.int32, (c, nblk * c), 1) % c
    eye = (ri == ci).astype(F32)
    ps = [-a for a in mats]
    xs = [eye + p for p in ps]
    pbs = [p.astype(BF16) for p in ps]
    ps = [jnp.dot(pb, _block_diag(pb, nblk), preferred_element_type=F32) for pb in pbs]
    m = 2
    while 2 * m < c:
        pbs = [p.astype(BF16) for p in ps]
        prods = [jnp.dot(jnp.concatenate([x.astype(BF16), pb], axis=0), _block_diag(pb, nblk),
                         preferred_element_type=F32) for x, pb in zip(xs, pbs)]
        xs = [x + prod[:c] for x, prod in zip(xs, prods)]
        ps = [prod[c:] for prod in prods]
        m *= 2
    return [x + jnp.dot(x.astype(BF16), _block_diag(p.astype(BF16), nblk), preferred_element_type=F32)
            for x, p in zip(xs, ps)]


def _gdn_scan_kernel(qf_ref, kf_ref, vf_ref, gf_ref, grf_ref, qb_ref, kb_ref, vb_ref, gb_ref, grb_ref,
                     of_ref, ob_ref, s_ref, stk_ref, wq_ref, vbeta_ref, *, tt, nb):
    i = pl.program_id(1)
    nc = tt // GDN_CHUNK
    C = GDN_CHUNK
    dh = GDN_HEAD_DIM
    H = GDN_HEADS

    @pl.when(i == 0)
    def _():
        s_ref[...] = jnp.zeros_like(s_ref)

    ri = lax.broadcasted_iota(jnp.int32, (C, H * C), 0)
    ci = lax.broadcasted_iota(jnp.int32, (C, H * C), 1) % C
    mask_incl = (ri >= ci, ri <= ci)
    mask_strict = (ri > ci, ri < ci)
    dirs = ((qf_ref, kf_ref, vf_ref, gf_ref, grf_ref, of_ref), (qb_ref, kb_ref, vb_ref, gb_ref, grb_ref, ob_ref))
    streams = [(b, d) for b in range(nb) for d in range(2)]
    half_lo = lax.broadcasted_iota(jnp.int32, (C, dh), 1) < C
    nt_dims = (((1,), (1,)), ((), ()))
    tn_dims = (((0,), (0,)), ((), ()))

    def chunk_end_decay(gt, d):
        return gt[C - 1:C, :] if d == 0 else gt[0:1, :]

    def prep_group(cg, carry):
        items = [(cg * GDN_PREP_CHUNKS + j, u) for j in range(GDN_PREP_CHUNKS) for u in range(len(streams))]
        lhs_l, bdk_l, gam_l, kd_l = [], [], [], []
        for c, u in items:
            b, d = streams[u]
            q_ref, k_ref, v_ref, g_ref, gr_ref, _ = dirs[d]
            r0 = pl.multiple_of(c * C, C)
            gt = g_ref[0, b, pl.ds(r0, C), :]
            gr = gr_ref[0, b, c]
            q = q_ref[b, pl.ds(r0, C), :]
            k = k_ref[b, pl.ds(r0, C), :]
            v = v_ref[b, pl.ds(r0, C), :].astype(F32)
            kf = k.astype(F32)
            gc_n = _lane_blocks(gt[:, 0:H], dh)
            beta = _lane_blocks(gt[:, H:2 * H], dh)
            gend_n = _lane_blocks(chunk_end_decay(gt, d)[:, 0:H], dh)
            e_n = jnp.exp(gc_n)
            kb = kf * beta
            kbe = (kb * e_n).astype(BF16)
            qd = (q.astype(F32) * e_n).astype(BF16)
            kdec = (kf * jnp.exp(gend_n - gc_n)).astype(BF16)
            gc_c = jnp.concatenate(
                [jnp.where(half_lo, gc_n[:, (2 * p) * dh:(2 * p + 1) * dh], gc_n[:, (2 * p + 1) * dh:(2 * p + 2) * dh])
                 for p in range(H // 2)], axis=1)
            for h in range(H):
                cols = slice(h * dh, (h + 1) * dh)
                stk_ref[c, u * H + h] = jnp.concatenate([kbe[:, cols], qd[:, cols]], axis=0)
            vbeta_ref[c, u] = v * beta
            lhs_l.append(jnp.concatenate([kb.astype(BF16), q], axis=0))
            bdk_l.append(_block_diag(k, H))
            kd_l.append(jnp.concatenate([kdec[:, h * dh:(h + 1) * dh] for h in range(H)], axis=0))
            gam_l.append(jnp.where(mask_incl[d], jnp.exp(gc_c - gr[0:1, :]), 0.0))
        kq_l = [lax.dot_general(lhs, bdk, nt_dims, preferred_element_type=F32)
                for lhs, bdk in zip(lhs_l, bdk_l)]
        a_l = [jnp.where(mask_strict[streams[u][1]], kq[:C] * gam, 0.0)
               for kq, gam, (_, u) in zip(kq_l, gam_l, items)]
        qk_l = [(kq[C:] * gam).astype(BF16) for kq, gam in zip(kq_l, gam_l)]
        bdt_l = [_block_diag(t.astype(BF16), H) for t in _unit_tri_inverses(a_l, H)]
        wc_l = [lax.dot_general(kd, bdt, tn_dims, preferred_element_type=F32) for kd, bdt in zip(kd_l, bdt_l)]
        qt_l = [jnp.dot(qk, bdt, preferred_element_type=F32) for qk, bdt in zip(qk_l, bdt_l)]
        for (c, u), wc, qt in zip(items, wc_l, qt_l):
            wq_ref[c, u] = jnp.concatenate([wc, qt], axis=0).astype(BF16)
        return carry

    lax.fori_loop(0, nc // GDN_PREP_CHUNKS, prep_group, 0)

    def scan_step(step, carry):
        cs = (step, nc - 1 - step)
        egs, x1_l = [], []
        for u, (b, d) in enumerate(streams):
            r0 = pl.multiple_of(cs[d] * C, C)
            egs.append(jnp.exp(chunk_end_decay(dirs[d][3][0, b, pl.ds(r0, C), :], d)))
        s_l = [[s_ref[u * H + h] for h in range(H)] for u in range(len(streams))]
        for u, (b, d) in enumerate(streams):
            x1_l.append([jnp.dot(stk_ref[cs[d], u * H + h], s_l[u][h].astype(BF16), preferred_element_type=F32)
                         for h in range(H)])
        res_l = [(vbeta_ref[cs[d], u] - jnp.concatenate([x1[:C] for x1 in x1_l[u]], axis=1)).astype(BF16)
                 for u, (b, d) in enumerate(streams)]
        z_l = [jnp.dot(wq_ref[cs[d], u], _block_diag(res, H), preferred_element_type=F32)
               for (u, (b, d)), res in zip(enumerate(streams), res_l)]
        for u, (b, d) in enumerate(streams):
            z = z_l[u]
            for h in range(H):
                s_ref[u * H + h] = s_l[u][h] * egs[u][:, h:h + 1] + z[:dh, h * dh:(h + 1) * dh]
            r0 = pl.multiple_of(cs[d] * C, C)
            dirs[d][5][b, pl.ds(r0, C), :] = jnp.concatenate([x1[C:] for x1 in x1_l[u]], axis=1) + z[dh:]
        return carry

    lax.fori_loop(0, nc, scan_step, 0)


def _gdn_scan(qkvp, gates, gcr):
    b, seq, _ = qkvp.shape
    tt = TT_SCAN
    nb = GDN_BATCH_PER_STEP if b % GDN_BATCH_PER_STEP == 0 else 1
    nt = seq // tt
    nc = tt // GDN_CHUNK
    C = GDN_CHUNK
    dh = GDN_HEAD_DIM
    H = GDN_HEADS
    nstream = 2 * nb

    def dir_specs(d):
        tmap = (lambda i: i) if d == 0 else (lambda i: nt - 1 - i)
        return [
            pl.BlockSpec((nb, tt, GDN_WIDTH), lambda bi, i: (bi, tmap(i), 0)),
            pl.BlockSpec((nb, tt, GDN_WIDTH), lambda bi, i: (bi, tmap(i), 1)),
            pl.BlockSpec((nb, tt, GDN_WIDTH), lambda bi, i: (bi, tmap(i), 2)),
            pl.BlockSpec((1, nb, tt, LANES), lambda bi, i: (d, bi, tmap(i), 0)),
            pl.BlockSpec((1, nb, nc, 8, H * C), lambda bi, i: (d, bi, tmap(i), 0, 0)),
        ]

    out_sds = jax.ShapeDtypeStruct((b, seq, GDN_WIDTH), F32)
    return pl.pallas_call(
        functools.partial(_gdn_scan_kernel, tt=tt, nb=nb),
        grid=(b // nb, nt),
        in_specs=dir_specs(0) + dir_specs(1),
        out_specs=[pl.BlockSpec((nb, tt, GDN_WIDTH), lambda bi, i: (bi, i, 0)),
                   pl.BlockSpec((nb, tt, GDN_WIDTH), lambda bi, i: (bi, nt - 1 - i, 0))],
        out_shape=[out_sds, out_sds],
        scratch_shapes=[
            pltpu.VMEM((nstream * H, dh, dh), F32),
            pltpu.VMEM((nc, nstream * H, 2 * C, dh), BF16),
            pltpu.VMEM((nc, nstream, dh + C, H * C), BF16),
            pltpu.VMEM((nc, nstream, C, H * dh), F32),
        ],
        compiler_params=_cparams(("parallel", "arbitrary")),
        name="gdn_scan",
    )(qkvp, qkvp, qkvp, gates, gcr, qkvp, qkvp, qkvp, gates, gcr)


def _na_kernel(q_ref, k_ref, v_ref, bm_ref, g_ref, o_ref, *, rows):
    W = GRID_W
    band = NA_WIN_R * W
    hd = NA_HEAD_DIM
    lane_q = lax.broadcasted_iota(jnp.int32, (W, LANES), 1)
    first = lane_q < hd
    scale = jnp.asarray(hd ** -0.5, BF16)

    nt_dims = (((1,), (1,)), ((), ()))

    def row_group(gi, carry):
        rs = [gi * NA_ROWS_PER_STEP + j for j in range(NA_ROWS_PER_STEP)]
        r0s = [jnp.clip(r - NA_WIN_R // 2, 0, rows - NA_WIN_R) for r in rs]
        qs_l, kb_l, vb_l = [], [], []
        for r, r0 in zip(rs, r0s):
            q2 = q_ref[0, pl.ds(pl.multiple_of(r * W, W), W), :] * scale
            zero = jnp.zeros_like(q2)
            qs_l.append(jnp.concatenate([jnp.where(first, q2, zero), jnp.where(first, zero, q2)], axis=0))
            kb_l.append(k_ref[0, pl.ds(pl.multiple_of(r0 * W, W), band), :])
            vb_l.append(v_ref[0, pl.ds(pl.multiple_of(r0 * W, W), band), :])
        s_l = [lax.dot_general(qs, kb, nt_dims, preferred_element_type=F32) for qs, kb in zip(qs_l, kb_l)]
        s_l = [s + bm_ref[0, r - r0] for s, r, r0 in zip(s_l, rs, r0s)]
        m_l = [jnp.max(s, axis=-1, keepdims=True) for s in s_l]
        p_l = [jnp.exp(s - m) for s, m in zip(s_l, m_l)]
        l_l = [jnp.sum(p, axis=-1, keepdims=True) for p in p_l]
        pb_l = [p.astype(BF16) for p in p_l]
        o0_l = [jnp.dot(pb[:W], vb, preferred_element_type=F32) for pb, vb in zip(pb_l, vb_l)]
        o1_l = [jnp.dot(pb[W:], vb, preferred_element_type=F32) for pb, vb in zip(pb_l, vb_l)]
        for r, o0, o1, l in zip(rs, o0_l, o1_l, l_l):
            o = jnp.where(first, o0 / l[:W], o1 / l[W:])
            sq = o * o
            ms0 = jnp.sum(jnp.where(first, sq, 0.0), axis=-1, keepdims=True)
            ms1 = jnp.sum(jnp.where(first, 0.0, sq), axis=-1, keepdims=True)
            ms = jnp.where(first, ms0, ms1) * (1.0 / hd)
            o_ref[0, pl.ds(pl.multiple_of(r * W, W), W), :] = (o * lax.rsqrt(ms + RMS_EPS) * g_ref[...]).astype(BF16)
        return carry

    lax.fori_loop(0, rows // NA_ROWS_PER_STEP, row_group, 0)


def _na(h_na, bias_tab, g_row):
    b, seq, _ = h_na.shape
    rows = seq // GRID_W
    assert rows >= NA_WIN_R
    npair = NA_WIDTH // LANES
    band = NA_WIN_R * GRID_W
    return pl.pallas_call(
        functools.partial(_na_kernel, rows=rows),
        grid=(b, npair),
        in_specs=[
            pl.BlockSpec((1, seq, LANES), lambda bi, p: (bi, 0, p)),
            pl.BlockSpec((1, seq, LANES), lambda bi, p: (bi, 0, npair + p)),
            pl.BlockSpec((1, seq, LANES), lambda bi, p: (bi, 0, 2 * npair + p)),
            pl.BlockSpec((1, NA_WIN_R, 2 * GRID_W, band), lambda bi, p: (p, 0, 0, 0)),
            pl.BlockSpec((1, LANES), lambda bi, p: (0, 0)),
        ],
        out_specs=pl.BlockSpec((1, seq, LANES), lambda bi, p: (bi, 0, p)),
        out_shape=jax.ShapeDtypeStruct((b, seq, NA_WIDTH), BF16),
        compiler_params=_cparams(("parallel", "parallel")),
        name="natten",
    )(h_na, h_na, h_na, bias_tab, g_row)


def _na_bias_table(rpb_l):
    W = GRID_W
    vi = np.arange(NA_WIN_R)
    kr = np.arange(NA_WIN_R)
    dr = kr[None, :] - vi[:, None] + NA_WIN_R - 1
    qc = np.arange(W)
    kc = np.arange(W)
    win_start = np.clip(qc - NA_WIN_C // 2, 0, W - NA_WIN_C)
    in_win = (kc[None, :] >= win_start[:, None]) & (kc[None, :] < win_start[:, None] + NA_WIN_C)
    dc = kc[None, :] - qc[:, None] + NA_WIN_C - 1
    rsel = (dr[:, :, None] == np.arange(2 * NA_WIN_R - 1)).astype(np.float32)
    csel = ((dc[:, :, None] == np.arange(2 * NA_WIN_C - 1)) & in_win[:, :, None]).astype(np.float32)
    rpb_pairs = rpb_l.astype(F32).reshape(NA_HEADS // 2, 2, 2 * NA_WIN_R - 1, 2 * NA_WIN_C - 1)
    tab = jnp.einsum("pjab,vka,qcb->pvjqkc", rpb_pairs, rsel, csel,
                     precision=lax.Precision.HIGHEST)
    tab = jnp.where(jnp.asarray(in_win)[:, None, :], tab, -jnp.inf)
    return tab.reshape(NA_HEADS // 2, NA_WIN_R, 2 * W, NA_WIN_R * W)


def _mix_ffn_kernel(x_ref, of_ref, ob_ref, z_ref, ona_ref, gg_ref, wo_ref, l1g_ref, l1b_ref,
                    w1_ref, b1_ref, w2_ref, b2_ref, l2g_ref, l2b_ref, out_ref):
    dh = GDN_HEAD_DIM
    tm = x_ref.shape[0]
    d_ff = w1_ref.shape[1]
    rows = [slice(s * (tm // FFN_SUBTILES), (s + 1) * (tm // FFN_SUBTILES)) for s in range(FFN_SUBTILES)]

    def gated(rs):
        o = of_ref[rs, :] + ob_ref[rs, :]
        z = z_ref[rs, :].astype(F32)
        gate = z * _sigmoid(z)
        parts = []
        for h in range(GDN_HEADS):
            cols = slice(h * dh, (h + 1) * dh)
            oh = o[:, cols]
            ms = jnp.mean(oh * oh, axis=-1, keepdims=True)
            parts.append((oh * lax.rsqrt(ms + RMS_EPS) * gg_ref[:, cols] * gate[:, cols]).astype(BF16))
        return jnp.concatenate(parts, axis=-1)

    og_l = [gated(rs) for rs in rows]
    mix_l = [jnp.dot(og, wo_ref[0:GDN_WIDTH, :], preferred_element_type=F32)
             + jnp.dot(ona_ref[rs, :], wo_ref[GDN_WIDTH:, :], preferred_element_type=F32)
             for og, rs in zip(og_l, rows)]
    x1_l = [_layer_norm(DEEPNORM_ALPHA * x_ref[rs, :] + mix, l1g_ref[...], l1b_ref[...])
            for mix, rs in zip(mix_l, rows)]
    x1b_l = [x1.astype(BF16) for x1 in x1_l]
    acc_l = [jnp.zeros(x1.shape, F32) for x1 in x1_l]
    for f in range(d_ff // FF_CHUNK):
        fs = slice(f * FF_CHUNK, (f + 1) * FF_CHUNK)
        hf_l = [jnp.dot(x1b, w1_ref[:, fs], preferred_element_type=F32) + b1_ref[:, fs] for x1b in x1b_l]
        hf_l = [jnp.square(jnp.maximum(hf, 0.0)).astype(BF16) for hf in hf_l]
        acc_l = [acc + jnp.dot(hf, w2_ref[fs, :], preferred_element_type=F32) for acc, hf in zip(acc_l, hf_l)]
    for rs, x1, acc in zip(rows, x1_l, acc_l):
        y = DEEPNORM_ALPHA * x1 + (acc + b2_ref[...])
        out_ref[rs, :] = _layer_norm(y, l2g_ref[...], l2b_ref[...])


def _mix_ffn(x2d, o_f, o_b, z, ona, gg, wo, l1g, l1b, w1, b1, w2, b2, l2g, l2b):
    bt, dm = x2d.shape
    d_ff = w1.shape[1]
    tm = TM_FFN
    row = lambda i: (i, 0)
    return pl.pallas_call(
        _mix_ffn_kernel,
        grid=(bt // tm,),
        in_specs=[
            pl.BlockSpec((tm, dm), row),
            pl.BlockSpec((tm, GDN_WIDTH), row),
            pl.BlockSpec((tm, GDN_WIDTH), row),
            pl.BlockSpec((tm, GDN_WIDTH), row),
            pl.BlockSpec((tm, NA_WIDTH), row),
            _const_spec((1, GDN_WIDTH)),
            _const_spec((dm, dm)),
            _const_spec((1, dm)), _const_spec((1, dm)),
            _const_spec((dm, d_ff)), _const_spec((1, d_ff)),
            _const_spec((d_ff, dm)), _const_spec((1, dm)),
            _const_spec((1, dm)), _const_spec((1, dm)),
        ],
        out_specs=pl.BlockSpec((tm, dm), row),
        out_shape=jax.ShapeDtypeStruct((bt, dm), F32),
        compiler_params=_cparams(("parallel",)),
        name="mix_ffn",
    )(x2d, o_f, o_b, z, ona, gg, wo, l1g, l1b, w1, b1, w2, b2, l2g, l2b)


def _split_w_in(w_l):
    dm = w_l.shape[0]
    h = GDN_HEADS
    a0 = 4 * GDN_WIDTH
    b0 = a0 + 2 * h
    na0 = b0 + 2 * h
    pad = jnp.zeros((dm, LANES - 2 * h), w_l.dtype)
    gates = jnp.concatenate([w_l[:, a0:a0 + h], w_l[:, b0:b0 + h], pad,
                             w_l[:, a0 + h:a0 + 2 * h], w_l[:, b0 + h:b0 + 2 * h], pad], axis=1)
    return tuple(w.astype(BF16) for w in (w_l[:, :C_QKV], w_l[:, C_QKV:a0], gates, w_l[:, na0:]))


def _gate_rows(p):
    return jnp.pad(p.astype(F32), ((0, 0), (0, LANES - p.shape[1])))


def kernel(x, ln_in_g, ln_in_b, w_in, conv_w, a_log, dt_bias, gdn_norm_g, rpb, na_norm_g, w_out,
           ln1_g, ln1_b, w1, b1, w2, b2, ln2_g, ln2_b):
    B, T, dm = x.shape
    bt = B * T
    nchunks = T // GDN_CHUNK
    row = lambda v: v.reshape(1, -1).astype(F32)
    xs = x.reshape(bt, dm)
    for l in range(DEPTH):
        w_l = _split_w_in(w_in[l])
        conv8 = jnp.pad(conv_w[l].astype(F32), ((0, 8 - CONV_WIDTH), (0, 0)))
        outs = _inproj(xs, row(ln_in_g), row(ln_in_b), w_l, conv8, _gate_rows(a_log[l]), _gate_rows(dt_bias[l]),
                       apply_ln=(l == 0), seq=T)
        if l == 0:
            xs, qkvp, z, gates, gcr, h_na = outs
        else:
            qkvp, z, gates, gcr, h_na = outs
        qkvp = qkvp.reshape(B, T, C_QKV)
        gates4 = gates.reshape(2, B, T, LANES)
        gcr = gcr.reshape(2, B, nchunks, 8, GDN_HEADS * GDN_CHUNK)
        o_f, o_b = _gdn_scan(qkvp, gates4, gcr)
        ona = _na(h_na.reshape(B, T, C_NA), _na_bias_table(rpb[l]),
                  jnp.tile(na_norm_g[l].astype(F32), LANES // NA_HEAD_DIM).reshape(1, LANES))
        xs = _mix_ffn(xs, o_f.reshape(bt, GDN_WIDTH), o_b.reshape(bt, GDN_WIDTH), z, ona.reshape(bt, NA_WIDTH),
                      jnp.tile(gdn_norm_g[l].astype(F32), GDN_HEADS).reshape(1, GDN_WIDTH),
                      w_out[l].astype(BF16), row(ln1_g[l]), row(ln1_b[l]),
                      w1[l].astype(BF16), row(b1[l]), w2[l].astype(BF16), row(b2[l]),
                      row(ln2_g[l]), row(ln2_b[l]))
    return xs.reshape(B, T, dm)
```

```python
import functools
import math

import jax
import jax.numpy as jnp
import numpy as np
from jax import lax
from jax.experimental import pallas as pl
from jax.experimental.pallas import tpu as pltpu

F32 = jnp.float32
BF16 = jnp.bfloat16

GRID_W = 64
GDN_HEAD_DIM = 128
GDN_HEADS = 4
GDN_WIDTH = GDN_HEADS * GDN_HEAD_DIM
NA_HEAD_DIM = 64
NA_HEADS = 8
NA_WIDTH = NA_HEADS * NA_HEAD_DIM
CONV_WIDTH = 5
GDN_CHUNK = 64
NA_WIN_R = 8
NA_WIN_C = 16
DEPTH = 2
DEEPNORM_ALPHA = (2 * DEPTH) ** 0.25
LN_EPS = 1e-5
RMS_EPS = 1e-6

LANES = 128
VMEM_LIMIT = 56 * 1024 * 1024

TM_PROJ = 512
TM_FFN = 512
TT_SCAN = 256
FF_CHUNK = 1024
NA_ROWS_PER_STEP = 8
FFN_SUBTILES = 2
GDN_BATCH_PER_STEP = 4
GDN_PREP_CHUNKS = 2

C_QKV = 3 * GDN_WIDTH
C_Z = GDN_WIDTH
C_NA = 3 * NA_WIDTH


def _cparams(sem):
    return pltpu.CompilerParams(dimension_semantics=sem, vmem_limit_bytes=VMEM_LIMIT)


def _const_spec(shape):
    nd = len(shape)
    return pl.BlockSpec(shape, lambda *_: (0,) * nd, pipeline_mode=pl.Buffered(1))


def _layer_norm(y, g, b):
    mu = jnp.mean(y, axis=-1, keepdims=True)
    yc = y - mu
    var = jnp.mean(yc * yc, axis=-1, keepdims=True)
    return yc * lax.rsqrt(var + LN_EPS) * g + b


def _sigmoid(x):
    return 1.0 / (1.0 + jnp.exp(-x))


def _chunk_cumsum(g, tpos, reverse):
    n = g.shape[0]
    s = 1
    while s < GDN_CHUNK:
        if reverse:
            g = g + jnp.where(tpos < GDN_CHUNK - s, pltpu.roll(g, n - s, 0), 0.0)
        else:
            g = g + jnp.where(tpos >= s, pltpu.roll(g, s, 0), 0.0)
        s *= 2
    return g


CONV_HALO = 8


def _inproj_kernel(x_ref, xp_ref, xn_ref, lng_ref, lnb_ref, wq_ref, wz_ref, wg_ref, wna_ref, cw_ref, alog_ref, dtb_ref,
                   *refs, apply_ln, tiles_per_seq):
    if apply_ln:
        xo_ref, qkv_ref, z_ref, gates_ref, gcr_ref, na_ref, hq_ref = refs
    else:
        qkv_ref, z_ref, gates_ref, gcr_ref, na_ref, hq_ref = refs
    i = pl.program_id(0)
    tm = x_ref.shape[0]
    halo = CONV_HALO
    x, x_prev, x_next = x_ref[...], xp_ref[...], xn_ref[...]
    if apply_ln:
        x = _layer_norm(x, lng_ref[...], lnb_ref[...])
        x_prev = _layer_norm(x_prev, lng_ref[...], lnb_ref[...])
        x_next = _layer_norm(x_next, lng_ref[...], lnb_ref[...])
        xo_ref[...] = x
    xb = x.astype(BF16)
    x_ext = jnp.concatenate([x_prev, x, x_next], axis=0).astype(BF16)

    pos = i % tiles_per_seq
    keep_prev = jnp.where(pos == 0, 0.0, 1.0).astype(F32)
    keep_next = jnp.where(pos == tiles_per_seq - 1, 0.0, 1.0).astype(F32)
    hq = jnp.dot(x_ext, wq_ref[...], preferred_element_type=F32)
    hq_ref[0:halo, :] = hq[0:halo] * keep_prev
    hq_ref[halo:tm + halo, :] = hq[halo:tm + halo]
    hq_ref[tm + halo:tm + 2 * halo, :] = hq[tm + halo:tm + 2 * halo] * keep_next
    z_ref[...] = jnp.dot(xb, wz_ref[...], preferred_element_type=F32).astype(BF16)
    na_ref[...] = jnp.dot(xb, wna_ref[...], preferred_element_type=F32).astype(BF16)
    hab2 = jnp.dot(xb, wg_ref[...], preferred_element_type=F32)

    first_tap = halo - CONV_WIDTH // 2
    for j in range(C_QKV // LANES):
        cols = slice(j * LANES, (j + 1) * LANES)
        y = hq_ref[first_tap:first_tap + tm, cols] * cw_ref[0:1, cols]
        for tap in range(1, CONV_WIDTH):
            y = y + hq_ref[first_tap + tap:first_tap + tap + tm, cols] * cw_ref[tap:tap + 1, cols]
        y = y * _sigmoid(y)
        if j < 2 * GDN_HEADS:
            fac = lax.rsqrt(jnp.sum(y * y, axis=-1, keepdims=True) + RMS_EPS)
            if j < GDN_HEADS:
                fac = fac * (GDN_HEAD_DIM ** -0.5)
            y = y * fac
        qkv_ref[:, cols] = y.astype(BF16)

    tpos = lax.broadcasted_iota(jnp.int32, (tm, LANES), 0) & (GDN_CHUNK - 1)
    lane = lax.broadcasted_iota(jnp.int32, (tm, LANES), 1)
    for d in range(2):
        hab = hab2[:, d * LANES:(d + 1) * LANES]
        sp_in = hab + dtb_ref[d:d + 1, :]
        softplus = jnp.maximum(sp_in, 0.0) + jnp.log1p(jnp.exp(-jnp.abs(sp_in)))
        g = -jnp.exp(alog_ref[d:d + 1, :]) * softplus
        g = jnp.where(lane < GDN_HEADS, g, 0.0)
        gc = _chunk_cumsum(g, tpos, reverse=(d == 1))
        gates_ref[d] = jnp.where(lane < GDN_HEADS, gc, _sigmoid(hab))
        gct = gc.T
        pad_rows = jnp.zeros((7, GDN_HEADS * GDN_CHUNK), F32)
        for c in range(tm // GDN_CHUNK):
            row_c = jnp.concatenate([gct[h:h + 1, c * GDN_CHUNK:(c + 1) * GDN_CHUNK] for h in range(GDN_HEADS)], axis=1)
            gcr_ref[d, c] = jnp.concatenate([row_c, pad_rows], axis=0)


def _inproj(x2d, lng, lnb, ws, conv_w8, alog_rows, dtb_rows, apply_ln, seq):
    bt, dm = x2d.shape
    tm = TM_PROJ
    grid = (bt // tm,)
    row = lambda i: (i, 0)
    hb = tm // CONV_HALO
    last_hb = bt // CONV_HALO - 1
    in_specs = [
        pl.BlockSpec((tm, dm), row),
        pl.BlockSpec((CONV_HALO, dm), lambda i: (jnp.maximum(i * hb - 1, 0), 0)),
        pl.BlockSpec((CONV_HALO, dm), lambda i: (jnp.minimum((i + 1) * hb, last_hb), 0)),
        _const_spec((1, dm)), _const_spec((1, dm)),
        _const_spec((dm, C_QKV)), _const_spec((dm, C_Z)), _const_spec((dm, 2 * LANES)), _const_spec((dm, C_NA)),
        _const_spec((8, C_QKV)),
        _const_spec((2, LANES)), _const_spec((2, LANES)),
    ]
    out_shape = [
        jax.ShapeDtypeStruct((bt, C_QKV), BF16),
        jax.ShapeDtypeStruct((bt, C_Z), BF16),
        jax.ShapeDtypeStruct((2, bt, LANES), F32),
        jax.ShapeDtypeStruct((2, bt // GDN_CHUNK, 8, GDN_HEADS * GDN_CHUNK), F32),
        jax.ShapeDtypeStruct((bt, C_NA), BF16),
    ]
    out_specs = [
        pl.BlockSpec((tm, C_QKV), row),
        pl.BlockSpec((tm, C_Z), row),
        pl.BlockSpec((2, tm, LANES), lambda i: (0, i, 0)),
        pl.BlockSpec((2, tm // GDN_CHUNK, 8, GDN_HEADS * GDN_CHUNK), lambda i: (0, i, 0, 0)),
        pl.BlockSpec((tm, C_NA), row),
    ]
    if apply_ln:
        out_shape = [jax.ShapeDtypeStruct((bt, dm), F32)] + out_shape
        out_specs = [pl.BlockSpec((tm, dm), row)] + out_specs
    return pl.pallas_call(
        functools.partial(_inproj_kernel, apply_ln=apply_ln, tiles_per_seq=seq // tm),
        grid=grid, in_specs=in_specs, out_specs=out_specs, out_shape=out_shape,
        scratch_shapes=[pltpu.VMEM((tm + 2 * CONV_HALO, C_QKV), F32)],
        compiler_params=_cparams(("parallel",)),
        name="inproj_ln" if apply_ln else "inproj",
    )(x2d, x2d, x2d, lng, lnb, *ws, conv_w8, alog_rows, dtb_rows)


def _block_diag(x, nblk):
    w = x.shape[1] // nblk
    blk = lax.broadcasted_iota(jnp.int32, x.shape, 1) // w
    zero = jnp.zeros_like(x)
    return jnp.concatenate([jnp.where(blk == h, x, zero) for h in range(nblk)], axis=0)


def _lane_blocks(cols, width):
    r, n = cols.shape
    return jnp.concatenate([jnp.broadcast_to(cols[:, j:j + 1], (r, width)) for j in range(n)], axis=1)


def _unit_tri_inverses(mats, nblk):
    c = mats[0].shape[0]
    ri = lax.broadcasted_iota(jnp.int32, (c, nblk * c), 0)
    ci = lax.broadcasted_iota(jnp.int32, (c, nblk * c), 1) % c
    eye = (ri == ci).astype(F32)
    ps = [-a for a in mats]
    xs = [eye + p for p in ps]
    pbs = [p.astype(BF16) for p in ps]
    ps = [jnp.dot(pb, _block_diag(pb, nblk), preferred_element_type=F32) for pb in pbs]
    m = 2
    while 2 * m < c:
        pbs = [p.astype(BF16) for p in ps]
        prods = [jnp.dot(jnp.concatenate([x.astype(BF16), pb], axis=0), _block_diag(pb, nblk),
                         preferred_element_type=F32) for x, pb in zip(xs, pbs)]
        xs = [x + prod[:c] for x, prod in zip(xs, prods)]
        ps = [prod[c:] for prod in prods]
        m *= 2
    return [x + jnp.dot(x.astype(BF16), _block_diag(p.astype(BF16), nblk), preferred_element_type=F32)
            for x, p in zip(xs, ps)]


def _gdn_scan_kernel(qf_ref, kf_ref, vf_ref, gf_ref, grf_ref, qb_ref, kb_ref, vb_ref, gb_ref, grb_ref,
                     of_ref, ob_ref, s_ref, en_ref, bn_ref, wq_ref, *, tt, nb):
    i = pl.program_id(1)
    nc = tt // GDN_CHUNK
    C = GDN_CHUNK
    dh = GDN_HEAD_DIM
    H = GDN_HEADS

    @pl.when(i == 0)
    def _():
        s_ref[...] = jnp.zeros_like(s_ref)

    ri = lax.broadcasted_iota(jnp.int32, (C, H * C), 0)
    ci = lax.broadcasted_iota(jnp.int32, (C, H * C), 1) % C
    mask_incl = (ri >= ci, ri <= ci)
    mask_strict = (ri > ci, ri < ci)
    dirs = ((qf_ref, kf_ref, vf_ref, gf_ref, grf_ref, of_ref), (qb_ref, kb_ref, vb_ref, gb_ref, grb_ref, ob_ref))
    streams = [(b, d) for b in range(nb) for d in range(2)]
    half_lo = lax.broadcasted_iota(jnp.int32, (C, dh), 1) < C
    nt_dims = (((1,), (1,)), ((), ()))
    tn_dims = (((0,), (0,)), ((), ()))

    def chunk_end_decay(gt, d):
        return gt[C - 1:C, :] if d == 0 else gt[0:1, :]

    def halve_blocks(x):
        lo = half_lo[0:x.shape[0]]
        return jnp.concatenate([jnp.where(lo, x[:, (2 * p) * dh:(2 * p + 1) * dh], x[:, (2 * p + 1) * dh:(2 * p + 2) * dh])
                                for p in range(H // 2)], axis=1)

    def prep_group(cg, carry):
        items = [(cg * GDN_PREP_CHUNKS + j, u) for j in range(GDN_PREP_CHUNKS) for u in range(len(streams))]
        lhs_l, bdk_l, kst_l, gam_l, gamb_l, edec_l = [], [], [], [], [], []
        for c, u in items:
            b, d = streams[u]
            q_ref, k_ref, _, g_ref, gr_ref, _ = dirs[d]
            r0 = pl.multiple_of(c * C, C)
            gt = g_ref[0, b, pl.ds(r0, C), :]
            gr = gr_ref[0, b, c]
            q = q_ref[b, pl.ds(r0, C), :]
            k = k_ref[b, pl.ds(r0, C), :]
            gc_n = _lane_blocks(gt[:, 0:H], dh)
            beta_n = _lane_blocks(gt[:, H:2 * H], dh)
            gend_n = _lane_blocks(chunk_end_decay(gt, d)[:, 0:H], dh)
            en_ref[c, u] = jnp.exp(gc_n)
            bn_ref[c, u] = beta_n
            gc_c = halve_blocks(gc_n)
            gam = jnp.where(mask_incl[d], jnp.exp(gc_c - gr[0:1, :]), 0.0)
            gam_l.append(gam)
            gamb_l.append(gam * halve_blocks(beta_n))
            edec_l.append(jnp.exp(halve_blocks(gend_n) - gc_c))
            lhs_l.append(jnp.concatenate([k, q], axis=0))
            bdk_l.append(_block_diag(k, H))
            kst_l.append(jnp.concatenate([k[:, h * dh:(h + 1) * dh] for h in range(H)], axis=0))
        kq_l = [lax.dot_general(lhs, bdk, nt_dims, preferred_element_type=F32)
                for lhs, bdk in zip(lhs_l, bdk_l)]
        a_l = [jnp.where(mask_strict[streams[u][1]], kq[:C] * gamb, 0.0)
               for kq, gamb, (_, u) in zip(kq_l, gamb_l, items)]
        qk_l = [(kq[C:] * gam).astype(BF16) for kq, gam in zip(kq_l, gam_l)]
        tinv_l = _unit_tri_inverses(a_l, H)
        bdt_l = [_block_diag(t.astype(BF16), H) for t in tinv_l]
        bdte_l = [_block_diag((t * edec).astype(BF16), H) for t, edec in zip(tinv_l, edec_l)]
        wc_l = [lax.dot_general(kst, bdte, tn_dims, preferred_element_type=F32)
                for kst, bdte in zip(kst_l, bdte_l)]
        qt_l = [jnp.dot(qk, bdt, preferred_element_type=F32) for qk, bdt in zip(qk_l, bdt_l)]
        for (c, u), wc, qt in zip(items, wc_l, qt_l):
            wq_ref[c, u] = jnp.concatenate([wc, qt], axis=0).astype(BF16)
        return carry

    lax.fori_loop(0, nc // GDN_PREP_CHUNKS, prep_group, 0)

    def scan_step(step, carry):
        cs = (step, nc - 1 - step)
        egs, kq_in, v_in = [], [], []
        for u, (b, d) in enumerate(streams):
            q_ref, k_ref, v_ref, g_ref = dirs[d][:4]
            r0 = pl.multiple_of(cs[d] * C, C)
            egs.append(jnp.exp(chunk_end_decay(g_ref[0, b, pl.ds(r0, C), :], d)))
            kq_in.append(jnp.concatenate([k_ref[b, pl.ds(r0, C), :], q_ref[b, pl.ds(r0, C), :]], axis=0))
            v_in.append(v_ref[b, pl.ds(r0, C), :])
        s_l = [[s_ref[u * H + h] for h in range(H)] for u in range(len(streams))]
        x1_l = [jnp.concatenate([jnp.dot(kq_in[u][:, h * dh:(h + 1) * dh], s_l[u][h].astype(BF16),
                                         preferred_element_type=F32) for h in range(H)], axis=1)
                for u in range(len(streams))]
        res_l, qs_l = [], []
        for u, (b, d) in enumerate(streams):
            en = en_ref[cs[d], u]
            res_l.append((bn_ref[cs[d], u] * (v_in[u].astype(F32) - x1_l[u][:C] * en)).astype(BF16))
            qs_l.append(x1_l[u][C:] * en)
        z_l = [jnp.dot(wq_ref[cs[d], u], _block_diag(res, H), preferred_element_type=F32)
               for (u, (b, d)), res in zip(enumerate(streams), res_l)]
        for u, (b, d) in enumerate(streams):
            z = z_l[u]
            for h in range(H):
                s_ref[u * H + h] = s_l[u][h] * egs[u][:, h:h + 1] + z[:dh, h * dh:(h + 1) * dh]
            r0 = pl.multiple_of(cs[d] * C, C)
            dirs[d][5][b, pl.ds(r0, C), :] = qs_l[u] + z[dh:]
        return carry

    lax.fori_loop(0, nc, scan_step, 0)


def _gdn_scan(qkvp, gates, gcr):
    b, seq, _ = qkvp.shape
    tt = TT_SCAN
    nb = GDN_BATCH_PER_STEP if b % GDN_BATCH_PER_STEP == 0 else 1
    nt = seq // tt
    nc = tt // GDN_CHUNK
    C = GDN_CHUNK
    dh = GDN_HEAD_DIM
    H = GDN_HEADS
    nstream = 2 * nb

    def dir_specs(d):
        tmap = (lambda i: i) if d == 0 else (lambda i: nt - 1 - i)
        return [
            pl.BlockSpec((nb, tt, GDN_WIDTH), lambda bi, i: (bi, tmap(i), 0)),
            pl.BlockSpec((nb, tt, GDN_WIDTH), lambda bi, i: (bi, tmap(i), 1)),
            pl.BlockSpec((nb, tt, GDN_WIDTH), lambda bi, i: (bi, tmap(i), 2)),
            pl.BlockSpec((1, nb, tt, LANES), lambda bi, i: (d, bi, tmap(i), 0)),
            pl.BlockSpec((1, nb, nc, 8, H * C), lambda bi, i: (d, bi, tmap(i), 0, 0)),
        ]

    out_sds = jax.ShapeDtypeStruct((b, seq, GDN_WIDTH), F32)
    return pl.pallas_call(
        functools.partial(_gdn_scan_kernel, tt=tt, nb=nb),
        grid=(b // nb, nt),
        in_specs=dir_specs(0) + dir_specs(1),
        out_specs=[pl.BlockSpec((nb, tt, GDN_WIDTH), lambda bi, i: (bi, i, 0)),
                   pl.BlockSpec((nb, tt, GDN_WIDTH), lambda bi, i: (bi, nt - 1 - i, 0))],
        out_shape=[out_sds, out_sds],
        scratch_shapes=[
            pltpu.VMEM((nstream * H, dh, dh), F32),
            pltpu.VMEM((nc, nstream, C, H * dh), F32),
            pltpu.VMEM((nc, nstream, C, H * dh), F32),
            pltpu.VMEM((nc, nstream, dh + C, H * C), BF16),
        ],
        compiler_params=_cparams(("parallel", "arbitrary")),
        name="gdn_scan",
    )(qkvp, qkvp, qkvp, gates, gcr, qkvp, qkvp, qkvp, gates, gcr)


def _na_kernel(q_ref, k_ref, v_ref, bm_ref, g_ref, o_ref, *, rows):
    W = GRID_W
    band = NA_WIN_R * W
    hd = NA_HEAD_DIM
    lane_q = lax.broadcasted_iota(jnp.int32, (W, LANES), 1)
    first = lane_q < hd
    scale = jnp.asarray(hd ** -0.5, BF16)

    nt_dims = (((1,), (1,)), ((), ()))

    def row_group(gi, carry):
        rs = [gi * NA_ROWS_PER_STEP + j for j in range(NA_ROWS_PER_STEP)]
        r0s = [jnp.clip(r - NA_WIN_R // 2, 0, rows - NA_WIN_R) for r in rs]
        qs_l, kb_l, vb_l = [], [], []
        for r, r0 in zip(rs, r0s):
            q2 = q_ref[0, pl.ds(pl.multiple_of(r * W, W), W), :] * scale
            zero = jnp.zeros_like(q2)
            qs_l.append(jnp.concatenate([jnp.where(first, q2, zero), jnp.where(first, zero, q2)], axis=0))
            kb_l.append(k_ref[0, pl.ds(pl.multiple_of(r0 * W, W), band), :])
            vb_l.append(v_ref[0, pl.ds(pl.multiple_of(r0 * W, W), band), :])
        s_l = [lax.dot_general(qs, kb, nt_dims, preferred_element_type=F32) for qs, kb in zip(qs_l, kb_l)]
        s_l = [s + bm_ref[0, r - r0] for s, r, r0 in zip(s_l, rs, r0s)]
        m_l = [jnp.max(s, axis=-1, keepdims=True) for s in s_l]
        p_l = [jnp.exp(s - m) for s, m in zip(s_l, m_l)]
        l_l = [jnp.sum(p, axis=-1, keepdims=True) for p in p_l]
        pb_l = [p.astype(BF16) for p in p_l]
        o0_l = [jnp.dot(pb[:W], vb, preferred_element_type=F32) for pb, vb in zip(pb_l, vb_l)]
        o1_l = [jnp.dot(pb[W:], vb, preferred_element_type=F32) for pb, vb in zip(pb_l, vb_l)]
        for r, o0, o1, l in zip(rs, o0_l, o1_l, l_l):
            o = jnp.where(first, o0 / l[:W], o1 / l[W:])
            sq = o * o
            ms0 = jnp.sum(jnp.where(first, sq, 0.0), axis=-1, keepdims=True)
            ms1 = jnp.sum(jnp.where(first, 0.0, sq), axis=-1, keepdims=True)
            ms = jnp.where(first, ms0, ms1) * (1.0 / hd)
            o_ref[0, pl.ds(pl.multiple_of(r * W, W), W), :] = (o * lax.rsqrt(ms + RMS_EPS) * g_ref[...]).astype(BF16)
        return carry

    lax.fori_loop(0, rows // NA_ROWS_PER_STEP, row_group, 0)


def _na(h_na, bias_tab, g_row):
    b, seq, _ = h_na.shape
    rows = seq // GRID_W
    assert rows >= NA_WIN_R
    npair = NA_WIDTH // LANES
    band = NA_WIN_R * GRID_W
    return pl.pallas_call(
        functools.partial(_na_kernel, rows=rows),
        grid=(b, npair),
        in_specs=[
            pl.BlockSpec((1, seq, LANES), lambda bi, p: (bi, 0, p)),
            pl.BlockSpec((1, seq, LANES), lambda bi, p: (bi, 0, npair + p)),
            pl.BlockSpec((1, seq, LANES), lambda bi, p: (bi, 0, 2 * npair + p)),
            pl.BlockSpec((1, NA_WIN_R, 2 * GRID_W, band), lambda bi, p: (p, 0, 0, 0)),
            pl.BlockSpec((1, LANES), lambda bi, p: (0, 0)),
        ],
        out_specs=pl.BlockSpec((1, seq, LANES), lambda bi, p: (bi, 0, p)),
        out_shape=jax.ShapeDtypeStruct((b, seq, NA_WIDTH), BF16),
        compiler_params=_cparams(("parallel", "parallel")),
        name="natten",
    )(h_na, h_na, h_na, bias_tab, g_row)


def _na_bias_table(rpb_l):
    W = GRID_W
    vi = np.arange(NA_WIN_R)
    kr = np.arange(NA_WIN_R)
    dr = kr[None, :] - vi[:, None] + NA_WIN_R - 1
    qc = np.arange(W)
    kc = np.arange(W)
    win_start = np.clip(qc - NA_WIN_C // 2, 0, W - NA_WIN_C)
    in_win = (kc[None, :] >= win_start[:, None]) & (kc[None, :] < win_start[:, None] + NA_WIN_C)
    dc = kc[None, :] - qc[:, None] + NA_WIN_C - 1
    rsel = (dr[:, :, None] == np.arange(2 * NA_WIN_R - 1)).astype(np.float32)
    csel = ((dc[:, :, None] == np.arange(2 * NA_WIN_C - 1)) & in_win[:, :, None]).astype(np.float32)
    rpb_pairs = rpb_l.astype(F32).reshape(NA_HEADS // 2, 2, 2 * NA_WIN_R - 1, 2 * NA_WIN_C - 1)
    tab = jnp.einsum("pjab,vka,qcb->pvjqkc", rpb_pairs, rsel, csel,
                     precision=lax.Precision.HIGHEST)
    tab = jnp.where(jnp.asarray(in_win)[:, None, :], tab, -jnp.inf)
    return tab.reshape(NA_HEADS // 2, NA_WIN_R, 2 * W, NA_WIN_R * W)


def _mix_ffn_kernel(x_ref, of_ref, ob_ref, z_ref, ona_ref, gg_ref, wo_ref, l1g_ref, l1b_ref,
                    w1_ref, b1_ref, w2_ref, b2_ref, l2g_ref, l2b_ref, out_ref):
    dh = GDN_HEAD_DIM
    tm = x_ref.shape[0]
    d_ff = w1_ref.shape[1]
    rows = [slice(s * (tm // FFN_SUBTILES), (s + 1) * (tm // FFN_SUBTILES)) for s in range(FFN_SUBTILES)]

    def gated(rs):
        o = of_ref[rs, :] + ob_ref[rs, :]
        z = z_ref[rs, :].astype(F32)
        gate = z * _sigmoid(z)
        parts = []
        for h in range(GDN_HEADS):
            cols = slice(h * dh, (h + 1) * dh)
            oh = o[:, cols]
            ms = jnp.mean(oh * oh, axis=-1, keepdims=True)
            parts.append((oh * lax.rsqrt(ms + RMS_EPS) * gg_ref[:, cols] * gate[:, cols]).astype(BF16))
        return jnp.concatenate(parts, axis=-1)

    og_l = [gated(rs) for rs in rows]
    mix_l = [jnp.dot(og, wo_ref[0:GDN_WIDTH, :], preferred_element_type=F32)
             + jnp.dot(ona_ref[rs, :], wo_ref[GDN_WIDTH:, :], preferred_element_type=F32)
             for og, rs in zip(og_l, rows)]
    x1_l = [_layer_norm(DEEPNORM_ALPHA * x_ref[rs, :] + mix, l1g_ref[...], l1b_ref[...])
            for mix, rs in zip(mix_l, rows)]
    x1b_l = [x1.astype(BF16) for x1 in x1_l]
    acc_l = [jnp.zeros(x1.shape, F32) for x1 in x1_l]
    for f in range(d_ff // FF_CHUNK):
        fs = slice(f * FF_CHUNK, (f + 1) * FF_CHUNK)
        hf_l = [jnp.dot(x1b, w1_ref[:, fs], preferred_element_type=F32) + b1_ref[:, fs] for x1b in x1b_l]
        hf_l = [jnp.square(jnp.maximum(hf, 0.0)).astype(BF16) for hf in hf_l]
        acc_l = [acc + jnp.dot(hf, w2_ref[fs, :], preferred_element_type=F32) for acc, hf in zip(acc_l, hf_l)]
    for rs, x1, acc in zip(rows, x1_l, acc_l):
        y = DEEPNORM_ALPHA * x1 + (acc + b2_ref[...])
        out_ref[rs, :] = _layer_norm(y, l2g_ref[...], l2b_ref[...])


def _mix_ffn(x2d, o_f, o_b, z, ona, gg, wo, l1g, l1b, w1, b1, w2, b2, l2g, l2b):
    bt, dm = x2d.shape
    d_ff = w1.shape[1]
    tm = TM_FFN
    row = lambda i: (i, 0)
    return pl.pallas_call(
        _mix_ffn_kernel,
        grid=(bt // tm,),
        in_specs=[
            pl.BlockSpec((tm, dm), row),
            pl.BlockSpec((tm, GDN_WIDTH), row),
            pl.BlockSpec((tm, GDN_WIDTH), row),
            pl.BlockSpec((tm, GDN_WIDTH), row),
            pl.BlockSpec((tm, NA_WIDTH), row),
            _const_spec((1, GDN_WIDTH)),
            _const_spec((dm, dm)),
            _const_spec((1, dm)), _const_spec((1, dm)),
            _const_spec((dm, d_ff)), _const_spec((1, d_ff)),
            _const_spec((d_ff, dm)), _const_spec((1, dm)),
            _const_spec((1, dm)), _const_spec((1, dm)),
        ],
        out_specs=pl.BlockSpec((tm, dm), row),
        out_shape=jax.ShapeDtypeStruct((bt, dm), F32),
        compiler_params=_cparams(("parallel",)),
        name="mix_ffn",
    )(x2d, o_f, o_b, z, ona, gg, wo, l1g, l1b, w1, b1, w2, b2, l2g, l2b)


def _split_w_in(w_l):
    dm = w_l.shape[0]
    h = GDN_HEADS
    a0 = 4 * GDN_WIDTH
    b0 = a0 + 2 * h
    na0 = b0 + 2 * h
    pad = jnp.zeros((dm, LANES - 2 * h), w_l.dtype)
    gates = jnp.concatenate([w_l[:, a0:a0 + h], w_l[:, b0:b0 + h], pad,
                             w_l[:, a0 + h:a0 + 2 * h], w_l[:, b0 + h:b0 + 2 * h], pad], axis=1)
    return tuple(w.astype(BF16) for w in (w_l[:, :C_QKV], w_l[:, C_QKV:a0], gates, w_l[:, na0:]))


def _gate_rows(p):
    return jnp.pad(p.astype(F32), ((0, 0), (0, LANES - p.shape[1])))


def kernel(x, ln_in_g, ln_in_b, w_in, conv_w, a_log, dt_bias, gdn_norm_g, rpb, na_norm_g, w_out,
           ln1_g, ln1_b, w1, b1, w2, b2, ln2_g, ln2_b):
    B, T, dm = x.shape
    bt = B * T
    nchunks = T // GDN_CHUNK
    row = lambda v: v.reshape(1, -1).astype(F32)
    xs = x.reshape(bt, dm)
    for l in range(DEPTH):
        w_l = _split_w_in(w_in[l])
        conv8 = jnp.pad(conv_w[l].astype(F32), ((0, 8 - CONV_WIDTH), (0, 0)))
        outs = _inproj(xs, row(ln_in_g), row(ln_in_b), w_l, conv8, _gate_rows(a_log[l]), _gate_rows(dt_bias[l]),
                       apply_ln=(l == 0), seq=T)
        if l == 0:
            xs, qkvp, z, gates, gcr, h_na = outs
        else:
            qkvp, z, gates, gcr, h_na = outs
        qkvp = qkvp.reshape(B, T, C_QKV)
        gates4 = gates.reshape(2, B, T, LANES)
        gcr = gcr.reshape(2, B, nchunks, 8, GDN_HEADS * GDN_CHUNK)
        o_f, o_b = _gdn_scan(qkvp, gates4, gcr)
        ona = _na(h_na.reshape(B, T, C_NA), _na_bias_table(rpb[l]),
                  jnp.tile(na_norm_g[l].astype(F32), LANES // NA_HEAD_DIM).reshape(1, LANES))
        xs = _mix_ffn(xs, o_f.reshape(bt, GDN_WIDTH), o_b.reshape(bt, GDN_WIDTH), z, ona.reshape(bt, NA_WIDTH),
                      jnp.tile(gdn_norm_g[l].astype(F32), GDN_HEADS).reshape(1, GDN_WIDTH),
                      w_out[l].astype(BF16), row(ln1_g[l]), row(ln1_b[l]),
                      w1[l].astype(BF16), row(b1[l]), w2[l].astype(BF16), row(b2[l]),
                      row(ln2_g[l]), row(ln2_b[l]))
    return xs.reshape(B, T, dm)
```

```python
import functools

import jax
import jax.numpy as jnp
import numpy as np
from jax import lax
from jax.experimental import pallas as pl
from jax.experimental.pallas import tpu as pltpu

F32 = jnp.float32
BF16 = jnp.bfloat16

GRID_W = 64
GDN_HEAD_DIM = 128
GDN_HEADS = 4
GDN_WIDTH = GDN_HEADS * GDN_HEAD_DIM
NA_HEAD_DIM = 64
NA_HEADS = 8
NA_WIDTH = NA_HEADS * NA_HEAD_DIM
CONV_WIDTH = 5
GDN_CHUNK = 64
NA_WIN_R = 8
NA_WIN_C = 16
DEPTH = 2
DEEPNORM_ALPHA = (2 * DEPTH) ** 0.25
LN_EPS = 1e-5
RMS_EPS = 1e-6

LANES = 128
VMEM_LIMIT = 56 * 1024 * 1024

TM_PROJ = 512
TM_FFN = 512
TT_SCAN = 256
FF_CHUNK = 1024
NA_ROWS_PER_STEP = 8
FFN_SUBTILES = 2
GDN_BATCH_PER_STEP = 4
GDN_PREP_CHUNKS = 2

C_QKV = 3 * GDN_WIDTH
C_Z = GDN_WIDTH
C_NA = 3 * NA_WIDTH


def _cparams(sem):
    return pltpu.CompilerParams(dimension_semantics=sem, vmem_limit_bytes=VMEM_LIMIT)


def _const_spec(shape):
    nd = len(shape)
    return pl.BlockSpec(shape, lambda *_: (0,) * nd, pipeline_mode=pl.Buffered(1))


def _layer_norm(y, g, b):
    mu = jnp.mean(y, axis=-1, keepdims=True)
    yc = y - mu
    var = jnp.mean(yc * yc, axis=-1, keepdims=True)
    return yc * lax.rsqrt(var + LN_EPS) * g + b


def _sigmoid(x):
    return 1.0 / (1.0 + jnp.exp(-x))


def _chunk_cumsum(g, tpos, reverse):
    n = g.shape[0]
    s = 1
    while s < GDN_CHUNK:
        if reverse:
            g = g + jnp.where(tpos < GDN_CHUNK - s, pltpu.roll(g, n - s, 0), 0.0)
        else:
            g = g + jnp.where(tpos >= s, pltpu.roll(g, s, 0), 0.0)
        s *= 2
    return g


CONV_HALO = 8


def _inproj_kernel(x_ref, xp_ref, xn_ref, lng_ref, lnb_ref, wq_ref, wz_ref, wg_ref, wna_ref, cw_ref, alog_ref, dtb_ref,
                   *refs, apply_ln, tiles_per_seq):
    if apply_ln:
        xo_ref, qkv_ref, z_ref, gates_ref, gcr_ref, na_ref, hq_ref = refs
    else:
        qkv_ref, z_ref, gates_ref, gcr_ref, na_ref, hq_ref = refs
    i = pl.program_id(0)
    tm = x_ref.shape[0]
    halo = CONV_HALO
    x, x_prev, x_next = x_ref[...], xp_ref[...], xn_ref[...]
    if apply_ln:
        x = _layer_norm(x, lng_ref[...], lnb_ref[...])
        x_prev = _layer_norm(x_prev, lng_ref[...], lnb_ref[...])
        x_next = _layer_norm(x_next, lng_ref[...], lnb_ref[...])
        xo_ref[...] = x
    xb = x.astype(BF16)
    x_ext = jnp.concatenate([x_prev, x, x_next], axis=0).astype(BF16)

    pos = i % tiles_per_seq
    keep_prev = jnp.where(pos == 0, 0.0, 1.0).astype(F32)
    keep_next = jnp.where(pos == tiles_per_seq - 1, 0.0, 1.0).astype(F32)
    hq = jnp.dot(x_ext, wq_ref[...], preferred_element_type=F32)
    hq_ref[0:halo, :] = hq[0:halo] * keep_prev
    hq_ref[halo:tm + halo, :] = hq[halo:tm + halo]
    hq_ref[tm + halo:tm + 2 * halo, :] = hq[tm + halo:tm + 2 * halo] * keep_next
    z_ref[...] = jnp.dot(xb, wz_ref[...], preferred_element_type=F32).astype(BF16)
    na_ref[...] = jnp.dot(xb, wna_ref[...], preferred_element_type=F32).astype(BF16)
    hab2 = jnp.dot(xb, wg_ref[...], preferred_element_type=F32)

    first_tap = halo - CONV_WIDTH // 2
    for j in range(C_QKV // LANES):
        cols = slice(j * LANES, (j + 1) * LANES)
        y = hq_ref[first_tap:first_tap + tm, cols] * cw_ref[0:1, cols]
        for tap in range(1, CONV_WIDTH):
            y = y + hq_ref[first_tap + tap:first_tap + tap + tm, cols] * cw_ref[tap:tap + 1, cols]
        y = y * _sigmoid(y)
        if j < 2 * GDN_HEADS:
            fac = lax.rsqrt(jnp.sum(y * y, axis=-1, keepdims=True) + RMS_EPS)
            if j < GDN_HEADS:
                fac = fac * (GDN_HEAD_DIM ** -0.5)
            y = y * fac
        qkv_ref[:, cols] = y.astype(BF16)

    tpos = lax.broadcasted_iota(jnp.int32, (tm, LANES), 0) & (GDN_CHUNK - 1)
    lane = lax.broadcasted_iota(jnp.int32, (tm, LANES), 1)
    for d in range(2):
        hab = hab2[:, d * LANES:(d + 1) * LANES]
        sp_in = hab + dtb_ref[d:d + 1, :]
        softplus = jnp.maximum(sp_in, 0.0) + jnp.log1p(jnp.exp(-jnp.abs(sp_in)))
        g = -jnp.exp(alog_ref[d:d + 1, :]) * softplus
        g = jnp.where(lane < GDN_HEADS, g, 0.0)
        gc = _chunk_cumsum(g, tpos, reverse=(d == 1))
        gates_ref[d] = jnp.where(lane < GDN_HEADS, gc, _sigmoid(hab))
        gct = gc.T
        pad_rows = jnp.zeros((7, GDN_HEADS * GDN_CHUNK), F32)
        for c in range(tm // GDN_CHUNK):
            row_c = jnp.concatenate([gct[h:h + 1, c * GDN_CHUNK:(c + 1) * GDN_CHUNK] for h in range(GDN_HEADS)], axis=1)
            gcr_ref[d, c] = jnp.concatenate([row_c, pad_rows], axis=0)


def _inproj(x2d, lng, lnb, ws, conv_w8, alog_rows, dtb_rows, apply_ln, seq):
    bt, dm = x2d.shape
    tm = TM_PROJ
    grid = (bt // tm,)
    row = lambda i: (i, 0)
    hb = tm // CONV_HALO
    last_hb = bt // CONV_HALO - 1
    in_specs = [
        pl.BlockSpec((tm, dm), row),
        pl.BlockSpec((CONV_HALO, dm), lambda i: (jnp.maximum(i * hb - 1, 0), 0)),
        pl.BlockSpec((CONV_HALO, dm), lambda i: (jnp.minimum((i + 1) * hb, last_hb), 0)),
        _const_spec((1, dm)), _const_spec((1, dm)),
        _const_spec((dm, C_QKV)), _const_spec((dm, C_Z)), _const_spec((dm, 2 * LANES)), _const_spec((dm, C_NA)),
        _const_spec((8, C_QKV)),
        _const_spec((2, LANES)), _const_spec((2, LANES)),
    ]
    out_shape = [
        jax.ShapeDtypeStruct((bt, C_QKV), BF16),
        jax.ShapeDtypeStruct((bt, C_Z), BF16),
        jax.ShapeDtypeStruct((2, bt, LANES), F32),
        jax.ShapeDtypeStruct((2, bt // GDN_CHUNK, 8, GDN_HEADS * GDN_CHUNK), F32),
        jax.ShapeDtypeStruct((bt, C_NA), BF16),
    ]
    out_specs = [
        pl.BlockSpec((tm, C_QKV), row),
        pl.BlockSpec((tm, C_Z), row),
        pl.BlockSpec((2, tm, LANES), lambda i: (0, i, 0)),
        pl.BlockSpec((2, tm // GDN_CHUNK, 8, GDN_HEADS * GDN_CHUNK), lambda i: (0, i, 0, 0)),
        pl.BlockSpec((tm, C_NA), row),
    ]
    if apply_ln:
        out_shape = [jax.ShapeDtypeStruct((bt, dm), F32)] + out_shape
        out_specs = [pl.BlockSpec((tm, dm), row)] + out_specs
    return pl.pallas_call(
        functools.partial(_inproj_kernel, apply_ln=apply_ln, tiles_per_seq=seq // tm),
        grid=grid, in_specs=in_specs, out_specs=out_specs, out_shape=out_shape,
        scratch_shapes=[pltpu.VMEM((tm + 2 * CONV_HALO, C_QKV), F32)],
        compiler_params=_cparams(("parallel",)),
        name="inproj_ln" if apply_ln else "inproj",
    )(x2d, x2d, x2d, lng, lnb, *ws, conv_w8, alog_rows, dtb_rows)


def _block_diag(x, nblk):
    w = x.shape[1] // nblk
    blk = lax.broadcasted_iota(jnp.int32, x.shape, 1) // w
    zero = jnp.zeros_like(x)
    return jnp.concatenate([jnp.where(blk == h, x, zero) for h in range(nblk)], axis=0)


def _lane_blocks(cols, width):
    r, n = cols.shape
    return jnp.concatenate([jnp.broadcast_to(cols[:, j:j + 1], (r, width)) for j in range(n)], axis=1)


def _unit_tri_inverses(mats, nblk):
    c = mats[0].shape[0]
    ri = lax.broadcasted_iota(jnp.int32, (c, nblk * c), 0)
    ci = lax.broadcasted_iota(jnp.int32, (c, nblk * c), 1) % c
    eye = (ri == ci).astype(F32)
    ps = [-a for a in mats]
    xs = [eye + p for p in ps]
    pbs = [p.astype(BF16) for p in ps]
    ps = [jnp.dot(pb, _block_diag(pb, nblk), preferred_element_type=F32) for pb in pbs]
    m = 2
    while 2 * m < c:
        pbs = [p.astype(BF16) for p in ps]
        prods = [jnp.dot(jnp.concatenate([x.astype(BF16), pb], axis=0), _block_diag(pb, nblk),
                         preferred_element_type=F32) for x, pb in zip(xs, pbs)]
        xs = [x + prod[:c] for x, prod in zip(xs, prods)]
        ps = [prod[c:] for prod in prods]
        m *= 2
    return [x + jnp.dot(x.astype(BF16), _block_diag(p.astype(BF16), nblk), preferred_element_type=F32)
            for x, p in zip(xs, ps)]


def _gdn_scan_kernel(qf_ref, kf_ref, vf_ref, gf_ref, grf_ref, qb_ref, kb_ref, vb_ref, gb_ref, grb_ref,
                     of_ref, ob_ref, s_ref, en_ref, bn_ref, wq_ref, *, tt, nb):
    i = pl.program_id(1)
    nc = tt // GDN_CHUNK
    C = GDN_CHUNK
    dh = GDN_HEAD_DIM
    H = GDN_HEADS

    @pl.when(i == 0)
    def _():
        s_ref[...] = jnp.zeros_like(s_ref)

    ri = lax.broadcasted_iota(jnp.int32, (C, H * C), 0)
    ci = lax.broadcasted_iota(jnp.int32, (C, H * C), 1) % C
    mask_incl = (ri >= ci, ri <= ci)
    mask_strict = (ri > ci, ri < ci)
    dirs = ((qf_ref, kf_ref, vf_ref, gf_ref, grf_ref, of_ref), (qb_ref, kb_ref, vb_ref, gb_ref, grb_ref, ob_ref))
    streams = [(b, d) for b in range(nb) for d in range(2)]
    half_lo = lax.broadcasted_iota(jnp.int32, (C, dh), 1) < C
    nt_dims = (((1,), (1,)), ((), ()))
    tn_dims = (((0,), (0,)), ((), ()))

    def chunk_end_decay(gt, d):
        return gt[C - 1:C, :] if d == 0 else gt[0:1, :]

    def halve_blocks(x):
        lo = half_lo[0:x.shape[0]]
        return jnp.concatenate([jnp.where(lo, x[:, (2 * p) * dh:(2 * p + 1) * dh], x[:, (2 * p + 1) * dh:(2 * p + 2) * dh])
                                for p in range(H // 2)], axis=1)

    def prep_group(cg, carry):
        items = [(cg * GDN_PREP_CHUNKS + j, u) for j in range(GDN_PREP_CHUNKS) for u in range(len(streams))]
        lhs_l, bdk_l, kst_l, gam_l, gamb_l, edec_l = [], [], [], [], [], []
        for c, u in items:
            b, d = streams[u]
            q_ref, k_ref, _, g_ref, gr_ref, _ = dirs[d]
            r0 = pl.multiple_of(c * C, C)
            gt = g_ref[0, b, pl.ds(r0, C), :]
            gr = gr_ref[0, b, c]
            q = q_ref[b, pl.ds(r0, C), :]
            k = k_ref[b, pl.ds(r0, C), :]
            gc_n = _lane_blocks(gt[:, 0:H], dh)
            beta_n = _lane_blocks(gt[:, H:2 * H], dh)
            gend_n = _lane_blocks(chunk_end_decay(gt, d)[:, 0:H], dh)
            en_ref[c, u] = jnp.exp(gc_n)
            bn_ref[c, u] = beta_n
            gc_c = halve_blocks(gc_n)
            gam = jnp.where(mask_incl[d], jnp.exp(gc_c - gr[0:1, :]), 0.0)
            gam_l.append(gam)
            gamb_l.append(gam * halve_blocks(beta_n))
            edec_l.append(jnp.exp(halve_blocks(gend_n) - gc_c))
            lhs_l.append(jnp.concatenate([k, q], axis=0))
            bdk_l.append(_block_diag(k, H))
            kst_l.append(jnp.concatenate([k[:, h * dh:(h + 1) * dh] for h in range(H)], axis=0))
        kq_l = [lax.dot_general(lhs, bdk, nt_dims, preferred_element_type=F32)
                for lhs, bdk in zip(lhs_l, bdk_l)]
        a_l = [jnp.where(mask_strict[streams[u][1]], kq[:C] * gamb, 0.0)
               for kq, gamb, (_, u) in zip(kq_l, gamb_l, items)]
        qk_l = [(kq[C:] * gam).astype(BF16) for kq, gam in zip(kq_l, gam_l)]
        tinv_l = _unit_tri_inverses(a_l, H)
        bdt_l = [_block_diag(t.astype(BF16), H) for t in tinv_l]
        bdte_l = [_block_diag((t * edec).astype(BF16), H) for t, edec in zip(tinv_l, edec_l)]
        wc_l = [lax.dot_general(kst, bdte, tn_dims, preferred_element_type=F32)
                for kst, bdte in zip(kst_l, bdte_l)]
        qt_l = [jnp.dot(qk, bdt, preferred_element_type=F32) for qk, bdt in zip(qk_l, bdt_l)]
        for (c, u), wc, qt in zip(items, wc_l, qt_l):
            wq_ref[c, u] = jnp.concatenate([wc, qt], axis=0).astype(BF16)
        return carry

    lax.fori_loop(0, nc // GDN_PREP_CHUNKS, prep_group, 0)

    def scan_step(step, carry):
        cs = (step, nc - 1 - step)
        egs, kq_in, v_in = [], [], []
        for u, (b, d) in enumerate(streams):
            q_ref, k_ref, v_ref, g_ref = dirs[d][:4]
            r0 = pl.multiple_of(cs[d] * C, C)
            egs.append(jnp.exp(chunk_end_decay(g_ref[0, b, pl.ds(r0, C), :], d)))
            kq_in.append(jnp.concatenate([k_ref[b, pl.ds(r0, C), :], q_ref[b, pl.ds(r0, C), :]], axis=0))
            v_in.append(v_ref[b, pl.ds(r0, C), :])
        s_l = [[s_ref[u * H + h] for h in range(H)] for u in range(len(streams))]
        x1_l = [jnp.concatenate([jnp.dot(kq_in[u][:, h * dh:(h + 1) * dh], s_l[u][h].astype(BF16),
                                         preferred_element_type=F32) for h in range(H)], axis=1)
                for u in range(len(streams))]
        res_l, qs_l = [], []
        for u, (b, d) in enumerate(streams):
            en = en_ref[cs[d], u]
            res_l.append((bn_ref[cs[d], u] * (v_in[u].astype(F32) - x1_l[u][:C] * en)).astype(BF16))
            qs_l.append(x1_l[u][C:] * en)
        z_l = [jnp.dot(wq_ref[cs[d], u], _block_diag(res, H), preferred_element_type=F32)
               for (u, (b, d)), res in zip(enumerate(streams), res_l)]
        for u, (b, d) in enumerate(streams):
            z = z_l[u]
            for h in range(H):
                s_ref[u * H + h] = s_l[u][h] * egs[u][:, h:h + 1] + z[:dh, h * dh:(h + 1) * dh]
            r0 = pl.multiple_of(cs[d] * C, C)
            dirs[d][5][b, pl.ds(r0, C), :] = qs_l[u] + z[dh:]
        return carry

    lax.fori_loop(0, nc, scan_step, 0)


def _gdn_scan(qkvp, gates, gcr):
    b, seq, _ = qkvp.shape
    tt = TT_SCAN
    nb = GDN_BATCH_PER_STEP if b % GDN_BATCH_PER_STEP == 0 else 1
    nt = seq // tt
    nc = tt // GDN_CHUNK
    C = GDN_CHUNK
    dh = GDN_HEAD_DIM
    H = GDN_HEADS
    nstream = 2 * nb

    def dir_specs(d):
        tmap = (lambda i: i) if d == 0 else (lambda i: nt - 1 - i)
        return [
            pl.BlockSpec((nb, tt, GDN_WIDTH), lambda bi, i: (bi, tmap(i), 0)),
            pl.BlockSpec((nb, tt, GDN_WIDTH), lambda bi, i: (bi, tmap(i), 1)),
            pl.BlockSpec((nb, tt, GDN_WIDTH), lambda bi, i: (bi, tmap(i), 2)),
            pl.BlockSpec((1, nb, tt, LANES), lambda bi, i: (d, bi, tmap(i), 0)),
            pl.BlockSpec((1, nb, nc, 8, H * C), lambda bi, i: (d, bi, tmap(i), 0, 0)),
        ]

    out_sds = jax.ShapeDtypeStruct((b, seq, GDN_WIDTH), F32)
    return pl.pallas_call(
        functools.partial(_gdn_scan_kernel, tt=tt, nb=nb),
        grid=(b // nb, nt),
        in_specs=dir_specs(0) + dir_specs(1),
        out_specs=[pl.BlockSpec((nb, tt, GDN_WIDTH), lambda bi, i: (bi, i, 0)),
                   pl.BlockSpec((nb, tt, GDN_WIDTH), lambda bi, i: (bi, nt - 1 - i, 0))],
        out_shape=[out_sds, out_sds],
        scratch_shapes=[
            pltpu.VMEM((nstream * H, dh, dh), F32),
            pltpu.VMEM((nc, nstream, C, H * dh), F32),
            pltpu.VMEM((nc, nstream, C, H * dh), F32),
            pltpu.VMEM((nc, nstream, dh + C, H * C), BF16),
        ],
        compiler_params=_cparams(("parallel", "arbitrary")),
        name="gdn_scan",
    )(qkvp, qkvp, qkvp, gates, gcr, qkvp, qkvp, qkvp, gates, gcr)


def _na_kernel(q_ref, k_ref, v_ref, bm_ref, g_ref, o_ref, *, rows):
    W = GRID_W
    band = NA_WIN_R * W
    hd = NA_HEAD_DIM
    lane_q = lax.broadcasted_iota(jnp.int32, (W, LANES), 1)
    first = lane_q < hd
    scale = jnp.asarray(hd ** -0.5, BF16)

    nt_dims = (((1,), (1,)), ((), ()))

    def row_group(gi, carry):
        rs = [gi * NA_ROWS_PER_STEP + j for j in range(NA_ROWS_PER_STEP)]
        r0s = [jnp.clip(r - NA_WIN_R // 2, 0, rows - NA_WIN_R) for r in rs]
        qs_l, kb_l, vb_l = [], [], []
        for r, r0 in zip(rs, r0s):
            q2 = q_ref[0, pl.ds(pl.multiple_of(r * W, W), W), :] * scale
            zero = jnp.zeros_like(q2)
            qs_l.append(jnp.concatenate([jnp.where(first, q2, zero), jnp.where(first, zero, q2)], axis=0))
            kb_l.append(k_ref[0, pl.ds(pl.multiple_of(r0 * W, W), band), :])
            vb_l.append(v_ref[0, pl.ds(pl.multiple_of(r0 * W, W), band), :])
        s_l = [lax.dot_general(qs, kb, nt_dims, preferred_element_type=F32) for qs, kb in zip(qs_l, kb_l)]
        s_l = [s + bm_ref[0, r - r0] for s, r, r0 in zip(s_l, rs, r0s)]
        m_l = [jnp.max(s, axis=-1, keepdims=True) for s in s_l]
        p_l = [jnp.exp(s - m) for s, m in zip(s_l, m_l)]
        l_l = [jnp.sum(p, axis=-1, keepdims=True) for p in p_l]
        pb_l = [p.astype(BF16) for p in p_l]
        ob_l = [jnp.dot(pb, vb, preferred_element_type=F32) for pb, vb in zip(pb_l, vb_l)]
        for r, ob, l in zip(rs, ob_l, l_l):
            o = jnp.where(first, ob[:W] / l[:W], ob[W:] / l[W:])
            sq = o * o
            ms0 = jnp.sum(jnp.where(first, sq, 0.0), axis=-1, keepdims=True)
            ms1 = jnp.sum(jnp.where(first, 0.0, sq), axis=-1, keepdims=True)
            ms = jnp.where(first, ms0, ms1) * (1.0 / hd)
            o_ref[0, pl.ds(pl.multiple_of(r * W, W), W), :] = (o * lax.rsqrt(ms + RMS_EPS) * g_ref[...]).astype(BF16)
        return carry

    lax.fori_loop(0, rows // NA_ROWS_PER_STEP, row_group, 0)


def _na(h_na, bias_tab, g_row):
    b, seq, _ = h_na.shape
    rows = seq // GRID_W
    assert rows >= NA_WIN_R
    npair = NA_WIDTH // LANES
    band = NA_WIN_R * GRID_W
    return pl.pallas_call(
        functools.partial(_na_kernel, rows=rows),
        grid=(b, npair),
        in_specs=[
            pl.BlockSpec((1, seq, LANES), lambda bi, p: (bi, 0, p)),
            pl.BlockSpec((1, seq, LANES), lambda bi, p: (bi, 0, npair + p)),
            pl.BlockSpec((1, seq, LANES), lambda bi, p: (bi, 0, 2 * npair + p)),
            pl.BlockSpec((1, NA_WIN_R, 2 * GRID_W, band), lambda bi, p: (p, 0, 0, 0)),
            pl.BlockSpec((1, LANES), lambda bi, p: (0, 0)),
        ],
        out_specs=pl.BlockSpec((1, seq, LANES), lambda bi, p: (bi, 0, p)),
        out_shape=jax.ShapeDtypeStruct((b, seq, NA_WIDTH), BF16),
        compiler_params=_cparams(("parallel", "parallel")),
        name="natten",
    )(h_na, h_na, h_na, bias_tab, g_row)


def _na_bias_table(rpb_l):
    W = GRID_W
    vi = np.arange(NA_WIN_R)
    kr = np.arange(NA_WIN_R)
    dr = kr[None, :] - vi[:, None] + NA_WIN_R - 1
    qc = np.arange(W)
    kc = np.arange(W)
    win_start = np.clip(qc - NA_WIN_C // 2, 0, W - NA_WIN_C)
    in_win = (kc[None, :] >= win_start[:, None]) & (kc[None, :] < win_start[:, None] + NA_WIN_C)
    dc = kc[None, :] - qc[:, None] + NA_WIN_C - 1
    rsel = (dr[:, :, None] == np.arange(2 * NA_WIN_R - 1)).astype(np.float32)
    csel = ((dc[:, :, None] == np.arange(2 * NA_WIN_C - 1)) & in_win[:, :, None]).astype(np.float32)
    rpb_pairs = rpb_l.astype(F32).reshape(NA_HEADS // 2, 2, 2 * NA_WIN_R - 1, 2 * NA_WIN_C - 1)
    tab = jnp.einsum("pjab,vka,qcb->pvjqkc", rpb_pairs, rsel, csel,
                     precision=lax.Precision.HIGHEST)
    tab = jnp.where(jnp.asarray(in_win)[:, None, :], tab, -jnp.inf)
    return tab.reshape(NA_HEADS // 2, NA_WIN_R, 2 * W, NA_WIN_R * W)


def _mix_ffn_kernel(x_ref, of_ref, ob_ref, z_ref, ona_ref, gg_ref, wo_ref, l1g_ref, l1b_ref,
                    w1_ref, b1_ref, w2_ref, b2_ref, l2g_ref, l2b_ref, out_ref):
    dh = GDN_HEAD_DIM
    tm = x_ref.shape[0]
    d_ff = w1_ref.shape[1]
    rows = [slice(s * (tm // FFN_SUBTILES), (s + 1) * (tm // FFN_SUBTILES)) for s in range(FFN_SUBTILES)]

    def gated(rs):
        o = of_ref[rs, :] + ob_ref[rs, :]
        z = z_ref[rs, :].astype(F32)
        gate = z * _sigmoid(z)
        parts = []
        for h in range(GDN_HEADS):
            cols = slice(h * dh, (h + 1) * dh)
            oh = o[:, cols]
            ms = jnp.mean(oh * oh, axis=-1, keepdims=True)
            parts.append((oh * lax.rsqrt(ms + RMS_EPS) * gg_ref[:, cols] * gate[:, cols]).astype(BF16))
        return jnp.concatenate(parts, axis=-1)

    og_l = [gated(rs) for rs in rows]
    mix_l = [jnp.dot(og, wo_ref[0:GDN_WIDTH, :], preferred_element_type=F32)
             + jnp.dot(ona_ref[rs, :], wo_ref[GDN_WIDTH:, :], preferred_element_type=F32)
             for og, rs in zip(og_l, rows)]
    x1_l = [_layer_norm(DEEPNORM_ALPHA * x_ref[rs, :] + mix, l1g_ref[...], l1b_ref[...])
            for mix, rs in zip(mix_l, rows)]
    x1b_l = [x1.astype(BF16) for x1 in x1_l]
    acc_l = [jnp.zeros(x1.shape, F32) for x1 in x1_l]
    for f in range(d_ff // FF_CHUNK):
        fs = slice(f * FF_CHUNK, (f + 1) * FF_CHUNK)
        hf_l = [jnp.dot(x1b, w1_ref[:, fs], preferred_element_type=F32) + b1_ref[:, fs] for x1b in x1b_l]
        hf_l = [jnp.square(jnp.maximum(hf, 0.0)).astype(BF16) for hf in hf_l]
        acc_l = [acc + jnp.dot(hf, w2_ref[fs, :], preferred_element_type=F32) for acc, hf in zip(acc_l, hf_l)]
    for rs, x1, acc in zip(rows, x1_l, acc_l):
        y = DEEPNORM_ALPHA * x1 + (acc + b2_ref[...])
        out_ref[rs, :] = _layer_norm(y, l2g_ref[...], l2b_ref[...])


def _mix_ffn(x2d, o_f, o_b, z, ona, gg, wo, l1g, l1b, w1, b1, w2, b2, l2g, l2b):
    bt, dm = x2d.shape
    d_ff = w1.shape[1]
    tm = TM_FFN
    row = lambda i: (i, 0)
    return pl.pallas_call(
        _mix_ffn_kernel,
        grid=(bt // tm,),
        in_specs=[
            pl.BlockSpec((tm, dm), row),
            pl.BlockSpec((tm, GDN_WIDTH), row),
            pl.BlockSpec((tm, GDN_WIDTH), row),
            pl.BlockSpec((tm, GDN_WIDTH), row),
            pl.BlockSpec((tm, NA_WIDTH), row),
            _const_spec((1, GDN_WIDTH)),
            _const_spec((dm, dm)),
            _const_spec((1, dm)), _const_spec((1, dm)),
            _const_spec((dm, d_ff)), _const_spec((1, d_ff)),
            _const_spec((d_ff, dm)), _const_spec((1, dm)),
            _const_spec((1, dm)), _const_spec((1, dm)),
        ],
        out_specs=pl.BlockSpec((tm, dm), row),
        out_shape=jax.ShapeDtypeStruct((bt, dm), F32),
        compiler_params=_cparams(("parallel",)),
        name="mix_ffn",
    )(x2d, o_f, o_b, z, ona, gg, wo, l1g, l1b, w1, b1, w2, b2, l2g, l2b)


def _split_w_in(w_l):
    dm = w_l.shape[0]
    h = GDN_HEADS
    a0 = 4 * GDN_WIDTH
    b0 = a0 + 2 * h
    na0 = b0 + 2 * h
    pad = jnp.zeros((dm, LANES - 2 * h), w_l.dtype)
    gates = jnp.concatenate([w_l[:, a0:a0 + h], w_l[:, b0:b0 + h], pad,
                             w_l[:, a0 + h:a0 + 2 * h], w_l[:, b0 + h:b0 + 2 * h], pad], axis=1)
    return tuple(w.astype(BF16) for w in (w_l[:, :C_QKV], w_l[:, C_QKV:a0], gates, w_l[:, na0:]))


def _gate_rows(p):
    return jnp.pad(p.astype(F32), ((0, 0), (0, LANES - p.shape[1])))


def kernel(x, ln_in_g, ln_in_b, w_in, conv_w, a_log, dt_bias, gdn_norm_g, rpb, na_norm_g, w_out,
           ln1_g, ln1_b, w1, b1, w2, b2, ln2_g, ln2_b):
    B, T, dm = x.shape
    bt = B * T
    nchunks = T // GDN_CHUNK
    row = lambda v: v.reshape(1, -1).astype(F32)
    xs = x.reshape(bt, dm)
    for l in range(DEPTH):
        w_l = _split_w_in(w_in[l])
        conv8 = jnp.pad(conv_w[l].astype(F32), ((0, 8 - CONV_WIDTH), (0, 0)))
        outs = _inproj(xs, row(ln_in_g), row(ln_in_b), w_l, conv8, _gate_rows(a_log[l]), _gate_rows(dt_bias[l]),
                       apply_ln=(l == 0), seq=T)
        if l == 0:
            xs, qkvp, z, gates, gcr, h_na = outs
        else:
            qkvp, z, gates, gcr, h_na = outs
        qkvp = qkvp.reshape(B, T, C_QKV)
        gates4 = gates.reshape(2, B, T, LANES)
        gcr = gcr.reshape(2, B, nchunks, 8, GDN_HEADS * GDN_CHUNK)
        o_f, o_b = _gdn_scan(qkvp, gates4, gcr)
        ona = _na(h_na.reshape(B, T, C_NA), _na_bias_table(rpb[l]),
                  jnp.tile(na_norm_g[l].astype(F32), LANES // NA_HEAD_DIM).reshape(1, LANES))
        xs = _mix_ffn(xs, o_f.reshape(bt, GDN_WIDTH), o_b.reshape(bt, GDN_WIDTH), z, ona.reshape(bt, NA_WIDTH),
                      jnp.tile(gdn_norm_g[l].astype(F32), GDN_HEADS).reshape(1, GDN_WIDTH),
                      w_out[l].astype(BF16), row(ln1_g[l]), row(ln1_b[l]),
                      w1[l].astype(BF16), row(b1[l]), w2[l].astype(BF16), row(b2[l]),
                      row(ln2_g[l]), row(ln2_b[l]))
    return xs.reshape(B, T, dm)
```

```python
import functools

import jax
import jax.numpy as jnp
import numpy as np
from jax import lax
from jax.experimental import pallas as pl
from jax.experimental.pallas import tpu as pltpu

F32 = jnp.float32
BF16 = jnp.bfloat16

GRID_W = 64
GDN_HEAD_DIM = 128
GDN_HEADS = 4
GDN_WIDTH = GDN_HEADS * GDN_HEAD_DIM
NA_HEAD_DIM = 64
NA_HEADS = 8
NA_WIDTH = NA_HEADS * NA_HEAD_DIM
CONV_WIDTH = 5
GDN_CHUNK = 64
NA_WIN_R = 8
NA_WIN_C = 16
DEPTH = 2
DEEPNORM_ALPHA = (2 * DEPTH) ** 0.25
LN_EPS = 1e-5
RMS_EPS = 1e-6

LANES = 128
SUBLANES = 8
VMEM_LIMIT = 56 * 1024 * 1024

TM_PROJ = 512
TM_FFN = 512
TT_SCAN = 256
FF_CHUNK = 2048
NA_ROWS_PER_STEP = 8
FFN_SUBTILES = 2
GDN_BATCH_PER_STEP = 4
GDN_PREP_CHUNKS = 2

C_QKV = 3 * GDN_WIDTH
C_Z = GDN_WIDTH
C_NA = 3 * NA_WIDTH


def _cparams(sem):
    return pltpu.CompilerParams(dimension_semantics=sem, vmem_limit_bytes=VMEM_LIMIT)


def _const_spec(shape):
    nd = len(shape)
    return pl.BlockSpec(shape, lambda *_: (0,) * nd, pipeline_mode=pl.Buffered(1))


def _layer_norm(y, g, b):
    mu = jnp.mean(y, axis=-1, keepdims=True)
    yc = y - mu
    var = jnp.mean(yc * yc, axis=-1, keepdims=True)
    return yc * lax.rsqrt(var + LN_EPS) * g + b


def _sigmoid(x):
    return 1.0 / (1.0 + jnp.exp(-x))


def _chunk_cumsum(g, tpos, reverse):
    n = g.shape[0]
    s = 1
    while s < GDN_CHUNK:
        if reverse:
            g = g + jnp.where(tpos < GDN_CHUNK - s, pltpu.roll(g, n - s, 0), 0.0)
        else:
            g = g + jnp.where(tpos >= s, pltpu.roll(g, s, 0), 0.0)
        s *= 2
    return g


CONV_HALO = SUBLANES


def _inproj_kernel(x_ref, xp_ref, xn_ref, lng_ref, lnb_ref, wq_ref, wz_ref, wg_ref, wna_ref, cw_ref, alog_ref, dtb_ref,
                   *refs, apply_ln, tiles_per_seq):
    if apply_ln:
        xo_ref, qkv_ref, z_ref, gates_ref, gcr_ref, na_ref, hq_ref = refs
    else:
        qkv_ref, z_ref, gates_ref, gcr_ref, na_ref, hq_ref = refs
    i = pl.program_id(0)
    tm = x_ref.shape[0]
    halo = CONV_HALO
    x, x_prev, x_next = x_ref[...], xp_ref[...], xn_ref[...]
    if apply_ln:
        x = _layer_norm(x, lng_ref[...], lnb_ref[...])
        x_prev = _layer_norm(x_prev, lng_ref[...], lnb_ref[...])
        x_next = _layer_norm(x_next, lng_ref[...], lnb_ref[...])
        xo_ref[...] = x
    xb = x.astype(BF16)
    x_ext = jnp.concatenate([x_prev, x, x_next], axis=0).astype(BF16)

    pos = i % tiles_per_seq
    keep_prev = jnp.where(pos == 0, 0.0, 1.0).astype(F32)
    keep_next = jnp.where(pos == tiles_per_seq - 1, 0.0, 1.0).astype(F32)
    hq = jnp.dot(x_ext, wq_ref[...], preferred_element_type=F32)
    hq_ref[0:halo, :] = hq[0:halo] * keep_prev
    hq_ref[halo:tm + halo, :] = hq[halo:tm + halo]
    hq_ref[tm + halo:tm + 2 * halo, :] = hq[tm + halo:tm + 2 * halo] * keep_next
    z_ref[...] = jnp.dot(xb, wz_ref[...], preferred_element_type=F32).astype(BF16)
    na_ref[...] = jnp.dot(xb, wna_ref[...], preferred_element_type=F32).astype(BF16)
    hab2 = jnp.dot(xb, wg_ref[...], preferred_element_type=F32)

    first_tap = halo - CONV_WIDTH // 2
    for j in range(C_QKV // LANES):
        cols = slice(j * LANES, (j + 1) * LANES)
        y = hq_ref[first_tap:first_tap + tm, cols] * cw_ref[0:1, cols]
        for tap in range(1, CONV_WIDTH):
            y = y + hq_ref[first_tap + tap:first_tap + tap + tm, cols] * cw_ref[tap:tap + 1, cols]
        y = y * _sigmoid(y)
        if j < 2 * GDN_HEADS:
            fac = lax.rsqrt(jnp.sum(y * y, axis=-1, keepdims=True) + RMS_EPS)
            if j < GDN_HEADS:
                fac = fac * (GDN_HEAD_DIM ** -0.5)
            y = y * fac
        qkv_ref[:, cols] = y.astype(BF16)

    tpos = lax.broadcasted_iota(jnp.int32, (tm, LANES), 0) & (GDN_CHUNK - 1)
    lane = lax.broadcasted_iota(jnp.int32, (tm, LANES), 1)
    for d in range(2):
        hab = hab2[:, d * LANES:(d + 1) * LANES]
        sp_in = hab + dtb_ref[d:d + 1, :]
        softplus = jnp.maximum(sp_in, 0.0) + jnp.log1p(jnp.exp(-jnp.abs(sp_in)))
        g = -jnp.exp(alog_ref[d:d + 1, :]) * softplus
        g = jnp.where(lane < GDN_HEADS, g, 0.0)
        gc = _chunk_cumsum(g, tpos, reverse=(d == 1))
        gates_ref[d] = jnp.where(lane < GDN_HEADS, gc, _sigmoid(hab))
        gct = gc.T
        pad_rows = jnp.zeros((SUBLANES - 1, GDN_HEADS * GDN_CHUNK), F32)
        for c in range(tm // GDN_CHUNK):
            row_c = jnp.concatenate([gct[h:h + 1, c * GDN_CHUNK:(c + 1) * GDN_CHUNK] for h in range(GDN_HEADS)], axis=1)
            gcr_ref[d, c] = jnp.concatenate([row_c, pad_rows], axis=0)


def _inproj(x2d, lng, lnb, ws, conv_w8, alog_rows, dtb_rows, apply_ln, seq):
    bt, dm = x2d.shape
    tm = TM_PROJ
    grid = (bt // tm,)
    row = lambda i: (i, 0)
    hb = tm // CONV_HALO
    last_hb = bt // CONV_HALO - 1
    in_specs = [
        pl.BlockSpec((tm, dm), row),
        pl.BlockSpec((CONV_HALO, dm), lambda i: (jnp.maximum(i * hb - 1, 0), 0)),
        pl.BlockSpec((CONV_HALO, dm), lambda i: (jnp.minimum((i + 1) * hb, last_hb), 0)),
        _const_spec((1, dm)), _const_spec((1, dm)),
        _const_spec((dm, C_QKV)), _const_spec((dm, C_Z)), _const_spec((dm, 2 * LANES)), _const_spec((dm, C_NA)),
        _const_spec((SUBLANES, C_QKV)),
        _const_spec((2, LANES)), _const_spec((2, LANES)),
    ]
    out_shape = [
        jax.ShapeDtypeStruct((bt, C_QKV), BF16),
        jax.ShapeDtypeStruct((bt, C_Z), BF16),
        jax.ShapeDtypeStruct((2, bt, LANES), F32),
        jax.ShapeDtypeStruct((2, bt // GDN_CHUNK, SUBLANES, GDN_HEADS * GDN_CHUNK), F32),
        jax.ShapeDtypeStruct((bt, C_NA), BF16),
    ]
    out_specs = [
        pl.BlockSpec((tm, C_QKV), row),
        pl.BlockSpec((tm, C_Z), row),
        pl.BlockSpec((2, tm, LANES), lambda i: (0, i, 0)),
        pl.BlockSpec((2, tm // GDN_CHUNK, SUBLANES, GDN_HEADS * GDN_CHUNK), lambda i: (0, i, 0, 0)),
        pl.BlockSpec((tm, C_NA), row),
    ]
    if apply_ln:
        out_shape = [jax.ShapeDtypeStruct((bt, dm), F32)] + out_shape
        out_specs = [pl.BlockSpec((tm, dm), row)] + out_specs
    return pl.pallas_call(
        functools.partial(_inproj_kernel, apply_ln=apply_ln, tiles_per_seq=seq // tm),
        grid=grid, in_specs=in_specs, out_specs=out_specs, out_shape=out_shape,
        scratch_shapes=[pltpu.VMEM((tm + 2 * CONV_HALO, C_QKV), F32)],
        compiler_params=_cparams(("parallel",)),
        name="inproj_ln" if apply_ln else "inproj",
    )(x2d, x2d, x2d, lng, lnb, *ws, conv_w8, alog_rows, dtb_rows)


def _block_diag(x, nblk):
    w = x.shape[1] // nblk
    blk = lax.broadcasted_iota(jnp.int32, x.shape, 1) // w
    zero = jnp.zeros_like(x)
    return jnp.concatenate([jnp.where(blk == h, x, zero) for h in range(nblk)], axis=0)


def _lane_blocks(cols, width):
    r, n = cols.shape
    return jnp.concatenate([jnp.broadcast_to(cols[:, j:j + 1], (r, width)) for j in range(n)], axis=1)


def _unit_tri_inverses(mats, nblk):
    c = mats[0].shape[0]
    ri = lax.broadcasted_iota(jnp.int32, (c, nblk * c), 0)
    ci = lax.broadcasted_iota(jnp.int32, (c, nblk * c), 1) % c
    eye = (ri == ci).astype(F32)
    ps = [-a for a in mats]
    xs = [eye + p for p in ps]
    pbs = [p.astype(BF16) for p in ps]
    ps = [jnp.dot(pb, _block_diag(pb, nblk), preferred_element_type=F32) for pb in pbs]
    m = 2
    while 2 * m < c:
        pbs = [p.astype(BF16) for p in ps]
        prods = [jnp.dot(jnp.concatenate([x.astype(BF16), pb], axis=0), _block_diag(pb, nblk),
                         preferred_element_type=F32) for x, pb in zip(xs, pbs)]
        xs = [x + prod[:c] for x, prod in zip(xs, prods)]
        ps = [prod[c:] for prod in prods]
        m *= 2
    return [x + jnp.dot(x.astype(BF16), _block_diag(p.astype(BF16), nblk), preferred_element_type=F32)
            for x, p in zip(xs, ps)]


def _gdn_scan_kernel(qf_ref, kf_ref, vf_ref, gf_ref, grf_ref, qb_ref, kb_ref, vb_ref, gb_ref, grb_ref,
                     of_ref, ob_ref, s_ref, en_ref, bn_ref, wq_ref, *, tt, nb):
    i = pl.program_id(1)
    nc = tt // GDN_CHUNK
    C = GDN_CHUNK
    dh = GDN_HEAD_DIM
    H = GDN_HEADS

    @pl.when(i == 0)
    def _():
        s_ref[...] = jnp.zeros_like(s_ref)

    ri = lax.broadcasted_iota(jnp.int32, (C, H * C), 0)
    ci = lax.broadcasted_iota(jnp.int32, (C, H * C), 1) % C
    mask_incl = (ri >= ci, ri <= ci)
    mask_strict = (ri > ci, ri < ci)
    dirs = ((qf_ref, kf_ref, vf_ref, gf_ref, grf_ref, of_ref), (qb_ref, kb_ref, vb_ref, gb_ref, grb_ref, ob_ref))
    streams = [(b, d) for b in range(nb) for d in range(2)]
    half_lo = lax.broadcasted_iota(jnp.int32, (C, dh), 1) < C
    nt_dims = (((1,), (1,)), ((), ()))
    tn_dims = (((0,), (0,)), ((), ()))

    def chunk_end_decay(gt, d):
        return gt[C - 1:C, :] if d == 0 else gt[0:1, :]

    def halve_blocks(x):
        lo = half_lo[0:x.shape[0]]
        return jnp.concatenate([jnp.where(lo, x[:, (2 * p) * dh:(2 * p + 1) * dh], x[:, (2 * p + 1) * dh:(2 * p + 2) * dh])
                                for p in range(H // 2)], axis=1)

    def prep_group(cg, carry):
        items = [(cg * GDN_PREP_CHUNKS + j, u) for j in range(GDN_PREP_CHUNKS) for u in range(len(streams))]
        lhs_l, bdk_l, kst_l, gam_l, gamb_l, edec_l = [], [], [], [], [], []
        for c, u in items:
            b, d = streams[u]
            q_ref, k_ref, _, g_ref, gr_ref, _ = dirs[d]
            r0 = pl.multiple_of(c * C, C)
            gt = g_ref[0, b, pl.ds(r0, C), :]
            gr = gr_ref[0, b, c]
            q = q_ref[b, pl.ds(r0, C), :]
            k = k_ref[b, pl.ds(r0, C), :]
            gc_n = _lane_blocks(gt[:, 0:H], dh)
            beta_n = _lane_blocks(gt[:, H:2 * H], dh)
            gend_n = _lane_blocks(chunk_end_decay(gt, d)[:, 0:H], dh)
            en_ref[c, u] = jnp.exp(gc_n)
            bn_ref[c, u] = beta_n
            gc_c = halve_blocks(gc_n)
            gam = jnp.where(mask_incl[d], jnp.exp(gc_c - gr[0:1, :]), 0.0)
            gam_l.append(gam)
            gamb_l.append(gam * halve_blocks(beta_n))
            edec_l.append(jnp.exp(halve_blocks(gend_n) - gc_c))
            lhs_l.append(jnp.concatenate([k, q], axis=0))
            bdk_l.append(_block_diag(k, H))
            kst_l.append(jnp.concatenate([k[:, h * dh:(h + 1) * dh] for h in range(H)], axis=0))
        kq_l = [lax.dot_general(lhs, bdk, nt_dims, preferred_element_type=F32)
                for lhs, bdk in zip(lhs_l, bdk_l)]
        a_l = [jnp.where(mask_strict[streams[u][1]], kq[:C] * gamb, 0.0)
               for kq, gamb, (_, u) in zip(kq_l, gamb_l, items)]
        qk_l = [(kq[C:] * gam).astype(BF16) for kq, gam in zip(kq_l, gam_l)]
        tinv_l = _unit_tri_inverses(a_l, H)
        bdt_l = [_block_diag(t.astype(BF16), H) for t in tinv_l]
        bdte_l = [_block_diag((t * edec).astype(BF16), H) for t, edec in zip(tinv_l, edec_l)]
        wc_l = [lax.dot_general(kst, bdte, tn_dims, preferred_element_type=F32)
                for kst, bdte in zip(kst_l, bdte_l)]
        qt_l = [jnp.dot(qk, bdt, preferred_element_type=F32) for qk, bdt in zip(qk_l, bdt_l)]
        for (c, u), wc, qt in zip(items, wc_l, qt_l):
            wq_ref[c, u] = jnp.concatenate([wc, qt], axis=0).astype(BF16)
        return carry

    lax.fori_loop(0, nc // GDN_PREP_CHUNKS, prep_group, 0)

    def scan_step(step, carry):
        cs = (step, nc - 1 - step)
        egs, kq_in, v_in = [], [], []
        for u, (b, d) in enumerate(streams):
            q_ref, k_ref, v_ref, g_ref = dirs[d][:4]
            r0 = pl.multiple_of(cs[d] * C, C)
            egs.append(jnp.exp(chunk_end_decay(g_ref[0, b, pl.ds(r0, C), :], d)))
            kq_in.append(jnp.concatenate([k_ref[b, pl.ds(r0, C), :], q_ref[b, pl.ds(r0, C), :]], axis=0))
            v_in.append(v_ref[b, pl.ds(r0, C), :])
        s_l = [[s_ref[u * H + h] for h in range(H)] for u in range(len(streams))]
        x1_l = [jnp.concatenate([jnp.dot(kq_in[u][:, h * dh:(h + 1) * dh], s_l[u][h].astype(BF16),
                                         preferred_element_type=F32) for h in range(H)], axis=1)
                for u in range(len(streams))]
        res_l, qs_l = [], []
        for u, (b, d) in enumerate(streams):
            en = en_ref[cs[d], u]
            res_l.append((bn_ref[cs[d], u] * (v_in[u].astype(F32) - x1_l[u][:C] * en)).astype(BF16))
            qs_l.append(x1_l[u][C:] * en)
        z_l = [jnp.dot(wq_ref[cs[d], u], _block_diag(res, H), preferred_element_type=F32)
               for (u, (b, d)), res in zip(enumerate(streams), res_l)]
        for u, (b, d) in enumerate(streams):
            z = z_l[u]
            for h in range(H):
                s_ref[u * H + h] = s_l[u][h] * egs[u][:, h:h + 1] + z[:dh, h * dh:(h + 1) * dh]
            r0 = pl.multiple_of(cs[d] * C, C)
            dirs[d][5][b, pl.ds(r0, C), :] = qs_l[u] + z[dh:]
        return carry

    lax.fori_loop(0, nc, scan_step, 0)


def _gdn_scan(qkvp, gates, gcr):
    b, seq, _ = qkvp.shape
    tt = TT_SCAN
    nb = GDN_BATCH_PER_STEP if b % GDN_BATCH_PER_STEP == 0 else 1
    nt = seq // tt
    nc = tt // GDN_CHUNK
    C = GDN_CHUNK
    dh = GDN_HEAD_DIM
    H = GDN_HEADS
    nstream = 2 * nb

    def dir_specs(d):
        tmap = (lambda i: i) if d == 0 else (lambda i: nt - 1 - i)
        return [
            pl.BlockSpec((nb, tt, GDN_WIDTH), lambda bi, i: (bi, tmap(i), 0)),
            pl.BlockSpec((nb, tt, GDN_WIDTH), lambda bi, i: (bi, tmap(i), 1)),
            pl.BlockSpec((nb, tt, GDN_WIDTH), lambda bi, i: (bi, tmap(i), 2)),
            pl.BlockSpec((1, nb, tt, LANES), lambda bi, i: (d, bi, tmap(i), 0)),
            pl.BlockSpec((1, nb, nc, SUBLANES, H * C), lambda bi, i: (d, bi, tmap(i), 0, 0)),
        ]

    out_sds = jax.ShapeDtypeStruct((b, seq, GDN_WIDTH), F32)
    return pl.pallas_call(
        functools.partial(_gdn_scan_kernel, tt=tt, nb=nb),
        grid=(b // nb, nt),
        in_specs=dir_specs(0) + dir_specs(1),
        out_specs=[pl.BlockSpec((nb, tt, GDN_WIDTH), lambda bi, i: (bi, i, 0)),
                   pl.BlockSpec((nb, tt, GDN_WIDTH), lambda bi, i: (bi, nt - 1 - i, 0))],
        out_shape=[out_sds, out_sds],
        scratch_shapes=[
            pltpu.VMEM((nstream * H, dh, dh), F32),
            pltpu.VMEM((nc, nstream, C, H * dh), F32),
            pltpu.VMEM((nc, nstream, C, H * dh), F32),
            pltpu.VMEM((nc, nstream, dh + C, H * C), BF16),
        ],
        compiler_params=_cparams(("parallel", "arbitrary")),
        name="gdn_scan",
    )(qkvp, qkvp, qkvp, gates, gcr, qkvp, qkvp, qkvp, gates, gcr)


def _na_kernel(q_ref, k_ref, v_ref, bm_ref, g_ref, o_ref, *, rows):
    W = GRID_W
    band = NA_WIN_R * W
    hd = NA_HEAD_DIM
    lane_q = lax.broadcasted_iota(jnp.int32, (W, LANES), 1)
    first = lane_q < hd
    scale = jnp.asarray(hd ** -0.5, BF16)

    nt_dims = (((1,), (1,)), ((), ()))

    def row_group(gi, carry):
        rs = [gi * NA_ROWS_PER_STEP + j for j in range(NA_ROWS_PER_STEP)]
        r0s = [jnp.clip(r - NA_WIN_R // 2, 0, rows - NA_WIN_R) for r in rs]
        qs_l, kb_l, vb_l = [], [], []
        for r, r0 in zip(rs, r0s):
            q2 = q_ref[0, pl.ds(pl.multiple_of(r * W, W), W), :] * scale
            zero = jnp.zeros_like(q2)
            qs_l.append(jnp.concatenate([jnp.where(first, q2, zero), jnp.where(first, zero, q2)], axis=0))
            kb_l.append(k_ref[0, pl.ds(pl.multiple_of(r0 * W, W), band), :])
            vb_l.append(v_ref[0, pl.ds(pl.multiple_of(r0 * W, W), band), :])
        s_l = [lax.dot_general(qs, kb, nt_dims, preferred_element_type=F32) for qs, kb in zip(qs_l, kb_l)]
        s_l = [s + bm_ref[0, r - r0] for s, r, r0 in zip(s_l, rs, r0s)]
        m_l = [jnp.max(s, axis=-1, keepdims=True) for s in s_l]
        p_l = [jnp.exp(s - m) for s, m in zip(s_l, m_l)]
        l_l = [jnp.sum(p, axis=-1, keepdims=True) for p in p_l]
        pb_l = [p.astype(BF16) for p in p_l]
        ob_l = [jnp.dot(pb, vb, preferred_element_type=F32) for pb, vb in zip(pb_l, vb_l)]
        for r, ob, l in zip(rs, ob_l, l_l):
            o = jnp.where(first, ob[:W] / l[:W], ob[W:] / l[W:])
            sq = o * o
            ms0 = jnp.sum(jnp.where(first, sq, 0.0), axis=-1, keepdims=True)
            ms1 = jnp.sum(jnp.where(first, 0.0, sq), axis=-1, keepdims=True)
            ms = jnp.where(first, ms0, ms1) * (1.0 / hd)
            o_ref[0, pl.ds(pl.multiple_of(r * W, W), W), :] = (o * lax.rsqrt(ms + RMS_EPS) * g_ref[...]).astype(BF16)
        return carry

    lax.fori_loop(0, rows // NA_ROWS_PER_STEP, row_group, 0)


def _na(h_na, bias_tab, g_row):
    b, seq, _ = h_na.shape
    rows = seq // GRID_W
    assert rows >= NA_WIN_R
    npair = NA_WIDTH // LANES
    band = NA_WIN_R * GRID_W
    return pl.pallas_call(
        functools.partial(_na_kernel, rows=rows),
        grid=(b, npair),
        in_specs=[
            pl.BlockSpec((1, seq, LANES), lambda bi, p: (bi, 0, p)),
            pl.BlockSpec((1, seq, LANES), lambda bi, p: (bi, 0, npair + p)),
            pl.BlockSpec((1, seq, LANES), lambda bi, p: (bi, 0, 2 * npair + p)),
            pl.BlockSpec((1, NA_WIN_R, 2 * GRID_W, band), lambda bi, p: (p, 0, 0, 0)),
            pl.BlockSpec((1, LANES), lambda bi, p: (0, 0)),
        ],
        out_specs=pl.BlockSpec((1, seq, LANES), lambda bi, p: (bi, 0, p)),
        out_shape=jax.ShapeDtypeStruct((b, seq, NA_WIDTH), BF16),
        compiler_params=_cparams(("parallel", "parallel")),
        name="natten",
    )(h_na, h_na, h_na, bias_tab, g_row)


def _na_bias_table(rpb_l):
    W = GRID_W
    vi = np.arange(NA_WIN_R)
    kr = np.arange(NA_WIN_R)
    dr = kr[None, :] - vi[:, None] + NA_WIN_R - 1
    qc = np.arange(W)
    kc = np.arange(W)
    win_start = np.clip(qc - NA_WIN_C // 2, 0, W - NA_WIN_C)
    in_win = (kc[None, :] >= win_start[:, None]) & (kc[None, :] < win_start[:, None] + NA_WIN_C)
    dc = kc[None, :] - qc[:, None] + NA_WIN_C - 1
    rsel = (dr[:, :, None] == np.arange(2 * NA_WIN_R - 1)).astype(np.float32)
    csel = ((dc[:, :, None] == np.arange(2 * NA_WIN_C - 1)) & in_win[:, :, None]).astype(np.float32)
    rpb_pairs = rpb_l.astype(F32).reshape(NA_HEADS // 2, 2, 2 * NA_WIN_R - 1, 2 * NA_WIN_C - 1)
    tab = jnp.einsum("pjab,vka,qcb->pvjqkc", rpb_pairs, rsel, csel,
                     precision=lax.Precision.HIGHEST)
    tab = jnp.where(jnp.asarray(in_win)[:, None, :], tab, -jnp.inf)
    return tab.reshape(NA_HEADS // 2, NA_WIN_R, 2 * W, NA_WIN_R * W)


def _mix_ffn_kernel(x_ref, of_ref, ob_ref, z_ref, ona_ref, gg_ref, wo_ref, l1g_ref, l1b_ref,
                    w1_ref, b1_ref, w2_ref, b2_ref, l2g_ref, l2b_ref, out_ref):
    dh = GDN_HEAD_DIM
    tm = x_ref.shape[0]
    d_ff = w1_ref.shape[1]
    rows = [slice(s * (tm // FFN_SUBTILES), (s + 1) * (tm // FFN_SUBTILES)) for s in range(FFN_SUBTILES)]

    def gated(rs):
        o = of_ref[rs, :] + ob_ref[rs, :]
        z = z_ref[rs, :].astype(F32)
        gate = z * _sigmoid(z)
        parts = []
        for h in range(GDN_HEADS):
            cols = slice(h * dh, (h + 1) * dh)
            oh = o[:, cols]
            ms = jnp.mean(oh * oh, axis=-1, keepdims=True)
            parts.append((oh * lax.rsqrt(ms + RMS_EPS) * gg_ref[:, cols] * gate[:, cols]).astype(BF16))
        return jnp.concatenate(parts, axis=-1)

    og_l = [gated(rs) for rs in rows]
    mix_l = [jnp.dot(og, wo_ref[0:GDN_WIDTH, :], preferred_element_type=F32)
             + jnp.dot(ona_ref[rs, :], wo_ref[GDN_WIDTH:, :], preferred_element_type=F32)
             for og, rs in zip(og_l, rows)]
    x1_l = [_layer_norm(DEEPNORM_ALPHA * x_ref[rs, :] + mix, l1g_ref[...], l1b_ref[...])
            for mix, rs in zip(mix_l, rows)]
    x1b_l = [x1.astype(BF16) for x1 in x1_l]
    acc_l = [jnp.zeros(x1.shape, F32) for x1 in x1_l]
    for f in range(d_ff // FF_CHUNK):
        fs = slice(f * FF_CHUNK, (f + 1) * FF_CHUNK)
        hf_l = [jnp.dot(x1b, w1_ref[:, fs], preferred_element_type=F32) + b1_ref[:, fs] for x1b in x1b_l]
        hf_l = [jnp.square(jnp.maximum(hf, 0.0)).astype(BF16) for hf in hf_l]
        acc_l = [acc + jnp.dot(hf, w2_ref[fs, :], preferred_element_type=F32) for acc, hf in zip(acc_l, hf_l)]
    for rs, x1, acc in zip(rows, x1_l, acc_l):
        y = DEEPNORM_ALPHA * x1 + (acc + b2_ref[...])
        out_ref[rs, :] = _layer_norm(y, l2g_ref[...], l2b_ref[...])


def _mix_ffn(x2d, o_f, o_b, z, ona, gg, wo, l1g, l1b, w1, b1, w2, b2, l2g, l2b):
    bt, dm = x2d.shape
    d_ff = w1.shape[1]
    tm = TM_FFN
    row = lambda i: (i, 0)
    return pl.pallas_call(
        _mix_ffn_kernel,
        grid=(bt // tm,),
        in_specs=[
            pl.BlockSpec((tm, dm), row),
            pl.BlockSpec((tm, GDN_WIDTH), row),
            pl.BlockSpec((tm, GDN_WIDTH), row),
            pl.BlockSpec((tm, GDN_WIDTH), row),
            pl.BlockSpec((tm, NA_WIDTH), row),
            _const_spec((1, GDN_WIDTH)),
            _const_spec((dm, dm)),
            _const_spec((1, dm)), _const_spec((1, dm)),
            _const_spec((dm, d_ff)), _const_spec((1, d_ff)),
            _const_spec((d_ff, dm)), _const_spec((1, dm)),
            _const_spec((1, dm)), _const_spec((1, dm)),
        ],
        out_specs=pl.BlockSpec((tm, dm), row),
        out_shape=jax.ShapeDtypeStruct((bt, dm), F32),
        compiler_params=_cparams(("parallel",)),
        name="mix_ffn",
    )(x2d, o_f, o_b, z, ona, gg, wo, l1g, l1b, w1, b1, w2, b2, l2g, l2b)


def _split_w_in(w_l):
    dm = w_l.shape[0]
    h = GDN_HEADS
    a0 = 4 * GDN_WIDTH
    b0 = a0 + 2 * h
    na0 = b0 + 2 * h
    pad = jnp.zeros((dm, LANES - 2 * h), w_l.dtype)
    gates = jnp.concatenate([w_l[:, a0:a0 + h], w_l[:, b0:b0 + h], pad,
                             w_l[:, a0 + h:a0 + 2 * h], w_l[:, b0 + h:b0 + 2 * h], pad], axis=1)
    return tuple(w.astype(BF16) for w in (w_l[:, :C_QKV], w_l[:, C_QKV:a0], gates, w_l[:, na0:]))


def _gate_rows(p):
    return jnp.pad(p.astype(F32), ((0, 0), (0, LANES - p.shape[1])))


def kernel(x, ln_in_g, ln_in_b, w_in, conv_w, a_log, dt_bias, gdn_norm_g, rpb, na_norm_g, w_out,
           ln1_g, ln1_b, w1, b1, w2, b2, ln2_g, ln2_b):
    B, T, dm = x.shape
    bt = B * T
    nchunks = T // GDN_CHUNK
    row = lambda v: v.reshape(1, -1).astype(F32)
    xs = x.reshape(bt, dm)
    for l in range(DEPTH):
        w_l = _split_w_in(w_in[l])
        conv8 = jnp.pad(conv_w[l].astype(F32), ((0, SUBLANES - CONV_WIDTH), (0, 0)))
        outs = _inproj(xs, row(ln_in_g), row(ln_in_b), w_l, conv8, _gate_rows(a_log[l]), _gate_rows(dt_bias[l]),
                       apply_ln=(l == 0), seq=T)
        if l == 0:
            xs, qkvp, z, gates, gcr, h_na = outs
        else:
            qkvp, z, gates, gcr, h_na = outs
        qkvp = qkvp.reshape(B, T, C_QKV)
        gates4 = gates.reshape(2, B, T, LANES)
        gcr = gcr.reshape(2, B, nchunks, SUBLANES, GDN_HEADS * GDN_CHUNK)
        o_f, o_b = _gdn_scan(qkvp, gates4, gcr)
        ona = _na(h_na.reshape(B, T, C_NA), _na_bias_table(rpb[l]),
                  jnp.tile(na_norm_g[l].astype(F32), LANES // NA_HEAD_DIM).reshape(1, LANES))
        xs = _mix_ffn(xs, o_f.reshape(bt, GDN_WIDTH), o_b.reshape(bt, GDN_WIDTH), z, ona.reshape(bt, NA_WIDTH),
                      jnp.tile(gdn_norm_g[l].astype(F32), GDN_HEADS).reshape(1, GDN_WIDTH),
                      w_out[l].astype(BF16), row(ln1_g[l]), row(ln1_b[l]),
                      w1[l].astype(BF16), row(b1[l]), w2[l].astype(BF16), row(b2[l]),
                      row(ln2_g[l]), row(ln2_b[l]))
    return xs.reshape(B, T, dm)
```

```python
import functools

import jax
import jax.numpy as jnp
import numpy as np
from jax import lax
from jax.experimental import pallas as pl
from jax.experimental.pallas import tpu as pltpu

F32 = jnp.float32
BF16 = jnp.bfloat16

GRID_W = 64
GDN_HEAD_DIM = 128
GDN_HEADS = 4
GDN_WIDTH = GDN_HEADS * GDN_HEAD_DIM
NA_HEAD_DIM = 64
NA_HEADS = 8
NA_WIDTH = NA_HEADS * NA_HEAD_DIM
CONV_WIDTH = 5
GDN_CHUNK = 64
NA_WIN_R = 8
NA_WIN_C = 16
DEPTH = 2
DEEPNORM_ALPHA = (2 * DEPTH) ** 0.25
LN_EPS = 1e-5
RMS_EPS = 1e-6

LANES = 128
SUBLANES = 8
VMEM_LIMIT = 56 * 1024 * 1024

TM_PROJ = 512
TM_FFN = 512
TT_SCAN = 256
FF_CHUNK = 2048
NA_ROWS_PER_STEP = 8
FFN_SUBTILES = 2
GDN_BATCH_PER_STEP = 4
GDN_PREP_CHUNKS = 4

C_QKV = 3 * GDN_WIDTH
C_Z = GDN_WIDTH
C_NA = 3 * NA_WIDTH


def _cparams(sem):
    return pltpu.CompilerParams(dimension_semantics=sem, vmem_limit_bytes=VMEM_LIMIT)


def _const_spec(shape):
    nd = len(shape)
    return pl.BlockSpec(shape, lambda *_: (0,) * nd, pipeline_mode=pl.Buffered(1))


def _layer_norm(y, g, b):
    mu = jnp.mean(y, axis=-1, keepdims=True)
    yc = y - mu
    var = jnp.mean(yc * yc, axis=-1, keepdims=True)
    return yc * lax.rsqrt(var + LN_EPS) * g + b


def _sigmoid(x):
    return 1.0 / (1.0 + jnp.exp(-x))


def _chunk_cumsum(g, tpos, reverse):
    n = g.shape[0]
    s = 1
    while s < GDN_CHUNK:
        if reverse:
            g = g + jnp.where(tpos < GDN_CHUNK - s, pltpu.roll(g, n - s, 0), 0.0)
        else:
            g = g + jnp.where(tpos >= s, pltpu.roll(g, s, 0), 0.0)
        s *= 2
    return g


CONV_HALO = SUBLANES


def _inproj_kernel(x_ref, xp_ref, xn_ref, lng_ref, lnb_ref, wq_ref, wz_ref, wg_ref, wna_ref, cw_ref, alog_ref, dtb_ref,
                   *refs, apply_ln, tiles_per_seq):
    if apply_ln:
        xo_ref, qkv_ref, z_ref, gates_ref, gcr_ref, na_ref, hq_ref = refs
    else:
        qkv_ref, z_ref, gates_ref, gcr_ref, na_ref, hq_ref = refs
    i = pl.program_id(0)
    tm = x_ref.shape[0]
    halo = CONV_HALO
    x, x_prev, x_next = x_ref[...], xp_ref[...], xn_ref[...]
    if apply_ln:
        x = _layer_norm(x, lng_ref[...], lnb_ref[...])
        x_prev = _layer_norm(x_prev, lng_ref[...], lnb_ref[...])
        x_next = _layer_norm(x_next, lng_ref[...], lnb_ref[...])
        xo_ref[...] = x
    xb = x.astype(BF16)
    x_ext = jnp.concatenate([x_prev, x, x_next], axis=0).astype(BF16)

    pos = i % tiles_per_seq
    keep_prev = jnp.where(pos == 0, 0.0, 1.0).astype(F32)
    keep_next = jnp.where(pos == tiles_per_seq - 1, 0.0, 1.0).astype(F32)
    hq = jnp.dot(x_ext, wq_ref[...], preferred_element_type=F32)
    hq_ref[0:halo, :] = hq[0:halo] * keep_prev
    hq_ref[halo:tm + halo, :] = hq[halo:tm + halo]
    hq_ref[tm + halo:tm + 2 * halo, :] = hq[tm + halo:tm + 2 * halo] * keep_next
    z_ref[...] = jnp.dot(xb, wz_ref[...], preferred_element_type=F32).astype(BF16)
    na_ref[...] = jnp.dot(xb, wna_ref[...], preferred_element_type=F32).astype(BF16)
    hab2 = jnp.dot(xb, wg_ref[...], preferred_element_type=F32)

    first_tap = halo - CONV_WIDTH // 2
    for j in range(C_QKV // LANES):
        cols = slice(j * LANES, (j + 1) * LANES)
        y = hq_ref[first_tap:first_tap + tm, cols] * cw_ref[0:1, cols]
        for tap in range(1, CONV_WIDTH):
            y = y + hq_ref[first_tap + tap:first_tap + tap + tm, cols] * cw_ref[tap:tap + 1, cols]
        y = y * _sigmoid(y)
        if j < 2 * GDN_HEADS:
            fac = lax.rsqrt(jnp.sum(y * y, axis=-1, keepdims=True) + RMS_EPS)
            if j < GDN_HEADS:
                fac = fac * (GDN_HEAD_DIM ** -0.5)
            y = y * fac
        qkv_ref[:, cols] = y.astype(BF16)

    tpos = lax.broadcasted_iota(jnp.int32, (tm, LANES), 0) & (GDN_CHUNK - 1)
    lane = lax.broadcasted_iota(jnp.int32, (tm, LANES), 1)
    for d in range(2):
        hab = hab2[:, d * LANES:(d + 1) * LANES]
        sp_in = hab + dtb_ref[d:d + 1, :]
        softplus = jnp.maximum(sp_in, 0.0) + jnp.log1p(jnp.exp(-jnp.abs(sp_in)))
        g = -jnp.exp(alog_ref[d:d + 1, :]) * softplus
        g = jnp.where(lane < GDN_HEADS, g, 0.0)
        gc = _chunk_cumsum(g, tpos, reverse=(d == 1))
        gates_ref[d] = jnp.where(lane < GDN_HEADS, gc, _sigmoid(hab))
        gct = gc.T
        pad_rows = jnp.zeros((SUBLANES - 1, GDN_HEADS * GDN_CHUNK), F32)
        for c in range(tm // GDN_CHUNK):
            row_c = jnp.concatenate([gct[h:h + 1, c * GDN_CHUNK:(c + 1) * GDN_CHUNK] for h in range(GDN_HEADS)], axis=1)
            gcr_ref[d, c] = jnp.concatenate([row_c, pad_rows], axis=0)


def _inproj(x2d, lng, lnb, ws, conv_w8, alog_rows, dtb_rows, apply_ln, seq):
    bt, dm = x2d.shape
    tm = TM_PROJ
    grid = (bt // tm,)
    row = lambda i: (i, 0)
    hb = tm // CONV_HALO
    last_hb = bt // CONV_HALO - 1
    in_specs = [
        pl.BlockSpec((tm, dm), row),
        pl.BlockSpec((CONV_HALO, dm), lambda i: (jnp.maximum(i * hb - 1, 0), 0)),
        pl.BlockSpec((CONV_HALO, dm), lambda i: (jnp.minimum((i + 1) * hb, last_hb), 0)),
        _const_spec((1, dm)), _const_spec((1, dm)),
        _const_spec((dm, C_QKV)), _const_spec((dm, C_Z)), _const_spec((dm, 2 * LANES)), _const_spec((dm, C_NA)),
        _const_spec((SUBLANES, C_QKV)),
        _const_spec((2, LANES)), _const_spec((2, LANES)),
    ]
    out_shape = [
        jax.ShapeDtypeStruct((bt, C_QKV), BF16),
        jax.ShapeDtypeStruct((bt, C_Z), BF16),
        jax.ShapeDtypeStruct((2, bt, LANES), F32),
        jax.ShapeDtypeStruct((2, bt // GDN_CHUNK, SUBLANES, GDN_HEADS * GDN_CHUNK), F32),
        jax.ShapeDtypeStruct((bt, C_NA), BF16),
    ]
    out_specs = [
        pl.BlockSpec((tm, C_QKV), row),
        pl.BlockSpec((tm, C_Z), row),
        pl.BlockSpec((2, tm, LANES), lambda i: (0, i, 0)),
        pl.BlockSpec((2, tm // GDN_CHUNK, SUBLANES, GDN_HEADS * GDN_CHUNK), lambda i: (0, i, 0, 0)),
        pl.BlockSpec((tm, C_NA), row),
    ]
    if apply_ln:
        out_shape = [jax.ShapeDtypeStruct((bt, dm), F32)] + out_shape
        out_specs = [pl.BlockSpec((tm, dm), row)] + out_specs
    return pl.pallas_call(
        functools.partial(_inproj_kernel, apply_ln=apply_ln, tiles_per_seq=seq // tm),
        grid=grid, in_specs=in_specs, out_specs=out_specs, out_shape=out_shape,
        scratch_shapes=[pltpu.VMEM((tm + 2 * CONV_HALO, C_QKV), F32)],
        compiler_params=_cparams(("parallel",)),
        name="inproj_ln" if apply_ln else "inproj",
    )(x2d, x2d, x2d, lng, lnb, *ws, conv_w8, alog_rows, dtb_rows)


def _block_diag(x, nblk):
    w = x.shape[1] // nblk
    blk = lax.broadcasted_iota(jnp.int32, x.shape, 1) // w
    zero = jnp.zeros_like(x)
    return jnp.concatenate([jnp.where(blk == h, x, zero) for h in range(nblk)], axis=0)


def _lane_blocks(cols, width):
    r, n = cols.shape
    return jnp.concatenate([jnp.broadcast_to(cols[:, j:j + 1], (r, width)) for j in range(n)], axis=1)


def _unit_tri_inverses(mats, nblk):
    c = mats[0].shape[0]
    ri = lax.broadcasted_iota(jnp.int32, (c, nblk * c), 0)
    ci = lax.broadcasted_iota(jnp.int32, (c, nblk * c), 1) % c
    eye = (ri == ci).astype(F32)
    ps = [-a for a in mats]
    xs = [eye + p for p in ps]
    pbs = [p.astype(BF16) for p in ps]
    ps = [jnp.dot(pb, _block_diag(pb, nblk), preferred_element_type=F32) for pb in pbs]
    m = 2
    while 2 * m < c:
        pbs = [p.astype(BF16) for p in ps]
        prods = [jnp.dot(jnp.concatenate([x.astype(BF16), pb], axis=0), _block_diag(pb, nblk),
                         preferred_element_type=F32) for x, pb in zip(xs, pbs)]
        xs = [x + prod[:c] for x, prod in zip(xs, prods)]
        ps = [prod[c:] for prod in prods]
        m *= 2
    return [x + jnp.dot(x.astype(BF16), _block_diag(p.astype(BF16), nblk), preferred_element_type=F32)
            for x, p in zip(xs, ps)]


def _gdn_scan_kernel(qf_ref, kf_ref, vf_ref, gf_ref, grf_ref, qb_ref, kb_ref, vb_ref, gb_ref, grb_ref,
                     of_ref, ob_ref, s_ref, en_ref, bn_ref, wq_ref, *, tt, nb):
    i = pl.program_id(1)
    nc = tt // GDN_CHUNK
    C = GDN_CHUNK
    dh = GDN_HEAD_DIM
    H = GDN_HEADS

    @pl.when(i == 0)
    def _():
        s_ref[...] = jnp.zeros_like(s_ref)

    ri = lax.broadcasted_iota(jnp.int32, (C, H * C), 0)
    ci = lax.broadcasted_iota(jnp.int32, (C, H * C), 1) % C
    mask_incl = (ri >= ci, ri <= ci)
    mask_strict = (ri > ci, ri < ci)
    dirs = ((qf_ref, kf_ref, vf_ref, gf_ref, grf_ref, of_ref), (qb_ref, kb_ref, vb_ref, gb_ref, grb_ref, ob_ref))
    streams = [(b, d) for b in range(nb) for d in range(2)]
    half_lo = lax.broadcasted_iota(jnp.int32, (C, dh), 1) < C
    nt_dims = (((1,), (1,)), ((), ()))
    tn_dims = (((0,), (0,)), ((), ()))

    def chunk_end_decay(gt, d):
        return gt[C - 1:C, :] if d == 0 else gt[0:1, :]

    def halve_blocks(x):
        lo = half_lo[0:x.shape[0]]
        return jnp.concatenate([jnp.where(lo, x[:, (2 * p) * dh:(2 * p + 1) * dh], x[:, (2 * p + 1) * dh:(2 * p + 2) * dh])
                                for p in range(H // 2)], axis=1)

    def prep_group(cg, carry):
        items = [(cg * GDN_PREP_CHUNKS + j, u) for j in range(GDN_PREP_CHUNKS) for u in range(len(streams))]
        lhs_l, bdk_l, kst_l, gam_l, gamb_l, edec_l = [], [], [], [], [], []
        for c, u in items:
            b, d = streams[u]
            q_ref, k_ref, _, g_ref, gr_ref, _ = dirs[d]
            r0 = pl.multiple_of(c * C, C)
            gt = g_ref[0, b, pl.ds(r0, C), :]
            gr = gr_ref[0, b, c]
            q = q_ref[b, pl.ds(r0, C), :]
            k = k_ref[b, pl.ds(r0, C), :]
            gc_n = _lane_blocks(gt[:, 0:H], dh)
            beta_n = _lane_blocks(gt[:, H:2 * H], dh)
            gend_n = _lane_blocks(chunk_end_decay(gt, d)[:, 0:H], dh)
            en_ref[c, u] = jnp.exp(gc_n)
            bn_ref[c, u] = beta_n
            gc_c = halve_blocks(gc_n)
            gam = jnp.where(mask_incl[d], jnp.exp(gc_c - gr[0:1, :]), 0.0)
            gam_l.append(gam)
            gamb_l.append(gam * halve_blocks(beta_n))
            edec_l.append(jnp.exp(halve_blocks(gend_n) - gc_c))
            lhs_l.append(jnp.concatenate([k, q], axis=0))
            bdk_l.append(_block_diag(k, H))
            kst_l.append(jnp.concatenate([k[:, h * dh:(h + 1) * dh] for h in range(H)], axis=0))
        kq_l = [lax.dot_general(lhs, bdk, nt_dims, preferred_element_type=F32)
                for lhs, bdk in zip(lhs_l, bdk_l)]
        a_l = [jnp.where(mask_strict[streams[u][1]], kq[:C] * gamb, 0.0)
               for kq, gamb, (_, u) in zip(kq_l, gamb_l, items)]
        qk_l = [(kq[C:] * gam).astype(BF16) for kq, gam in zip(kq_l, gam_l)]
        tinv_l = _unit_tri_inverses(a_l, H)
        bdt_l = [_block_diag(t.astype(BF16), H) for t in tinv_l]
        bdte_l = [_block_diag((t * edec).astype(BF16), H) for t, edec in zip(tinv_l, edec_l)]
        wc_l = [lax.dot_general(kst, bdte, tn_dims, preferred_element_type=F32)
                for kst, bdte in zip(kst_l, bdte_l)]
        qt_l = [jnp.dot(qk, bdt, preferred_element_type=F32) for qk, bdt in zip(qk_l, bdt_l)]
        for (c, u), wc, qt in zip(items, wc_l, qt_l):
            wq_ref[c, u] = jnp.concatenate([wc, qt], axis=0).astype(BF16)
        return carry

    lax.fori_loop(0, nc // GDN_PREP_CHUNKS, prep_group, 0)

    def scan_step(step, carry):
        cs = (step, nc - 1 - step)
        egs, kq_in, v_in = [], [], []
        for u, (b, d) in enumerate(streams):
            q_ref, k_ref, v_ref, g_ref = dirs[d][:4]
            r0 = pl.multiple_of(cs[d] * C, C)
            egs.append(jnp.exp(chunk_end_decay(g_ref[0, b, pl.ds(r0, C), :], d)))
            kq_in.append(jnp.concatenate([k_ref[b, pl.ds(r0, C), :], q_ref[b, pl.ds(r0, C), :]], axis=0))
            v_in.append(v_ref[b, pl.ds(r0, C), :])
        s_l = [[s_ref[u * H + h] for h in range(H)] for u in range(len(streams))]
        x1_l = [jnp.concatenate([jnp.dot(kq_in[u][:, h * dh:(h + 1) * dh], s_l[u][h].astype(BF16),
                                         preferred_element_type=F32) for h in range(H)], axis=1)
                for u in range(len(streams))]
        res_l, qs_l = [], []
        for u, (b, d) in enumerate(streams):
            en = en_ref[cs[d], u]
            res_l.append((bn_ref[cs[d], u] * (v_in[u].astype(F32) - x1_l[u][:C] * en)).astype(BF16))
            qs_l.append(x1_l[u][C:] * en)
        z_l = [jnp.dot(wq_ref[cs[d], u], _block_diag(res, H), preferred_element_type=F32)
               for (u, (b, d)), res in zip(enumerate(streams), res_l)]
        for u, (b, d) in enumerate(streams):
            z = z_l[u]
            for h in range(H):
                s_ref[u * H + h] = s_l[u][h] * egs[u][:, h:h + 1] + z[:dh, h * dh:(h + 1) * dh]
            r0 = pl.multiple_of(cs[d] * C, C)
            dirs[d][5][b, pl.ds(r0, C), :] = qs_l[u] + z[dh:]
        return carry

    lax.fori_loop(0, nc, scan_step, 0)


def _gdn_scan(qkvp, gates, gcr):
    b, seq, _ = qkvp.shape
    tt = TT_SCAN
    nb = GDN_BATCH_PER_STEP if b % GDN_BATCH_PER_STEP == 0 else 1
    nt = seq // tt
    nc = tt // GDN_CHUNK
    C = GDN_CHUNK
    dh = GDN_HEAD_DIM
    H = GDN_HEADS
    nstream = 2 * nb

    def dir_specs(d):
        tmap = (lambda i: i) if d == 0 else (lambda i: nt - 1 - i)
        return [
            pl.BlockSpec((nb, tt, GDN_WIDTH), lambda bi, i: (bi, tmap(i), 0)),
            pl.BlockSpec((nb, tt, GDN_WIDTH), lambda bi, i: (bi, tmap(i), 1)),
            pl.BlockSpec((nb, tt, GDN_WIDTH), lambda bi, i: (bi, tmap(i), 2)),
            pl.BlockSpec((1, nb, tt, LANES), lambda bi, i: (d, bi, tmap(i), 0)),
            pl.BlockSpec((1, nb, nc, SUBLANES, H * C), lambda bi, i: (d, bi, tmap(i), 0, 0)),
        ]

    out_sds = jax.ShapeDtypeStruct((b, seq, GDN_WIDTH), F32)
    return pl.pallas_call(
        functools.partial(_gdn_scan_kernel, tt=tt, nb=nb),
        grid=(b // nb, nt),
        in_specs=dir_specs(0) + dir_specs(1),
        out_specs=[pl.BlockSpec((nb, tt, GDN_WIDTH), lambda bi, i: (bi, i, 0)),
                   pl.BlockSpec((nb, tt, GDN_WIDTH), lambda bi, i: (bi, nt - 1 - i, 0))],
        out_shape=[out_sds, out_sds],
        scratch_shapes=[
            pltpu.VMEM((nstream * H, dh, dh), F32),
            pltpu.VMEM((nc, nstream, C, H * dh), F32),
            pltpu.VMEM((nc, nstream, C, H * dh), F32),
            pltpu.VMEM((nc, nstream, dh + C, H * C), BF16),
        ],
        compiler_params=_cparams(("parallel", "arbitrary")),
        name="gdn_scan",
    )(qkvp, qkvp, qkvp, gates, gcr, qkvp, qkvp, qkvp, gates, gcr)


def _na_kernel(q_ref, k_ref, v_ref, bm_ref, g_ref, o_ref, *, rows):
    W = GRID_W
    band = NA_WIN_R * W
    hd = NA_HEAD_DIM
    lane_q = lax.broadcasted_iota(jnp.int32, (W, LANES), 1)
    first = lane_q < hd
    scale = jnp.asarray(hd ** -0.5, BF16)

    nt_dims = (((1,), (1,)), ((), ()))

    def row_group(gi, carry):
        rs = [gi * NA_ROWS_PER_STEP + j for j in range(NA_ROWS_PER_STEP)]
        r0s = [jnp.clip(r - NA_WIN_R // 2, 0, rows - NA_WIN_R) for r in rs]
        qs_l, kb_l, vb_l = [], [], []
        for r, r0 in zip(rs, r0s):
            q2 = q_ref[0, pl.ds(pl.multiple_of(r * W, W), W), :] * scale
            zero = jnp.zeros_like(q2)
            qs_l.append(jnp.concatenate([jnp.where(first, q2, zero), jnp.where(first, zero, q2)], axis=0))
            kb_l.append(k_ref[0, pl.ds(pl.multiple_of(r0 * W, W), band), :])
            vb_l.append(v_ref[0, pl.ds(pl.multiple_of(r0 * W, W), band), :])
        s_l = [lax.dot_general(qs, kb, nt_dims, preferred_element_type=F32) for qs, kb in zip(qs_l, kb_l)]
        s_l = [s + bm_ref[0, r - r0] for s, r, r0 in zip(s_l, rs, r0s)]
        m_l = [jnp.max(s, axis=-1, keepdims=True) for s in s_l]
        p_l = [jnp.exp(s - m) for s, m in zip(s_l, m_l)]
        l_l = [jnp.sum(p, axis=-1, keepdims=True) for p in p_l]
        pb_l = [p.astype(BF16) for p in p_l]
        ob_l = [jnp.dot(pb, vb, preferred_element_type=F32) for pb, vb in zip(pb_l, vb_l)]
        for r, ob, l in zip(rs, ob_l, l_l):
            o = jnp.where(first, ob[:W] / l[:W], ob[W:] / l[W:])
            sq = o * o
            ms0 = jnp.sum(jnp.where(first, sq, 0.0), axis=-1, keepdims=True)
            ms1 = jnp.sum(jnp.where(first, 0.0, sq), axis=-1, keepdims=True)
            ms = jnp.where(first, ms0, ms1) * (1.0 / hd)
            o_ref[0, pl.ds(pl.multiple_of(r * W, W), W), :] = (o * lax.rsqrt(ms + RMS_EPS) * g_ref[...]).astype(BF16)
        return carry

    lax.fori_loop(0, rows // NA_ROWS_PER_STEP, row_group, 0)


def _na(h_na, bias_tab, g_row):
    b, seq, _ = h_na.shape
    rows = seq // GRID_W
    assert rows >= NA_WIN_R
    npair = NA_WIDTH // LANES
    band = NA_WIN_R * GRID_W
    return pl.pallas_call(
        functools.partial(_na_kernel, rows=rows),
        grid=(b, npair),
        in_specs=[
            pl.BlockSpec((1, seq, LANES), lambda bi, p: (bi, 0, p)),
            pl.BlockSpec((1, seq, LANES), lambda bi, p: (bi, 0, npair + p)),
            pl.BlockSpec((1, seq, LANES), lambda bi, p: (bi, 0, 2 * npair + p)),
            pl.BlockSpec((1, NA_WIN_R, 2 * GRID_W, band), lambda bi, p: (p, 0, 0, 0)),
            pl.BlockSpec((1, LANES), lambda bi, p: (0, 0)),
        ],
        out_specs=pl.BlockSpec((1, seq, LANES), lambda bi, p: (bi, 0, p)),
        out_shape=jax.ShapeDtypeStruct((b, seq, NA_WIDTH), BF16),
        compiler_params=_cparams(("parallel", "parallel")),
        name="natten",
    )(h_na, h_na, h_na, bias_tab, g_row)


def _na_bias_table(rpb_l):
    W = GRID_W
    vi = np.arange(NA_WIN_R)
    kr = np.arange(NA_WIN_R)
    dr = kr[None, :] - vi[:, None] + NA_WIN_R - 1
    qc = np.arange(W)
    kc = np.arange(W)
    win_start = np.clip(qc - NA_WIN_C // 2, 0, W - NA_WIN_C)
    in_win = (kc[None, :] >= win_start[:, None]) & (kc[None, :] < win_start[:, None] + NA_WIN_C)
    dc = kc[None, :] - qc[:, None] + NA_WIN_C - 1
    rsel = (dr[:, :, None] == np.arange(2 * NA_WIN_R - 1)).astype(np.float32)
    csel = ((dc[:, :, None] == np.arange(2 * NA_WIN_C - 1)) & in_win[:, :, None]).astype(np.float32)
    rpb_pairs = rpb_l.astype(F32).reshape(NA_HEADS // 2, 2, 2 * NA_WIN_R - 1, 2 * NA_WIN_C - 1)
    tab = jnp.einsum("pjab,vka,qcb->pvjqkc", rpb_pairs, rsel, csel,
                     precision=lax.Precision.HIGHEST)
    tab = jnp.where(jnp.asarray(in_win)[:, None, :], tab, -jnp.inf)
    return tab.reshape(NA_HEADS // 2, NA_WIN_R, 2 * W, NA_WIN_R * W)


def _mix_ffn_kernel(x_ref, of_ref, ob_ref, z_ref, ona_ref, gg_ref, wo_ref, l1g_ref, l1b_ref,
                    w1_ref, b1_ref, w2_ref, b2_ref, l2g_ref, l2b_ref, out_ref):
    dh = GDN_HEAD_DIM
    tm = x_ref.shape[0]
    d_ff = w1_ref.shape[1]
    rows = [slice(s * (tm // FFN_SUBTILES), (s + 1) * (tm // FFN_SUBTILES)) for s in range(FFN_SUBTILES)]

    def gated(rs):
        o = of_ref[rs, :] + ob_ref[rs, :]
        z = z_ref[rs, :].astype(F32)
        gate = z * _sigmoid(z)
        parts = []
        for h in range(GDN_HEADS):
            cols = slice(h * dh, (h + 1) * dh)
            oh = o[:, cols]
            ms = jnp.mean(oh * oh, axis=-1, keepdims=True)
            parts.append((oh * lax.rsqrt(ms + RMS_EPS) * gg_ref[:, cols] * gate[:, cols]).astype(BF16))
        return jnp.concatenate(parts, axis=-1)

    og_l = [gated(rs) for rs in rows]
    mix_l = [jnp.dot(og, wo_ref[0:GDN_WIDTH, :], preferred_element_type=F32)
             + jnp.dot(ona_ref[rs, :], wo_ref[GDN_WIDTH:, :], preferred_element_type=F32)
             for og, rs in zip(og_l, rows)]
    x1_l = [_layer_norm(DEEPNORM_ALPHA * x_ref[rs, :] + mix, l1g_ref[...], l1b_ref[...])
            for mix, rs in zip(mix_l, rows)]
    x1b_l = [x1.astype(BF16) for x1 in x1_l]
    acc_l = [jnp.zeros(x1.shape, F32) for x1 in x1_l]
    for f in range(d_ff // FF_CHUNK):
        fs = slice(f * FF_CHUNK, (f + 1) * FF_CHUNK)
        hf_l = [jnp.dot(x1b, w1_ref[:, fs], preferred_element_type=F32) + b1_ref[:, fs] for x1b in x1b_l]
        hf_l = [jnp.square(jnp.maximum(hf, 0.0)).astype(BF16) for hf in hf_l]
        acc_l = [acc + jnp.dot(hf, w2_ref[fs, :], preferred_element_type=F32) for acc, hf in zip(acc_l, hf_l)]
    for rs, x1, acc in zip(rows, x1_l, acc_l):
        y = DEEPNORM_ALPHA * x1 + (acc + b2_ref[...])
        out_ref[rs, :] = _layer_norm(y, l2g_ref[...], l2b_ref[...])


def _mix_ffn(x2d, o_f, o_b, z, ona, gg, wo, l1g, l1b, w1, b1, w2, b2, l2g, l2b):
    bt, dm = x2d.shape
    d_ff = w1.shape[1]
    tm = TM_FFN
    row = lambda i: (i, 0)
    return pl.pallas_call(
        _mix_ffn_kernel,
        grid=(bt // tm,),
        in_specs=[
            pl.BlockSpec((tm, dm), row),
            pl.BlockSpec((tm, GDN_WIDTH), row),
            pl.BlockSpec((tm, GDN_WIDTH), row),
            pl.BlockSpec((tm, GDN_WIDTH), row),
            pl.BlockSpec((tm, NA_WIDTH), row),
            _const_spec((1, GDN_WIDTH)),
            _const_spec((dm, dm)),
            _const_spec((1, dm)), _const_spec((1, dm)),
            _const_spec((dm, d_ff)), _const_spec((1, d_ff)),
            _const_spec((d_ff, dm)), _const_spec((1, dm)),
            _const_spec((1, dm)), _const_spec((1, dm)),
        ],
        out_specs=pl.BlockSpec((tm, dm), row),
        out_shape=jax.ShapeDtypeStruct((bt, dm), F32),
        compiler_params=_cparams(("parallel",)),
        name="mix_ffn",
    )(x2d, o_f, o_b, z, ona, gg, wo, l1g, l1b, w1, b1, w2, b2, l2g, l2b)


def _split_w_in(w_l):
    dm = w_l.shape[0]
    h = GDN_HEADS
    a0 = 4 * GDN_WIDTH
    b0 = a0 + 2 * h
    na0 = b0 + 2 * h
    pad = jnp.zeros((dm, LANES - 2 * h), w_l.dtype)
    gates = jnp.concatenate([w_l[:, a0:a0 + h], w_l[:, b0:b0 + h], pad,
                             w_l[:, a0 + h:a0 + 2 * h], w_l[:, b0 + h:b0 + 2 * h], pad], axis=1)
    return tuple(w.astype(BF16) for w in (w_l[:, :C_QKV], w_l[:, C_QKV:a0], gates, w_l[:, na0:]))


def _gate_rows(p):
    return jnp.pad(p.astype(F32), ((0, 0), (0, LANES - p.shape[1])))


def kernel(x, ln_in_g, ln_in_b, w_in, conv_w, a_log, dt_bias, gdn_norm_g, rpb, na_norm_g, w_out,
           ln1_g, ln1_b, w1, b1, w2, b2, ln2_g, ln2_b):
    B, T, dm = x.shape
    bt = B * T
    nchunks = T // GDN_CHUNK
    row = lambda v: v.reshape(1, -1).astype(F32)
    xs = x.reshape(bt, dm)
    for l in range(DEPTH):
        w_l = _split_w_in(w_in[l])
        conv8 = jnp.pad(conv_w[l].astype(F32), ((0, SUBLANES - CONV_WIDTH), (0, 0)))
        outs = _inproj(xs, row(ln_in_g), row(ln_in_b), w_l, conv8, _gate_rows(a_log[l]), _gate_rows(dt_bias[l]),
                       apply_ln=(l == 0), seq=T)
        if l == 0:
            xs, qkvp, z, gates, gcr, h_na = outs
        else:
            qkvp, z, gates, gcr, h_na = outs
        qkvp = qkvp.reshape(B, T, C_QKV)
        gates4 = gates.reshape(2, B, T, LANES)
        gcr = gcr.reshape(2, B, nchunks, SUBLANES, GDN_HEADS * GDN_CHUNK)
        o_f, o_b = _gdn_scan(qkvp, gates4, gcr)
        ona = _na(h_na.reshape(B, T, C_NA), _na_bias_table(rpb[l]),
                  jnp.tile(na_norm_g[l].astype(F32), LANES // NA_HEAD_DIM).reshape(1, LANES))
        xs = _mix_ffn(xs, o_f.reshape(bt, GDN_WIDTH), o_b.reshape(bt, GDN_WIDTH), z, ona.reshape(bt, NA_WIDTH),
                      jnp.tile(gdn_norm_g[l].astype(F32), GDN_HEADS).reshape(1, GDN_WIDTH),
                      w_out[l].astype(BF16), row(ln1_g[l]), row(ln1_b[l]),
                      w1[l].astype(BF16), row(b1[l]), w2[l].astype(BF16), row(b2[l]),
                      row(ln2_g[l]), row(ln2_b[l]))
    return xs.reshape(B, T, dm)
```

```python
import functools

import jax
import jax.numpy as jnp
import numpy as np
from jax import lax
from jax.experimental import pallas as pl
from jax.experimental.pallas import tpu as pltpu

F32 = jnp.float32
BF16 = jnp.bfloat16

GRID_W = 64
GDN_HEAD_DIM = 128
GDN_HEADS = 4
GDN_WIDTH = GDN_HEADS * GDN_HEAD_DIM
NA_HEAD_DIM = 64
NA_HEADS = 8
NA_WIDTH = NA_HEADS * NA_HEAD_DIM
CONV_WIDTH = 5
GDN_CHUNK = 64
NA_WIN_R = 8
NA_WIN_C = 16
DEPTH = 2
DEEPNORM_ALPHA = (2 * DEPTH) ** 0.25
LN_EPS = 1e-5
RMS_EPS = 1e-6

LANES = 128
SUBLANES = 8
VMEM_LIMIT = 56 * 1024 * 1024

TM_PROJ = 512
TM_FFN = 512
TT_SCAN = 256
FF_CHUNK = 2048
NA_ROWS_PER_STEP = 8
FFN_SUBTILES = 2
GDN_BATCH_PER_STEP = 4
GDN_PREP_CHUNKS = 2

C_QKV = 3 * GDN_WIDTH
C_Z = GDN_WIDTH
C_NA = 3 * NA_WIDTH


def _cparams(sem):
    return pltpu.CompilerParams(dimension_semantics=sem, vmem_limit_bytes=VMEM_LIMIT)


def _const_spec(shape):
    nd = len(shape)
    return pl.BlockSpec(shape, lambda *_: (0,) * nd, pipeline_mode=pl.Buffered(1))


def _layer_norm(y, g, b):
    mu = jnp.mean(y, axis=-1, keepdims=True)
    yc = y - mu
    var = jnp.mean(yc * yc, axis=-1, keepdims=True)
    return yc * lax.rsqrt(var + LN_EPS) * g + b


def _sigmoid(x):
    return 1.0 / (1.0 + jnp.exp(-x))


def _chunk_cumsum(g, tpos, reverse):
    n = g.shape[0]
    s = 1
    while s < GDN_CHUNK:
        if reverse:
            g = g + jnp.where(tpos < GDN_CHUNK - s, pltpu.roll(g, n - s, 0), 0.0)
        else:
            g = g + jnp.where(tpos >= s, pltpu.roll(g, s, 0), 0.0)
        s *= 2
    return g


CONV_HALO = SUBLANES


def _inproj_kernel(x_ref, xp_ref, xn_ref, lng_ref, lnb_ref, wq_ref, wz_ref, wg_ref, wna_ref, cw_ref, alog_ref, dtb_ref,
                   *refs, apply_ln, tiles_per_seq):
    if apply_ln:
        xo_ref, qkv_ref, z_ref, gates_ref, gcr_ref, na_ref, hq_ref = refs
    else:
        qkv_ref, z_ref, gates_ref, gcr_ref, na_ref, hq_ref = refs
    i = pl.program_id(0)
    tm = x_ref.shape[0]
    halo = CONV_HALO
    x, x_prev, x_next = x_ref[...], xp_ref[...], xn_ref[...]
    if apply_ln:
        x = _layer_norm(x, lng_ref[...], lnb_ref[...])
        x_prev = _layer_norm(x_prev, lng_ref[...], lnb_ref[...])
        x_next = _layer_norm(x_next, lng_ref[...], lnb_ref[...])
        xo_ref[...] = x
    xb = x.astype(BF16)
    x_ext = jnp.concatenate([x_prev, x, x_next], axis=0).astype(BF16)

    pos = i % tiles_per_seq
    keep_prev = jnp.where(pos == 0, 0.0, 1.0).astype(F32)
    keep_next = jnp.where(pos == tiles_per_seq - 1, 0.0, 1.0).astype(F32)
    hq = jnp.dot(x_ext, wq_ref[...], preferred_element_type=F32)
    hq_ref[0:halo, :] = hq[0:halo] * keep_prev
    hq_ref[halo:tm + halo, :] = hq[halo:tm + halo]
    hq_ref[tm + halo:tm + 2 * halo, :] = hq[tm + halo:tm + 2 * halo] * keep_next
    z_ref[...] = jnp.dot(xb, wz_ref[...], preferred_element_type=F32).astype(BF16)
    na_ref[...] = jnp.dot(xb, wna_ref[...], preferred_element_type=F32).astype(BF16)
    hab2 = jnp.dot(xb, wg_ref[...], preferred_element_type=F32)

    first_tap = halo - CONV_WIDTH // 2
    for j in range(C_QKV // LANES):
        cols = slice(j * LANES, (j + 1) * LANES)
        y = hq_ref[first_tap:first_tap + tm, cols] * cw_ref[0:1, cols]
        for tap in range(1, CONV_WIDTH):
            y = y + hq_ref[first_tap + tap:first_tap + tap + tm, cols] * cw_ref[tap:tap + 1, cols]
        y = y * _sigmoid(y)
        if j < 2 * GDN_HEADS:
            fac = lax.rsqrt(jnp.sum(y * y, axis=-1, keepdims=True) + RMS_EPS)
            if j < GDN_HEADS:
                fac = fac * (GDN_HEAD_DIM ** -0.5)
            y = y * fac
        qkv_ref[:, cols] = y.astype(BF16)

    tpos = lax.broadcasted_iota(jnp.int32, (tm, LANES), 0) & (GDN_CHUNK - 1)
    lane = lax.broadcasted_iota(jnp.int32, (tm, LANES), 1)
    for d in range(2):
        hab = hab2[:, d * LANES:(d + 1) * LANES]
        sp_in = hab + dtb_ref[d:d + 1, :]
        softplus = jnp.maximum(sp_in, 0.0) + jnp.log1p(jnp.exp(-jnp.abs(sp_in)))
        g = -jnp.exp(alog_ref[d:d + 1, :]) * softplus
        g = jnp.where(lane < GDN_HEADS, g, 0.0)
        gc = _chunk_cumsum(g, tpos, reverse=(d == 1))
        gates_ref[d] = jnp.where(lane < GDN_HEADS, gc, _sigmoid(hab))
        gct = gc.T
        pad_rows = jnp.zeros((SUBLANES - 1, GDN_HEADS * GDN_CHUNK), F32)
        for c in range(tm // GDN_CHUNK):
            row_c = jnp.concatenate([gct[h:h + 1, c * GDN_CHUNK:(c + 1) * GDN_CHUNK] for h in range(GDN_HEADS)], axis=1)
            gcr_ref[d, c] = jnp.concatenate([row_c, pad_rows], axis=0)


def _inproj(x2d, lng, lnb, ws, conv_w8, alog_rows, dtb_rows, apply_ln, seq):
    bt, dm = x2d.shape
    tm = TM_PROJ
    grid = (bt // tm,)
    row = lambda i: (i, 0)
    hb = tm // CONV_HALO
    last_hb = bt // CONV_HALO - 1
    in_specs = [
        pl.BlockSpec((tm, dm), row),
        pl.BlockSpec((CONV_HALO, dm), lambda i: (jnp.maximum(i * hb - 1, 0), 0)),
        pl.BlockSpec((CONV_HALO, dm), lambda i: (jnp.minimum((i + 1) * hb, last_hb), 0)),
        _const_spec((1, dm)), _const_spec((1, dm)),
        _const_spec((dm, C_QKV)), _const_spec((dm, C_Z)), _const_spec((dm, 2 * LANES)), _const_spec((dm, C_NA)),
        _const_spec((SUBLANES, C_QKV)),
        _const_spec((2, LANES)), _const_spec((2, LANES)),
    ]
    out_shape = [
        jax.ShapeDtypeStruct((bt, C_QKV), BF16),
        jax.ShapeDtypeStruct((bt, C_Z), BF16),
        jax.ShapeDtypeStruct((2, bt, LANES), F32),
        jax.ShapeDtypeStruct((2, bt // GDN_CHUNK, SUBLANES, GDN_HEADS * GDN_CHUNK), F32),
        jax.ShapeDtypeStruct((bt, C_NA), BF16),
    ]
    out_specs = [
        pl.BlockSpec((tm, C_QKV), row),
        pl.BlockSpec((tm, C_Z), row),
        pl.BlockSpec((2, tm, LANES), lambda i: (0, i, 0)),
        pl.BlockSpec((2, tm // GDN_CHUNK, SUBLANES, GDN_HEADS * GDN_CHUNK), lambda i: (0, i, 0, 0)),
        pl.BlockSpec((tm, C_NA), row),
    ]
    if apply_ln:
        out_shape = [jax.ShapeDtypeStruct((bt, dm), F32)] + out_shape
        out_specs = [pl.BlockSpec((tm, dm), row)] + out_specs
    return pl.pallas_call(
        functools.partial(_inproj_kernel, apply_ln=apply_ln, tiles_per_seq=seq // tm),
        grid=grid, in_specs=in_specs, out_specs=out_specs, out_shape=out_shape,
        scratch_shapes=[pltpu.VMEM((tm + 2 * CONV_HALO, C_QKV), F32)],
        compiler_params=_cparams(("parallel",)),
        name="inproj_ln" if apply_ln else "inproj",
    )(x2d, x2d, x2d, lng, lnb, *ws, conv_w8, alog_rows, dtb_rows)


def _block_diag(x, nblk):
    w = x.shape[1] // nblk
    blk = lax.broadcasted_iota(jnp.int32, x.shape, 1) // w
    zero = jnp.zeros_like(x)
    return jnp.concatenate([jnp.where(blk == h, x, zero) for h in range(nblk)], axis=0)


def _lane_blocks(cols, width):
    r, n = cols.shape
    return jnp.concatenate([jnp.broadcast_to(cols[:, j:j + 1], (r, width)) for j in range(n)], axis=1)


def _unit_tri_inverses(mats, nblk):
    c = mats[0].shape[0]
    ri = lax.broadcasted_iota(jnp.int32, (c, nblk * c), 0)
    ci = lax.broadcasted_iota(jnp.int32, (c, nblk * c), 1) % c
    eye = (ri == ci).astype(F32)
    ps = [-a for a in mats]
    xs = [eye + p for p in ps]
    pbs = [p.astype(BF16) for p in ps]
    ps = [jnp.dot(pb, _block_diag(pb, nblk), preferred_element_type=F32) for pb in pbs]
    m = 2
    while 2 * m < c:
        pbs = [p.astype(BF16) for p in ps]
        prods = [jnp.dot(jnp.concatenate([x.astype(BF16), pb], axis=0), _block_diag(pb, nblk),
                         preferred_element_type=F32) for x, pb in zip(xs, pbs)]
        xs = [x + prod[:c] for x, prod in zip(xs, prods)]
        ps = [prod[c:] for prod in prods]
        m *= 2
    return [x + jnp.dot(x.astype(BF16), _block_diag(p.astype(BF16), nblk), preferred_element_type=F32)
            for x, p in zip(xs, ps)]


def _gdn_scan_kernel(qf_ref, kf_ref, vf_ref, gf_ref, grf_ref, qb_ref, kb_ref, vb_ref, gb_ref, grb_ref,
                     of_ref, ob_ref, s_ref, en_ref, bn_ref, wq_ref, *, tt, nb):
    i = pl.program_id(1)
    nc = tt // GDN_CHUNK
    C = GDN_CHUNK
    dh = GDN_HEAD_DIM
    H = GDN_HEADS

    @pl.when(i == 0)
    def _():
        s_ref[...] = jnp.zeros_like(s_ref)

    ri = lax.broadcasted_iota(jnp.int32, (C, H * C), 0)
    ci = lax.broadcasted_iota(jnp.int32, (C, H * C), 1) % C
    mask_incl = (ri >= ci, ri <= ci)
    mask_strict = (ri > ci, ri < ci)
    dirs = ((qf_ref, kf_ref, vf_ref, gf_ref, grf_ref, of_ref), (qb_ref, kb_ref, vb_ref, gb_ref, grb_ref, ob_ref))
    streams = [(b, d) for b in range(nb) for d in range(2)]
    half_lo = lax.broadcasted_iota(jnp.int32, (C, dh), 1) < C
    nt_dims = (((1,), (1,)), ((), ()))
    tn_dims = (((0,), (0,)), ((), ()))

    def chunk_end_decay(gt, d):
        return gt[C - 1:C, :] if d == 0 else gt[0:1, :]

    def halve_blocks(x):
        lo = half_lo[0:x.shape[0]]
        return jnp.concatenate([jnp.where(lo, x[:, (2 * p) * dh:(2 * p + 1) * dh], x[:, (2 * p + 1) * dh:(2 * p + 2) * dh])
                                for p in range(H // 2)], axis=1)

    def prep_group(cg, carry):
        items = [(cg * GDN_PREP_CHUNKS + j, u) for j in range(GDN_PREP_CHUNKS) for u in range(len(streams))]
        lhs_l, bdk_l, kst_l, gam_l, gamb_l, edec_l = [], [], [], [], [], []
        for c, u in items:
            b, d = streams[u]
            q_ref, k_ref, _, g_ref, gr_ref, _ = dirs[d]
            r0 = pl.multiple_of(c * C, C)
            gt = g_ref[0, b, pl.ds(r0, C), :]
            gr = gr_ref[0, b, c]
            q = q_ref[b, pl.ds(r0, C), :]
            k = k_ref[b, pl.ds(r0, C), :]
            gc_n = _lane_blocks(gt[:, 0:H], dh)
            beta_n = _lane_blocks(gt[:, H:2 * H], dh)
            gend_n = _lane_blocks(chunk_end_decay(gt, d)[:, 0:H], dh)
            en_ref[c, u] = jnp.exp(gc_n)
            bn_ref[c, u] = beta_n
            gc_c = halve_blocks(gc_n)
            gam = jnp.where(mask_incl[d], jnp.exp(gc_c - gr[0:1, :]), 0.0)
            gam_l.append(gam)
            gamb_l.append(gam * halve_blocks(beta_n))
            edec_l.append(jnp.exp(halve_blocks(gend_n) - gc_c))
            lhs_l.append(jnp.concatenate([k, q], axis=0))
            bdk_l.append(_block_diag(k, H))
            kst_l.append(jnp.concatenate([k[:, h * dh:(h + 1) * dh] for h in range(H)], axis=0))
        kq_l = [lax.dot_general(lhs, bdk, nt_dims, preferred_element_type=F32)
                for lhs, bdk in zip(lhs_l, bdk_l)]
        a_l = [jnp.where(mask_strict[streams[u][1]], kq[:C] * gamb, 0.0)
               for kq, gamb, (_, u) in zip(kq_l, gamb_l, items)]
        qk_l = [(kq[C:] * gam).astype(BF16) for kq, gam in zip(kq_l, gam_l)]
        tinv_l = _unit_tri_inverses(a_l, H)
        bdt_l = [_block_diag(t.astype(BF16), H) for t in tinv_l]
        bdte_l = [_block_diag((t * edec).astype(BF16), H) for t, edec in zip(tinv_l, edec_l)]
        wc_l = [lax.dot_general(kst, bdte, tn_dims, preferred_element_type=F32)
                for kst, bdte in zip(kst_l, bdte_l)]
        qt_l = [jnp.dot(qk, bdt, preferred_element_type=F32) for qk, bdt in zip(qk_l, bdt_l)]
        for (c, u), wc, qt in zip(items, wc_l, qt_l):
            wq_ref[c, u] = jnp.concatenate([wc, qt], axis=0).astype(BF16)
        return carry

    lax.fori_loop(0, nc // GDN_PREP_CHUNKS, prep_group, 0)

    def scan_step(step, carry):
        cs = (step, nc - 1 - step)
        egs, kq_in, v_in = [], [], []
        for u, (b, d) in enumerate(streams):
            q_ref, k_ref, v_ref, g_ref = dirs[d][:4]
            r0 = pl.multiple_of(cs[d] * C, C)
            egs.append(jnp.exp(chunk_end_decay(g_ref[0, b, pl.ds(r0, C), :], d)))
            kq_in.append(jnp.concatenate([k_ref[b, pl.ds(r0, C), :], q_ref[b, pl.ds(r0, C), :]], axis=0))
            v_in.append(v_ref[b, pl.ds(r0, C), :])
        s_l = [[s_ref[u * H + h] for h in range(H)] for u in range(len(streams))]
        x1_l = [jnp.concatenate([jnp.dot(kq_in[u][:, h * dh:(h + 1) * dh], s_l[u][h].astype(BF16),
                                         preferred_element_type=F32) for h in range(H)], axis=1)
                for u in range(len(streams))]
        res_l, qs_l = [], []
        for u, (b, d) in enumerate(streams):
            en = en_ref[cs[d], u]
            res_l.append((bn_ref[cs[d], u] * (v_in[u].astype(F32) - x1_l[u][:C] * en)).astype(BF16))
            qs_l.append(x1_l[u][C:] * en)
        z_l = [jnp.dot(wq_ref[cs[d], u], _block_diag(res, H), preferred_element_type=F32)
               for (u, (b, d)), res in zip(enumerate(streams), res_l)]
        for u, (b, d) in enumerate(streams):
            z = z_l[u]
            for h in range(H):
                s_ref[u * H + h] = s_l[u][h] * egs[u][:, h:h + 1] + z[:dh, h * dh:(h + 1) * dh]
            r0 = pl.multiple_of(cs[d] * C, C)
            dirs[d][5][b, pl.ds(r0, C), :] = qs_l[u] + z[dh:]
        return carry

    lax.fori_loop(0, nc, scan_step, 0)


def _gdn_scan(qkvp, gates, gcr):
    b, seq, _ = qkvp.shape
    tt = TT_SCAN
    nb = GDN_BATCH_PER_STEP if b % GDN_BATCH_PER_STEP == 0 else 1
    nt = seq // tt
    nc = tt // GDN_CHUNK
    C = GDN_CHUNK
    dh = GDN_HEAD_DIM
    H = GDN_HEADS
    nstream = 2 * nb

    def dir_specs(d):
        tmap = (lambda i: i) if d == 0 else (lambda i: nt - 1 - i)
        return [
            pl.BlockSpec((nb, tt, GDN_WIDTH), lambda bi, i: (bi, tmap(i), 0)),
            pl.BlockSpec((nb, tt, GDN_WIDTH), lambda bi, i: (bi, tmap(i), 1)),
            pl.BlockSpec((nb, tt, GDN_WIDTH), lambda bi, i: (bi, tmap(i), 2)),
            pl.BlockSpec((1, nb, tt, LANES), lambda bi, i: (d, bi, tmap(i), 0)),
            pl.BlockSpec((1, nb, nc, SUBLANES, H * C), lambda bi, i: (d, bi, tmap(i), 0, 0)),
        ]

    out_sds = jax.ShapeDtypeStruct((b, seq, GDN_WIDTH), F32)
    return pl.pallas_call(
        functools.partial(_gdn_scan_kernel, tt=tt, nb=nb),
        grid=(b // nb, nt),
        in_specs=dir_specs(0) + dir_specs(1),
        out_specs=[pl.BlockSpec((nb, tt, GDN_WIDTH), lambda bi, i: (bi, i, 0)),
                   pl.BlockSpec((nb, tt, GDN_WIDTH), lambda bi, i: (bi, nt - 1 - i, 0))],
        out_shape=[out_sds, out_sds],
        scratch_shapes=[
            pltpu.VMEM((nstream * H, dh, dh), F32),
            pltpu.VMEM((nc, nstream, C, H * dh), F32),
            pltpu.VMEM((nc, nstream, C, H * dh), F32),
            pltpu.VMEM((nc, nstream, dh + C, H * C), BF16),
        ],
        compiler_params=_cparams(("parallel", "arbitrary")),
        name="gdn_scan",
    )(qkvp, qkvp, qkvp, gates, gcr, qkvp, qkvp, qkvp, gates, gcr)


def _na_kernel(q_ref, k_ref, v_ref, bm_ref, g_ref, o_ref, *, rows):
    W = GRID_W
    band = NA_WIN_R * W
    hd = NA_HEAD_DIM
    lane_q = lax.broadcasted_iota(jnp.int32, (W, LANES), 1)
    first = lane_q < hd
    scale = jnp.asarray(hd ** -0.5, BF16)

    nt_dims = (((1,), (1,)), ((), ()))

    def row_group(gi, carry):
        rs = [gi * NA_ROWS_PER_STEP + j for j in range(NA_ROWS_PER_STEP)]
        r0s = [jnp.clip(r - NA_WIN_R // 2, 0, rows - NA_WIN_R) for r in rs]
        qs_l, kb_l, vb_l = [], [], []
        for r, r0 in zip(rs, r0s):
            q2 = q_ref[0, pl.ds(pl.multiple_of(r * W, W), W), :] * scale
            zero = jnp.zeros_like(q2)
            qs_l.append(jnp.concatenate([jnp.where(first, q2, zero), jnp.where(first, zero, q2)], axis=0))
            kb_l.append(k_ref[0, pl.ds(pl.multiple_of(r0 * W, W), band), :])
            vb_l.append(v_ref[0, pl.ds(pl.multiple_of(r0 * W, W), band), :])
        s_l = [lax.dot_general(qs, kb, nt_dims, preferred_element_type=F32) for qs, kb in zip(qs_l, kb_l)]
        s_l = [s + bm_ref[0, r - r0] for s, r, r0 in zip(s_l, rs, r0s)]
        m_l = [jnp.max(s, axis=-1, keepdims=True) for s in s_l]
        p_l = [jnp.exp(s - m) for s, m in zip(s_l, m_l)]
        l_l = [jnp.sum(p, axis=-1, keepdims=True) for p in p_l]
        pb_l = [p.astype(BF16) for p in p_l]
        ob_l = [jnp.dot(pb, vb, preferred_element_type=F32) for pb, vb in zip(pb_l, vb_l)]
        for r, ob, l in zip(rs, ob_l, l_l):
            o = jnp.where(first, ob[:W] / l[:W], ob[W:] / l[W:])
            sq = o * o
            ms0 = jnp.sum(jnp.where(first, sq, 0.0), axis=-1, keepdims=True)
            ms1 = jnp.sum(jnp.where(first, 0.0, sq), axis=-1, keepdims=True)
            ms = jnp.where(first, ms0, ms1) * (1.0 / hd)
            o_ref[0, pl.ds(pl.multiple_of(r * W, W), W), :] = (o * lax.rsqrt(ms + RMS_EPS) * g_ref[...]).astype(BF16)
        return carry

    lax.fori_loop(0, rows // NA_ROWS_PER_STEP, row_group, 0)


def _na(h_na, bias_tab, g_row):
    b, seq, _ = h_na.shape
    rows = seq // GRID_W
    assert rows >= NA_WIN_R
    npair = NA_WIDTH // LANES
    band = NA_WIN_R * GRID_W
    return pl.pallas_call(
        functools.partial(_na_kernel, rows=rows),
        grid=(b, npair),
        in_specs=[
            pl.BlockSpec((1, seq, LANES), lambda bi, p: (bi, 0, p)),
            pl.BlockSpec((1, seq, LANES), lambda bi, p: (bi, 0, npair + p)),
            pl.BlockSpec((1, seq, LANES), lambda bi, p: (bi, 0, 2 * npair + p)),
            pl.BlockSpec((1, NA_WIN_R, 2 * GRID_W, band), lambda bi, p: (p, 0, 0, 0)),
            pl.BlockSpec((1, LANES), lambda bi, p: (0, 0)),
        ],
        out_specs=pl.BlockSpec((1, seq, LANES), lambda bi, p: (bi, 0, p)),
        out_shape=jax.ShapeDtypeStruct((b, seq, NA_WIDTH), BF16),
        compiler_params=_cparams(("parallel", "parallel")),
        name="natten",
    )(h_na, h_na, h_na, bias_tab, g_row)


def _na_bias_table(rpb_l):
    W = GRID_W
    vi = np.arange(NA_WIN_R)
    kr = np.arange(NA_WIN_R)
    dr = kr[None, :] - vi[:, None] + NA_WIN_R - 1
    qc = np.arange(W)
    kc = np.arange(W)
    win_start = np.clip(qc - NA_WIN_C // 2, 0, W - NA_WIN_C)
    in_win = (kc[None, :] >= win_start[:, None]) & (kc[None, :] < win_start[:, None] + NA_WIN_C)
    dc = kc[None, :] - qc[:, None] + NA_WIN_C - 1
    rsel = (dr[:, :, None] == np.arange(2 * NA_WIN_R - 1)).astype(np.float32)
    csel = ((dc[:, :, None] == np.arange(2 * NA_WIN_C - 1)) & in_win[:, :, None]).astype(np.float32)
    rpb_pairs = rpb_l.astype(F32).reshape(NA_HEADS // 2, 2, 2 * NA_WIN_R - 1, 2 * NA_WIN_C - 1)
    tab = jnp.einsum("pjab,vka,qcb->pvjqkc", rpb_pairs, rsel, csel,
                     precision=lax.Precision.HIGHEST)
    tab = jnp.where(jnp.asarray(in_win)[:, None, :], tab, -jnp.inf)
    return tab.reshape(NA_HEADS // 2, NA_WIN_R, 2 * W, NA_WIN_R * W)


def _mix_ffn_kernel(x_ref, of_ref, ob_ref, z_ref, ona_ref, gg_ref, wo_ref, l1g_ref, l1b_ref,
                    w1_ref, b1_ref, w2_ref, b2_ref, l2g_ref, l2b_ref, out_ref):
    dh = GDN_HEAD_DIM
    tm = x_ref.shape[0]
    d_ff = w1_ref.shape[1]
    rows = [slice(s * (tm // FFN_SUBTILES), (s + 1) * (tm // FFN_SUBTILES)) for s in range(FFN_SUBTILES)]

    def gated(rs):
        o = of_ref[rs, :] + ob_ref[rs, :]
        z = z_ref[rs, :].astype(F32)
        gate = z * _sigmoid(z)
        parts = []
        for h in range(GDN_HEADS):
            cols = slice(h * dh, (h + 1) * dh)
            oh = o[:, cols]
            ms = jnp.mean(oh * oh, axis=-1, keepdims=True)
            parts.append((oh * lax.rsqrt(ms + RMS_EPS) * gg_ref[:, cols] * gate[:, cols]).astype(BF16))
        return jnp.concatenate(parts, axis=-1)

    og_l = [gated(rs) for rs in rows]
    mix_l = [jnp.dot(og, wo_ref[0:GDN_WIDTH, :], preferred_element_type=F32)
             + jnp.dot(ona_ref[rs, :], wo_ref[GDN_WIDTH:, :], preferred_element_type=F32)
             for og, rs in zip(og_l, rows)]
    x1_l = [_layer_norm(DEEPNORM_ALPHA * x_ref[rs, :] + mix, l1g_ref[...], l1b_ref[...])
            for mix, rs in zip(mix_l, rows)]
    x1b_l = [x1.astype(BF16) for x1 in x1_l]
    acc_l = [jnp.zeros(x1.shape, F32) for x1 in x1_l]
    for f in range(d_ff // FF_CHUNK):
        fs = slice(f * FF_CHUNK, (f + 1) * FF_CHUNK)
        hf_l = [jnp.dot(x1b, w1_ref[:, fs], preferred_element_type=F32) + b1_ref[:, fs] for x1b in x1b_l]
        hf_l = [jnp.square(jnp.maximum(hf, 0.0)).astype(BF16) for hf in hf_l]
        acc_l = [acc + jnp.dot(hf, w2_ref[fs, :], preferred_element_type=F32) for acc, hf in zip(acc_l, hf_l)]
    for rs, x1, acc in zip(rows, x1_l, acc_l):
        y = DEEPNORM_ALPHA * x1 + (acc + b2_ref[...])
        out_ref[rs, :] = _layer_norm(y, l2g_ref[...], l2b_ref[...])


def _mix_ffn(x2d, o_f, o_b, z, ona, gg, wo, l1g, l1b, w1, b1, w2, b2, l2g, l2b):
    bt, dm = x2d.shape
    d_ff = w1.shape[1]
    tm = TM_FFN
    row = lambda i: (i, 0)
    return pl.pallas_call(
        _mix_ffn_kernel,
        grid=(bt // tm,),
        in_specs=[
            pl.BlockSpec((tm, dm), row),
            pl.BlockSpec((tm, GDN_WIDTH), row),
            pl.BlockSpec((tm, GDN_WIDTH), row),
            pl.BlockSpec((tm, GDN_WIDTH), row),
            pl.BlockSpec((tm, NA_WIDTH), row),
            _const_spec((1, GDN_WIDTH)),
            _const_spec((dm, dm)),
            _const_spec((1, dm)), _const_spec((1, dm)),
            _const_spec((dm, d_ff)), _const_spec((1, d_ff)),
            _const_spec((d_ff, dm)), _const_spec((1, dm)),
            _const_spec((1, dm)), _const_spec((1, dm)),
        ],
        out_specs=pl.BlockSpec((tm, dm), row),
        out_shape=jax.ShapeDtypeStruct((bt, dm), F32),
        compiler_params=pltpu.CompilerParams(
            dimension_semantics=("parallel",), vmem_limit_bytes=VMEM_LIMIT,
            allow_input_fusion=[i in (5, 8, 10) for i in range(15)]),
        name="mix_ffn",
    )(x2d, o_f, o_b, z, ona, gg, wo, l1g, l1b, w1, b1, w2, b2, l2g, l2b)


def _split_w_in(w_l):
    dm = w_l.shape[0]
    h = GDN_HEADS
    a0 = 4 * GDN_WIDTH
    b0 = a0 + 2 * h
    na0 = b0 + 2 * h
    pad = jnp.zeros((dm, LANES - 2 * h), w_l.dtype)
    gates = jnp.concatenate([w_l[:, a0:a0 + h], w_l[:, b0:b0 + h], pad,
                             w_l[:, a0 + h:a0 + 2 * h], w_l[:, b0 + h:b0 + 2 * h], pad], axis=1)
    return tuple(w.astype(BF16) for w in (w_l[:, :C_QKV], w_l[:, C_QKV:a0], gates, w_l[:, na0:]))


def _gate_rows(p):
    return jnp.pad(p.astype(F32), ((0, 0), (0, LANES - p.shape[1])))


def kernel(x, ln_in_g, ln_in_b, w_in, conv_w, a_log, dt_bias, gdn_norm_g, rpb, na_norm_g, w_out,
           ln1_g, ln1_b, w1, b1, w2, b2, ln2_g, ln2_b):
    B, T, dm = x.shape
    bt = B * T
    nchunks = T // GDN_CHUNK
    row = lambda v: v.reshape(1, -1).astype(F32)
    xs = x.reshape(bt, dm)
    for l in range(DEPTH):
        w_l = _split_w_in(w_in[l])
        conv8 = jnp.pad(conv_w[l].astype(F32), ((0, SUBLANES - CONV_WIDTH), (0, 0)))
        outs = _inproj(xs, row(ln_in_g), row(ln_in_b), w_l, conv8, _gate_rows(a_log[l]), _gate_rows(dt_bias[l]),
                       apply_ln=(l == 0), seq=T)
        if l == 0:
            xs, qkvp, z, gates, gcr, h_na = outs
        else:
            qkvp, z, gates, gcr, h_na = outs
        qkvp = qkvp.reshape(B, T, C_QKV)
        gates4 = gates.reshape(2, B, T, LANES)
        gcr = gcr.reshape(2, B, nchunks, SUBLANES, GDN_HEADS * GDN_CHUNK)
        o_f, o_b = _gdn_scan(qkvp, gates4, gcr)
        ona = _na(h_na.reshape(B, T, C_NA), _na_bias_table(rpb[l]),
                  jnp.tile(na_norm_g[l].astype(F32), LANES // NA_HEAD_DIM).reshape(1, LANES))
        xs = _mix_ffn(xs, o_f.reshape(bt, GDN_WIDTH), o_b.reshape(bt, GDN_WIDTH), z, ona.reshape(bt, NA_WIDTH),
                      jnp.tile(gdn_norm_g[l].astype(F32), GDN_HEADS).reshape(1, GDN_WIDTH),
                      w_out[l].astype(BF16), row(ln1_g[l]), row(ln1_b[l]),
                      w1[l].astype(BF16), row(b1[l]), w2[l].astype(BF16), row(b2[l]),
                      row(ln2_g[l]), row(ln2_b[l]))
    return xs.reshape(B, T, dm)
```
